```python
import jax
import jax.numpy as jnp
from jax import lax
import numpy as np

D_MODEL = 2048
BATCH = 8
SEQ = 2048
DEPTH = 1

HEAD_DIM = 128
MOBA_HEADS = 8
MOBA_WIDTH = MOBA_HEADS * HEAD_DIM
MOBA_BLOCK = 256
MOBA_TOPK = 3
MOBA_QROWS = 128
ROPE_THETA = 500000.0
ROPE_DIM = HEAD_DIM // 4

DN_HEADS = 8
DN_KEY_DIM = 128
DN_VAL_DIM = 128
DN_QK_WIDTH = DN_HEADS * DN_KEY_DIM
DN_V_WIDTH = DN_HEADS * DN_VAL_DIM
DN_CONV_WIDTH = 4
DN_CHUNK = 64

MOE_GROUPS = 4
MOE_EXPERTS_PER_GROUP = 8
MOE_EXPERTS = MOE_GROUPS * MOE_EXPERTS_PER_GROUP
MOE_TOPK = 2
MOE_FF = 512
MOE_ROWS = 128

NORM_EPS = 1e-6
N_MOD = 6

IN_SPLITS = (MOBA_WIDTH, MOBA_WIDTH, MOBA_WIDTH, 2 * DN_QK_WIDTH + DN_V_WIDTH, DN_V_WIDTH,
             DN_HEADS, DN_HEADS, D_MODEL, D_MODEL)
IN_WIDTH = 3 * MOBA_WIDTH + 2 * DN_QK_WIDTH + 2 * DN_V_WIDTH + 2 * DN_HEADS + 2 * D_MODEL

kernel_name = "hybrid_moba_gdn_hmoe_block"


def _rms_norm(x, gain):
    x32 = x.astype(jnp.float32)
    y = x32 * lax.rsqrt(jnp.mean(x32 * x32, axis=-1, keepdims=True) + NORM_EPS)
    return y.astype(x.dtype) * gain


def _l2_normalize(x):
    return x * lax.rsqrt(jnp.sum(x * x, axis=-1, keepdims=True) + NORM_EPS)


def _partial_rotary(x, positions):
    half = ROPE_DIM // 2
    inv_freq = jnp.power(ROPE_THETA, -jnp.arange(half, dtype=jnp.float32) * (2.0 / ROPE_DIM))
    ang = positions.astype(jnp.float32)[..., None] * inv_freq
    cos = jnp.cos(ang)[:, :, None, :]
    sin = jnp.sin(ang)[:, :, None, :]
    x1 = x[..., :half]
    x2 = x[..., half:ROPE_DIM]
    rot = jnp.concatenate([x1 * cos - x2 * sin, x2 * cos + x1 * sin], axis=-1).astype(x.dtype)
    return jnp.concatenate([rot, x[..., ROPE_DIM:]], axis=-1)


def _padded_rows(n_valid, n_groups, rows):
    total = n_valid + n_groups * (rows - 1)
    return -(-total // rows) * rows


def _group_rows(group_id, valid, n_groups, n_rows, rows):
    key = jnp.where(valid, group_id, n_groups)
    order = jnp.argsort(key).astype(jnp.int32)
    skey = key[order]
    counts = jnp.bincount(key, length=n_groups + 1)[:n_groups]
    padded = (counts + rows - 1) // rows * rows
    start = jnp.cumsum(counts) - counts
    pad_end = jnp.cumsum(padded)
    pad_start = pad_end - padded
    gc = jnp.minimum(skey, n_groups - 1)
    rank = jnp.arange(key.shape[0], dtype=jnp.int32) - start[gc]
    dest = jnp.where(skey < n_groups, pad_start[gc] + rank, n_rows)
    row_src = jnp.zeros((n_rows,), jnp.int32).at[dest].set(order, mode="drop")
    row_valid = jnp.zeros((n_rows,), bool).at[dest].set(True, mode="drop")
    block_start = jnp.arange(n_rows // rows, dtype=pad_end.dtype) * rows
    block_group = jnp.minimum(jnp.searchsorted(pad_end, block_start, side="right"), n_groups - 1)
    return row_src, row_valid, block_group


def _moba_single(q, k, v):
    s_pad, dh = q.shape
    nblk = s_pad // MOBA_BLOCK
    scale = dh ** -0.5
    qb = q.reshape(nblk, MOBA_BLOCK, dh)
    kb = k.reshape(nblk, MOBA_BLOCK, dh)
    vb = v.reshape(nblk, MOBA_BLOCK, dh)
    causal = jnp.tril(jnp.ones((MOBA_BLOCK, MOBA_BLOCK), dtype=bool))
    s_own = jnp.where(causal, jnp.einsum("nqd,nkd->nqk", qb, kb) * scale, -jnp.inf)
    lse_own = jax.nn.logsumexp(s_own, axis=-1)
    o_own = jnp.einsum("nqk,nkd->nqd", jnp.exp(s_own - lse_own[..., None]), vb).reshape(s_pad, dh)
    lse_own = lse_own.reshape(s_pad)
    k_sel = min(MOBA_TOPK, nblk - 1)
    if k_sel == 0:
        return o_own
    q_blk = jnp.arange(s_pad) // MOBA_BLOCK
    k_mean = kb.mean(axis=1)
    gate = jnp.where(jnp.arange(nblk)[None, :] < q_blk[:, None], q @ k_mean.T, -jnp.inf)
    _, sel_blk = lax.top_k(gate, k_sel)
    sel_valid = jnp.arange(k_sel)[None, :] < q_blk[:, None]
    n_valid = sum(MOBA_BLOCK * min(k_sel, j) for j in range(nblk))
    n_rows = _padded_rows(n_valid, nblk, MOBA_QROWS)
    row_src, row_valid, block_of = _group_rows(sel_blk.reshape(-1), sel_valid.reshape(-1),
                                               nblk, n_rows, MOBA_QROWS)
    row_q = row_src // k_sel

    def attend_block(args):
        rq, blk = args
        s = (q[rq] @ kb[blk].T) * scale
        lse = jax.nn.logsumexp(s, axis=-1)
        return jnp.exp(s - lse[:, None]) @ vb[blk], lse

    o_sel, lse_sel = lax.map(attend_block, (row_q.reshape(-1, MOBA_QROWS), block_of))
    o_sel = o_sel.reshape(n_rows, dh)
    lse_sel = jnp.where(row_valid, lse_sel.reshape(n_rows), -jnp.inf)
    m = jnp.maximum(lse_own, jax.ops.segment_max(lse_sel, row_q, num_segments=s_pad))
    w_sel = jnp.where(row_valid, jnp.exp(lse_sel - m[row_q]), 0.0)
    w_own = jnp.exp(lse_own - m)
    num = w_own[:, None] * o_own + jax.ops.segment_sum(w_sel[:, None] * o_sel, row_q, num_segments=s_pad)
    den = w_own + jax.ops.segment_sum(w_sel, row_q, num_segments=s_pad)
    return num / den[:, None]


def _moba_branch(q, k, v, positions):
    b, s, _ = q.shape
    q = _partial_rotary(q.reshape(b, s, MOBA_HEADS, HEAD_DIM), positions)
    k = _partial_rotary(k.reshape(b, s, MOBA_HEADS, HEAD_DIM), positions)
    v = v.reshape(b, s, MOBA_HEADS, HEAD_DIM)
    pad = (-s) % MOBA_BLOCK

    def prep(t):
        t = t.astype(jnp.float32).transpose(0, 2, 1, 3)
        return jnp.pad(t, ((0, 0), (0, 0), (0, pad), (0, 0)))

    o = jax.vmap(jax.vmap(_moba_single))(prep(q), prep(k), prep(v))
    return o[:, :, :s].transpose(0, 2, 1, 3).reshape(b, s, MOBA_WIDTH)


def _causal_short_conv(x, w):
    return lax.conv_general_dilated(
        x, w[:, None, :].astype(x.dtype), window_strides=(1,),
        padding=[(DN_CONV_WIDTH - 1, 0)], dimension_numbers=("NWC", "WIO", "NWC"),
        feature_group_count=x.shape[-1])


def _chunk_gated_delta(q, k, v, g, beta):
    b, h, s, dk = q.shape
    dv = v.shape[-1]
    c = DN_CHUNK
    n = s // c
    q = (q * dk ** -0.5).reshape(b, h, n, c, dk)
    k = k.reshape(b, h, n, c, dk)
    v = v.reshape(b, h, n, c, dv)
    beta = beta.reshape(b, h, n, c)
    big_g = jnp.cumsum(g.reshape(b, h, n, c), axis=-1)
    lower = jnp.tril(jnp.ones((c, c), dtype=bool))
    strict = jnp.tril(jnp.ones((c, c), dtype=bool), -1)
    diff = big_g[..., :, None] - big_g[..., None, :]
    decay = jnp.where(lower, jnp.exp(jnp.where(lower, diff, 0.0)), 0.0)
    k_beta = k * beta[..., None]
    v_beta = v * beta[..., None]
    kk = jnp.where(strict, jnp.einsum("bhnid,bhnjd->bhnij", k_beta, k) * decay, 0.0)
    a_mat = kk + jnp.eye(c, dtype=kk.dtype)
    u = lax.linalg.triangular_solve(a_mat, v_beta, left_side=True, lower=True, unit_diagonal=True)
    w = lax.linalg.triangular_solve(a_mat, k_beta * jnp.exp(big_g)[..., None],
                                    left_side=True, lower=True, unit_diagonal=True)
    qk = jnp.einsum("bhnid,bhnjd->bhnij", q, k) * decay
    q_dec = q * jnp.exp(big_g)[..., None]
    g_last = big_g[..., -1]
    k_tail = k * jnp.exp(g_last[..., None] - big_g)[..., None]
    xs = tuple(jnp.moveaxis(t, 2, 0) for t in (q_dec, qk, u, w, k_tail, g_last))

    def step(state, chunk):
        q_c, qk_c, u_c, w_c, kt_c, gl_c = chunk
        v_new = u_c - jnp.einsum("bhcd,bhdv->bhcv", w_c, state)
        o_c = jnp.einsum("bhcd,bhdv->bhcv", q_c, state) + jnp.einsum("bhij,bhjv->bhiv", qk_c, v_new)
        state = state * jnp.exp(gl_c)[..., None, None] + jnp.einsum("bhcd,bhcv->bhdv", kt_c, v_new)
        return state, o_c

    state0 = jnp.zeros((b, h, dk, dv), jnp.float32)
    _, o = lax.scan(step, state0, xs)
    return jnp.moveaxis(o, 0, 2).reshape(b, h, s, dv)


def _deltanet_branch(qkv, z, beta_raw, a_raw, conv_w, a_log, dt_bias, out_gain):
    b, s, _ = qkv.shape
    qkv = jax.nn.silu(_causal_short_conv(qkv, conv_w)).astype(jnp.float32)
    q, k, v = jnp.split(qkv, [DN_QK_WIDTH, 2 * DN_QK_WIDTH], axis=-1)
    q = _l2_normalize(q.reshape(b, s, DN_HEADS, DN_KEY_DIM))
    k = _l2_normalize(k.reshape(b, s, DN_HEADS, DN_KEY_DIM))
    v = v.reshape(b, s, DN_HEADS, DN_VAL_DIM)
    beta = jax.nn.sigmoid(beta_raw.astype(jnp.float32))
    g = -jnp.exp(a_log.astype(jnp.float32)) * jax.nn.softplus(
        a_raw.astype(jnp.float32) + dt_bias.astype(jnp.float32))
    o = _chunk_gated_delta(q.transpose(0, 2, 1, 3), k.transpose(0, 2, 1, 3), v.transpose(0, 2, 1, 3),
                           g.transpose(0, 2, 1), beta.transpose(0, 2, 1))
    o = o.transpose(0, 2, 1, 3)
    o = _rms_norm(o, out_gain.astype(jnp.float32)) * jax.nn.silu(
        z.reshape(b, s, DN_HEADS, DN_VAL_DIM).astype(jnp.float32))
    return o.reshape(b, s, DN_V_WIDTH)


def _mixer(u, positions, w_in, conv_w, a_log, dt_bias, dn_out_norm, w_branch_moba, w_branch_delta, w_out):
    proj = u @ w_in
    split_at = np.cumsum(IN_SPLITS)[:-1].tolist()
    qa, ka, va, qkv_d, z_d, beta_raw, a_raw, gate_a, gate_b = jnp.split(proj, split_at, axis=-1)
    y_a = _moba_branch(qa, ka, va, positions).astype(u.dtype)
    y_b = _deltanet_branch(qkv_d, z_d, beta_raw, a_raw, conv_w, a_log, dt_bias, dn_out_norm).astype(u.dtype)
    merged = jax.nn.sigmoid(gate_a) * (y_a @ w_branch_moba) + jax.nn.sigmoid(gate_b) * (y_b @ w_branch_delta)
    return merged @ w_out


def _hier_moe(h, w_rg, b_rg, w_re, b_re, w_gate, w_up, w_down):
    b, s, d = h.shape
    t = b * s
    tok = h.reshape(t, d)
    g_logits = (tok @ w_rg + b_rg).astype(jnp.float32)
    p_group = jax.nn.softmax(g_logits, axis=-1)
    g_sel = jnp.argmax(g_logits, axis=-1)
    e_logits = (tok @ w_re + b_re).astype(jnp.float32).reshape(t, MOE_GROUPS, MOE_EXPERTS_PER_GROUP)
    e_logits = jnp.take_along_axis(e_logits, g_sel[:, None, None], axis=1)[:, 0]
    top_p, top_e = lax.top_k(jax.nn.softmax(e_logits, axis=-1), MOE_TOPK)
    weights = jnp.take_along_axis(p_group, g_sel[:, None], axis=1) * top_p / jnp.sum(top_p, axis=-1, keepdims=True)
    expert_id = g_sel[:, None] * MOE_EXPERTS_PER_GROUP + top_e
    n_rows = _padded_rows(t * MOE_TOPK, MOE_EXPERTS, MOE_ROWS)
    row_src, row_valid, block_expert = _group_rows(
        expert_id.reshape(-1), jnp.ones((t * MOE_TOPK,), bool), MOE_EXPERTS, n_rows, MOE_ROWS)
    row_tok = row_src // MOE_TOPK
    row_w = jnp.where(row_valid, weights.reshape(-1)[row_src], 0.0)

    def expert_block(args):
        rt, e = args
        xb = tok[rt]
        return (jax.nn.silu(xb @ w_gate[e]) * (xb @ w_up[e])) @ w_down[e]

    yb = lax.map(expert_block, (row_tok.reshape(-1, MOE_ROWS), block_expert)).reshape(n_rows, d)
    y = jax.ops.segment_sum(yb * row_w[:, None].astype(yb.dtype), row_tok, num_segments=t)
    return y.reshape(b, s, d)


def setup_inputs(seed: int = 0) -> dict:
    key = jax.random.key(seed)
    ks = jax.random.split(key, 24)
    nrm = jax.random.normal
    f32 = jnp.float32
    x = nrm(ks[0], (BATCH, SEQ, D_MODEL), f32)
    c = nrm(ks[1], (BATCH, D_MODEL), f32)
    offset = jax.random.randint(ks[2], (BATCH, 1), 0, 4096, dtype=jnp.int32)
    positions = (offset + jnp.arange(SEQ, dtype=jnp.int32)[None, :]).astype(jnp.int32)

    def gain(k):
        return 1.0 + 0.05 * nrm(k, (DEPTH, D_MODEL), f32)

    gk = jax.random.split(ks[3], 5)
    dt = jnp.exp(jax.random.uniform(ks[4], (DEPTH, DN_HEADS), f32, np.log(1e-3), np.log(1e-1)))
    return {
        "x": x,
        "c": c,
        "positions": positions,
        "w_ada": nrm(ks[5], (DEPTH, D_MODEL, N_MOD * D_MODEL), f32) * (0.5 * D_MODEL ** -0.5),
        "b_ada": 0.02 * nrm(ks[6], (DEPTH, N_MOD * D_MODEL), f32),
        "norm_mix_pre": gain(gk[0]),
        "norm_mix_post": gain(gk[1]),
        "norm_ffn_pre": gain(gk[2]),
        "norm_ffn_post": gain(gk[3]),
        "w_in": nrm(ks[7], (DEPTH, D_MODEL, IN_WIDTH), f32) * D_MODEL ** -0.5,
        "conv_w": nrm(ks[8], (DEPTH, DN_CONV_WIDTH, 2 * DN_QK_WIDTH + DN_V_WIDTH), f32) * DN_CONV_WIDTH ** -0.5,
        "dn_a_log": jnp.log(jax.random.uniform(ks[9], (DEPTH, DN_HEADS), f32, 1.0, 16.0)),
        "dn_dt_bias": dt + jnp.log(-jnp.expm1(-dt)),
        "dn_out_norm": 1.0 + 0.05 * nrm(gk[4], (DEPTH, DN_VAL_DIM), f32),
        "w_branch_moba": nrm(ks[10], (DEPTH, MOBA_WIDTH, D_MODEL), f32) * MOBA_WIDTH ** -0.5,
        "w_branch_delta": nrm(ks[11], (DEPTH, DN_V_WIDTH, D_MODEL), f32) * DN_V_WIDTH ** -0.5,
        "w_out": nrm(ks[12], (DEPTH, D_MODEL, D_MODEL), f32) * D_MODEL ** -0.5,
        "router_group_w": nrm(ks[13], (DEPTH, D_MODEL, MOE_GROUPS), f32) * D_MODEL ** -0.5,
        "router_group_b": 0.01 * nrm(ks[14], (DEPTH, MOE_GROUPS), f32),
        "router_expert_w": nrm(ks[15], (DEPTH, D_MODEL, MOE_EXPERTS), f32) * D_MODEL ** -0.5,
        "router_expert_b": 0.01 * nrm(ks[16], (DEPTH, MOE_EXPERTS), f32),
        "expert_w_gate": nrm(ks[17], (DEPTH, MOE_EXPERTS, D_MODEL, MOE_FF), f32) * D_MODEL ** -0.5,
        "expert_w_up": nrm(ks[18], (DEPTH, MOE_EXPERTS, D_MODEL, MOE_FF), f32) * D_MODEL ** -0.5,
        "expert_w_down": nrm(ks[19], (DEPTH, MOE_EXPERTS, MOE_FF, D_MODEL), f32) * MOE_FF ** -0.5,
    }


def reference(x, c, positions, w_ada, b_ada, norm_mix_pre, norm_mix_post, norm_ffn_pre, norm_ffn_post,
              w_in, conv_w, dn_a_log, dn_dt_bias, dn_out_norm, w_branch_moba, w_branch_delta, w_out,
              router_group_w, router_group_b, router_expert_w, router_expert_b,
              expert_w_gate, expert_w_up, expert_w_down):
    cond = jax.nn.silu(c)
    for layer in range(DEPTH):
        mod = cond @ w_ada[layer] + b_ada[layer]
        sh1, sc1, gt1, sh2, sc2, gt2 = [m[:, None, :] for m in jnp.split(mod, N_MOD, axis=-1)]
        u = _rms_norm(x, norm_mix_pre[layer]) * (1.0 + sc1) + sh1
        y = _mixer(u, positions, w_in[layer], conv_w[layer], dn_a_log[layer], dn_dt_bias[layer],
                   dn_out_norm[layer], w_branch_moba[layer], w_branch_delta[layer], w_out[layer])
        x = x + gt1 * _rms_norm(y, norm_mix_post[layer])
        u = _rms_norm(x, norm_ffn_pre[layer]) * (1.0 + sc2) + sh2
        y = _hier_moe(u, router_group_w[layer], router_group_b[layer], router_expert_w[layer],
                      router_expert_b[layer], expert_w_gate[layer], expert_w_up[layer], expert_w_down[layer])
        x = x + gt2 * _rms_norm(y, norm_ffn_post[layer])
    return x
```

```python
import functools
import math

import jax
import jax.numpy as jnp
from jax import lax
from jax.experimental import pallas as pl
from jax.experimental.pallas import tpu as pltpu

F32 = jnp.float32
BF16 = jnp.bfloat16
HI = lax.Precision.HIGHEST

NORM_EPS = 1e-6
HEAD_DIM = 128
MOBA_HEADS = 8
MOBA_BLOCK = 256
MOBA_TOPK = 3
ROPE_THETA = 500000.0
ROPE_DIM = HEAD_DIM // 4
DN_HEADS = 8
DN_CONV_WIDTH = 4
DN_CHUNK = 64
DN_GROUP = 256
MOE_GROUPS = 4
MOE_EXPERTS_PER_GROUP = 8
MOE_EXPERTS = MOE_GROUPS * MOE_EXPERTS_PER_GROUP
MOE_TOPK = 2
MOE_ROWS = 256
LANES = 128
NEG = -1e30

VMEM_LIMIT = 56 * 1024 * 1024


def _cparams(*sem):
    return pltpu.CompilerParams(dimension_semantics=sem, vmem_limit_bytes=VMEM_LIMIT)


def _mm(a, b, precision=None):
    return jnp.dot(a, b, precision=precision, preferred_element_type=F32)


def _nt(a, b, precision=None):
    return lax.dot_general(a, b, (((1,), (1,)), ((), ())), precision=precision,
                           preferred_element_type=F32)


def _eye(n, dtype):
    r = lax.broadcasted_iota(jnp.int32, (n, n), 0)
    c = lax.broadcasted_iota(jnp.int32, (n, n), 1)
    return jnp.where(r == c, 1.0, 0.0).astype(dtype)


def _silu(x):
    return x * jax.nn.sigmoid(x)


def _softplus(x):
    return jnp.maximum(x, 0.0) + jnp.log1p(jnp.exp(-jnp.abs(x)))


def _ada_body(c_ref, w_ref, b_ref, o_ref):
    o_ref[...] = _mm(_silu(c_ref[...]), w_ref[...], HI) + b_ref[...]


def _ada(c, w, b):
    bsz, d = c.shape
    n = w.shape[1]
    tn = 1024
    return pl.pallas_call(
        _ada_body,
        grid=(n // tn,),
        in_specs=[pl.BlockSpec((bsz, d), lambda j: (0, 0)),
                  pl.BlockSpec((d, tn), lambda j: (0, j)),
                  pl.BlockSpec((1, tn), lambda j: (0, j))],
        out_specs=pl.BlockSpec((bsz, tn), lambda j: (0, j)),
        out_shape=jax.ShapeDtypeStruct((bsz, n), F32),
        compiler_params=_cparams("parallel"),
        name="ada",
    )(c, w, b)


def _inproj_body(x_ref, g_ref, sc_ref, sh_ref, w_ref, o_ref, u_ref, *, tm, rc):
    @pl.when(pl.program_id(1) == 0)
    def _():
        gain = g_ref[...]
        scale = 1.0 + sc_ref[...]
        shift = sh_ref[...]

        def chunk(i, carry):
            r0 = pl.multiple_of(i * rc, rc)
            x = x_ref[pl.ds(r0, rc), :]
            y = x * lax.rsqrt(jnp.mean(x * x, axis=-1, keepdims=True) + NORM_EPS) * gain
            u_ref[pl.ds(r0, rc), :] = (y * scale + shift).astype(BF16)
            return carry

        lax.fori_loop(0, tm // rc, chunk, 0)

    o_ref[...] = _mm(u_ref[...], w_ref[...]).astype(o_ref.dtype)


def _inproj(x2, gain, mod3, w, seq, *, sc_chunk, sh_chunk):
    t, d = x2.shape
    n = w.shape[1]
    tm = min(1024, seq)
    tn = 512
    per_b = seq // tm
    rc = min(256, tm)
    return pl.pallas_call(
        functools.partial(_inproj_body, tm=tm, rc=rc),
        grid=(t // tm, n // tn),
        in_specs=[pl.BlockSpec((tm, d), lambda i, j: (i, 0)),
                  pl.BlockSpec((1, d), lambda i, j: (0, 0)),
                  pl.BlockSpec((None, 1, d), lambda i, j: (i // per_b, 0, sc_chunk)),
                  pl.BlockSpec((None, 1, d), lambda i, j: (i // per_b, 0, sh_chunk)),
                  pl.BlockSpec((d, tn), lambda i, j: (0, j))],
        out_specs=pl.BlockSpec((tm, tn), lambda i, j: (i, j)),
        out_shape=jax.ShapeDtypeStruct((t, n), BF16),
        scratch_shapes=[pltpu.VMEM((tm, d), BF16)],
        compiler_params=_cparams("parallel", "arbitrary"),
        name="inproj",
    )(x2, gain, mod3, mod3, w)


def _moba_body(q_ref, k_ref, v_ref, cos_ref, sin_ref, o_ref, kr_s, vt_s, km_s, s_s, *, nblk):
    blk = MOBA_BLOCK
    half = ROPE_DIM // 2
    lane = lax.broadcasted_iota(jnp.int32, (blk, HEAD_DIM), 1)
    eye_d = _eye(HEAD_DIM, BF16)
    eye_b = _eye(blk, BF16)
    scale = HEAD_DIM ** -0.5

    def rope(xf, rows):
        partner = jnp.where(lane < half, pltpu.roll(xf, HEAD_DIM - half, 1), pltpu.roll(xf, half, 1))
        return xf * cos_ref[rows, :] + partner * sin_ref[rows, :]

    km_s[...] = jnp.zeros_like(km_s)
    for j in range(nblk):
        rows = slice(j * blk, (j + 1) * blk)
        kc = rope(k_ref[rows, :].astype(F32), rows)
        kr_s[rows, :] = kc.astype(BF16)
        km_s[j:j + 1, :] = jnp.mean(kc, axis=0, keepdims=True)
        vt_s[:, rows] = _nt(eye_d, v_ref[rows, :]).astype(BF16)

    key_i = lax.broadcasted_iota(jnp.int32, (blk, blk), 0)
    qry_i = lax.broadcasted_iota(jnp.int32, (blk, blk), 1)
    blk_i = lax.broadcasted_iota(jnp.int32, (8, blk), 0)

    for i in range(nblk):
        rows_i = slice(i * blk, (i + 1) * blk)
        qc = rope(q_ref[rows_i, :].astype(F32), rows_i)
        qs = (qc * scale).astype(BF16)
        if i > 0:
            g_t = _nt(km_s[...], qc, HI)
            rank = jnp.zeros((8, blk), F32)
            for jp in range(i):
                row = g_t[jp:jp + 1, :]
                beats = (row > g_t) | ((row == g_t) & (jp < blk_i))
                rank = rank + jnp.where(beats, 1.0, 0.0)
            sel = (rank < float(MOBA_TOPK)) & (blk_i < i)
            bias_t = jnp.where(sel, 0.0, NEG)
        m = None
        for j in range(i + 1):
            rows_j = slice(j * blk, (j + 1) * blk)
            st = _nt(kr_s[rows_j, :], qs)
            if j < i:
                st = st + bias_t[j:j + 1, :]
            else:
                st = jnp.where(key_i <= qry_i, st, NEG)
            s_s[rows_j, :] = st
            mj = jnp.max(st, axis=0, keepdims=True)
            m = mj if m is None else jnp.maximum(m, mj)
        den = jnp.zeros((1, blk), F32)
        acc = jnp.zeros((HEAD_DIM, blk), F32)
        for j in range(i + 1):
            rows_j = slice(j * blk, (j + 1) * blk)
            p = jnp.exp(s_s[rows_j, :] - m)
            den = den + jnp.sum(p, axis=0, keepdims=True)
            acc = acc + _mm(vt_s[:, rows_j], p.astype(BF16))
        o_t = (acc / den).astype(BF16)
        o_ref[rows_i, :] = _nt(eye_b, o_t).astype(o_ref.dtype)


def _moba(proj, cos_t, sin_t, *, q_blk0, k_blk0, v_blk0):
    bsz, seq, _ = proj.shape
    nblk = seq // MOBA_BLOCK
    assert seq % MOBA_BLOCK == 0 and 1 <= nblk <= 8
    hd = HEAD_DIM
    return pl.pallas_call(
        functools.partial(_moba_body, nblk=nblk),
        grid=(bsz, MOBA_HEADS),
        in_specs=[pl.BlockSpec((None, seq, hd), lambda b, h: (b, 0, q_blk0 + h)),
                  pl.BlockSpec((None, seq, hd), lambda b, h: (b, 0, k_blk0 + h)),
                  pl.BlockSpec((None, seq, hd), lambda b, h: (b, 0, v_blk0 + h)),
                  pl.BlockSpec((None, seq, hd), lambda b, h: (b, 0, 0)),
                  pl.BlockSpec((None, seq, hd), lambda b, h: (b, 0, 0))],
        out_specs=pl.BlockSpec((None, seq, hd), lambda b, h: (b, 0, h)),
        out_shape=jax.ShapeDtypeStruct((bsz, seq, MOBA_HEADS * hd), BF16),
        scratch_shapes=[pltpu.VMEM((seq, hd), BF16),
                        pltpu.VMEM((hd, seq), BF16),
                        pltpu.VMEM((8, hd), F32),
                        pltpu.VMEM((seq, MOBA_BLOCK), F32)],
        compiler_params=_cparams("parallel", "parallel"),
        name="moba",
    )(proj, proj, proj, cos_t, sin_t)


def _chunk_masks(n, chunk):
    r = lax.broadcasted_iota(jnp.int32, (n, n), 0)
    c = lax.broadcasted_iota(jnp.int32, (n, n), 1)
    shift = int(math.log2(chunk))
    same = jnp.right_shift(r, shift) == jnp.right_shift(c, shift)
    return r, c, same


def _gates_body(ba_ref, par_ref, beta_ref, g_ref, gl_ref, *, seq):
    grp = DN_GROUP
    r, c, same = _chunk_masks(grp, DN_CHUNK)
    low = jnp.where(same & (c <= r), 1.0, 0.0)
    ones = jnp.where(same, 1.0, 0.0)
    neg_a = -jnp.exp(par_ref[0:1, :])
    dt_b = par_ref[1:2, :]
    for i in range(seq // grp):
        rows = slice(i * grp, (i + 1) * grp)
        x = ba_ref[rows, :].astype(F32)
        beta_ref[rows, :] = jax.nn.sigmoid(x)
        g = neg_a * _softplus(x + dt_b)
        g_ref[rows, :] = _mm(low, g, HI)
        gl_ref[rows, :] = _mm(ones, g, HI)


def _gates(ba, par):
    bsz, seq, _ = ba.shape
    spec = pl.BlockSpec((None, seq, LANES), lambda b: (b, 0, 0))
    return pl.pallas_call(
        functools.partial(_gates_body, seq=seq),
        grid=(bsz,),
        in_specs=[spec, pl.BlockSpec((8, LANES), lambda b: (0, 0))],
        out_specs=[spec, spec, spec],
        out_shape=[jax.ShapeDtypeStruct((bsz, seq, LANES), F32)] * 3,
        compiler_params=_cparams("parallel"),
        name="dn_gates",
    )(ba, par)


def _dn_body(q_ref, k_ref, v_ref, z_ref, cwq_ref, cwk_ref, cwv_ref, beta_ref, g_ref, gl_ref, grow_ref,
             gain_ref, o_ref, xq_s, xk_s, xv_s, qn_s, kn_s, vn_s, vnew_s, *, seq):
    h = pl.program_id(1)
    grp = DN_GROUP
    chunk = DN_CHUNK
    ngrp = seq // grp
    hd = HEAD_DIM
    pad = 8

    for src, dst in ((q_ref, xq_s), (k_ref, xk_s), (v_ref, xv_s)):
        dst[0:pad, :] = jnp.zeros((pad, hd), F32)
        for i in range(ngrp):
            dst[pad + i * grp: pad + (i + 1) * grp, :] = src[i * grp:(i + 1) * grp, :].astype(F32)

    for src, cw_ref, dst, l2, post in ((xq_s, cwq_ref, qn_s, True, hd ** -0.5),
                                       (xk_s, cwk_ref, kn_s, True, None),
                                       (xv_s, cwv_ref, vn_s, False, None)):
        cw = cw_ref[...]
        for i in range(ngrp):
            acc = None
            for j in range(DN_CONV_WIDTH):
                off = pad + i * grp - (DN_CONV_WIDTH - 1) + j
                term = src[off:off + grp, :] * cw[j:j + 1, :]
                acc = term if acc is None else acc + term
            y = _silu(acc)
            if l2:
                y = y * lax.rsqrt(jnp.sum(y * y, axis=-1, keepdims=True) + NORM_EPS)
            if post is not None:
                y = y * post
            dst[i * grp:(i + 1) * grp, :] = y

    r, c, same = _chunk_masks(grp, chunk)
    low_incl = same & (c <= r)
    low_strict = same & (c < r)
    eye_g = jnp.where(r == c, 1.0, 0.0)
    eye_d = _eye(hd, BF16)
    lane = lax.broadcasted_iota(jnp.int32, (grp, LANES), 1)
    col_chunk = jnp.right_shift(lax.broadcasted_iota(jnp.int32, (hd, grp), 1), int(math.log2(chunk)))
    gain = gain_ref[...]
    n_double = int(math.log2(chunk)) - 1

    def pick(ref, r0, lane_id):
        return jnp.sum(jnp.where(lane == lane_id, ref[pl.ds(r0, grp), :], 0.0), axis=-1, keepdims=True)

    def group(gi, state):
        r0 = pl.multiple_of(gi * grp, grp)
        q = qn_s[pl.ds(r0, grp), :]
        k = kn_s[pl.ds(r0, grp), :]
        v = vn_s[pl.ds(r0, grp), :]
        beta = pick(beta_ref, r0, h)
        g_col = pick(g_ref, r0, DN_HEADS + h)
        gl_col = pick(gl_ref, r0, DN_HEADS + h)
        g_row = grow_ref[gi]
        decay = jnp.exp(jnp.minimum(g_col - g_row, 0.0))
        e_g = jnp.exp(g_col)
        kb = k * beta
        vb = v * beta
        k16 = k.astype(BF16)
        n_mat = jnp.where(low_strict, _nt(kb.astype(BF16), k16) * decay, 0.0)
        p = (-n_mat).astype(BF16)
        x = eye_g - n_mat
        for _ in range(n_double):
            p2 = _mm(p, p)
            x = x + _mm(x.astype(BF16), p2.astype(BF16))
            p = p2.astype(BF16)
        x16 = x.astype(BF16)
        u = _mm(x16, vb.astype(BF16))
        w = _mm(x16, (kb * e_g).astype(BF16)).astype(BF16)
        qk = jnp.where(low_incl, _nt(q.astype(BF16), k16) * decay, 0.0).astype(BF16)
        q_dec = (q * e_g).astype(BF16)
        k_tail_t = _nt(eye_d, (k * jnp.exp(gl_col - g_col)).astype(BF16)).astype(BF16)
        vnew_s[...] = jnp.zeros_like(vnew_s)
        for ci in range(grp // chunk):
            rows = slice(ci * chunk, (ci + 1) * chunk)
            s16 = state.astype(BF16)
            v_new = u[rows, :] - _mm(w[rows, :], s16)
            vnew_s[rows, :] = v_new.astype(BF16)
            o = _mm(q_dec[rows, :], s16) + _mm(qk[rows, :], vnew_s[...])
            chunk_decay = jnp.exp(gl_ref[pl.ds(r0 + ci * chunk, 1), :])
            chunk_decay = jnp.sum(jnp.where(lane[0:1, :] == DN_HEADS + h, chunk_decay, 0.0),
                                  axis=-1, keepdims=True)
            kt = jnp.where(col_chunk == ci, k_tail_t, jnp.zeros_like(k_tail_t))
            state = state * chunk_decay + _mm(kt, vnew_s[...])
            on = o * lax.rsqrt(jnp.mean(o * o, axis=-1, keepdims=True) + NORM_EPS) * gain
            zz = z_ref[pl.ds(r0 + ci * chunk, chunk), :].astype(F32)
            o_ref[pl.ds(r0 + ci * chunk, chunk), :] = (on * _silu(zz)).astype(o_ref.dtype)
        return state

    lax.fori_loop(0, ngrp, group, jnp.zeros((hd, hd), F32))


def _deltanet(proj, conv_w, beta, gcum, glast, grow, gain, *, q_blk0, k_blk0, v_blk0, z_blk0):
    bsz, seq, _ = proj.shape
    hd = HEAD_DIM
    nh = DN_HEADS
    assert seq % DN_GROUP == 0
    ngrp = seq // DN_GROUP

    def col(blk0):
        return pl.BlockSpec((None, seq, hd), lambda b, h: (b, 0, blk0 + h))

    def cw(blk0):
        return pl.BlockSpec((DN_CONV_WIDTH, hd), lambda b, h: (0, blk0 + h))

    full = pl.BlockSpec((None, seq, LANES), lambda b, h: (b, 0, 0))
    return pl.pallas_call(
        functools.partial(_dn_body, seq=seq),
        grid=(bsz, nh),
        in_specs=[col(q_blk0), col(k_blk0), col(v_blk0), col(z_blk0),
                  cw(0), cw(nh), cw(2 * nh),
                  full, full, full,
                  pl.BlockSpec((None, ngrp, 1, DN_GROUP), lambda b, h: (b * nh + h, 0, 0, 0)),
                  pl.BlockSpec((1, hd), lambda b, h: (0, 0))],
        out_specs=pl.BlockSpec((None, seq, hd), lambda b, h: (b, 0, h)),
        out_shape=jax.ShapeDtypeStruct((bsz, seq, nh * hd), BF16),
        scratch_shapes=[pltpu.VMEM((seq + 8, hd), F32)] * 3 + [pltpu.VMEM((seq, hd), F32)] * 3
        + [pltpu.VMEM((DN_GROUP, hd), BF16)],
        compiler_params=_cparams("parallel", "parallel"),
        name="deltanet",
    )(proj, proj, proj, proj, conv_w, conv_w, conv_w, beta, gcum, glast, grow, gain)


def _merge_body(ya_ref, yb_ref, ga_ref, gb_ref, x_ref, wm_ref, wd_ref, wo_ref, wr_ref, br_ref,
                npost_ref, npre_ref, gt_ref, sc_ref, sh_ref, x1_ref, u2_ref, lg_ref):
    ma = _mm(ya_ref[...], wm_ref[...])
    mb = _mm(yb_ref[...], wd_ref[...])
    merged = (jax.nn.sigmoid(ga_ref[...].astype(F32)) * ma
              + jax.nn.sigmoid(gb_ref[...].astype(F32)) * mb)
    y = _mm(merged.astype(BF16), wo_ref[...])
    yn = y * lax.rsqrt(jnp.mean(y * y, axis=-1, keepdims=True) + NORM_EPS) * npost_ref[...]
    x1 = x_ref[...] + gt_ref[...] * yn
    x1_ref[...] = x1
    un = x1 * lax.rsqrt(jnp.mean(x1 * x1, axis=-1, keepdims=True) + NORM_EPS) * npre_ref[...]
    u2 = un * (1.0 + sc_ref[...]) + sh_ref[...]
    u2_ref[...] = u2
    lg_ref[...] = _mm(u2, wr_ref[...], HI) + br_ref[...]


def _merge(ya, yb, proj2, x2, wm, wd, wo, wr, br, npost, npre, mod3, seq, *, ga_blk, gb_blk):
    t, d = x2.shape
    wa = ya.shape[1]
    tm = min(256, seq)
    per_b = seq // tm
    const = lambda i: (0, 0)
    once = dict(pipeline_mode=pl.Buffered(1))

    def modspec(chunk):
        return pl.BlockSpec((None, 1, d), lambda i: (i // per_b, 0, chunk))

    return pl.pallas_call(
        _merge_body,
        grid=(t // tm,),
        in_specs=[pl.BlockSpec((tm, wa), lambda i: (i, 0)),
                  pl.BlockSpec((tm, wa), lambda i: (i, 0)),
                  pl.BlockSpec((tm, d), lambda i: (i, ga_blk)),
                  pl.BlockSpec((tm, d), lambda i: (i, gb_blk)),
                  pl.BlockSpec((tm, d), lambda i: (i, 0)),
                  pl.BlockSpec((wa, d), const, **once),
                  pl.BlockSpec((wa, d), const, **once),
                  pl.BlockSpec((d, d), const, **once),
                  pl.BlockSpec((d, LANES), const, **once),
                  pl.BlockSpec((1, LANES), const),
                  pl.BlockSpec((1, d), const),
                  pl.BlockSpec((1, d), const),
                  modspec(2), modspec(4), modspec(3)],
        out_specs=[pl.BlockSpec((tm, d), lambda i: (i, 0)),
                   pl.BlockSpec((tm, d), lambda i: (i, 0)),
                   pl.BlockSpec((tm, LANES), lambda i: (i, 0))],
        out_shape=[jax.ShapeDtypeStruct((t, d), F32),
                   jax.ShapeDtypeStruct((t, d), F32),
                   jax.ShapeDtypeStruct((t, LANES), F32)],
        compiler_params=_cparams("parallel"),
        name="merge",
    )(ya, yb, proj2, proj2, x2, wm, wd, wo, wr, br, npost, npre, mod3, mod3, mod3)


def _route_body(lg_ref, info_ref, cnt_ref, run_s, *, tr):
    @pl.when(pl.program_id(0) == 0)
    def _():
        run_s[...] = jnp.zeros_like(run_s)

    lg = lg_ref[...]
    lane = lax.broadcasted_iota(jnp.int32, (tr, LANES), 1)
    lane_f = lane.astype(F32)
    big = float(LANES)

    def first_max(vals, mask):
        mx = jnp.max(jnp.where(mask, vals, NEG), axis=-1, keepdims=True)
        idx = jnp.min(jnp.where(mask & (vals == mx), lane_f, big), axis=-1, keepdims=True)
        return mx, idx

    gmask = lane < MOE_GROUPS
    gmax, gidx = first_max(lg, gmask)
    p_group = 1.0 / jnp.sum(jnp.where(gmask, jnp.exp(lg - gmax), 0.0), axis=-1, keepdims=True)
    lo = float(MOE_GROUPS) + gidx * float(MOE_EXPERTS_PER_GROUP)
    emask = (lane_f >= lo) & (lane_f < lo + float(MOE_EXPERTS_PER_GROUP))
    m1, i1 = first_max(lg, emask)
    m2, i2 = first_max(lg, emask & (lane_f != i1))
    e2 = jnp.exp(m2 - m1)
    w1 = p_group / (1.0 + e2)
    w2 = p_group * e2 / (1.0 + e2)
    oh1 = lane_f == i1
    oh2 = lane_f == i2
    oh = jnp.where(oh1 | oh2, 1.0, 0.0).astype(BF16)
    r = lax.broadcasted_iota(jnp.int32, (tr, tr), 0)
    c = lax.broadcasted_iota(jnp.int32, (tr, tr), 1)
    before = jnp.where(c < r, 1.0, 0.0).astype(BF16)
    prefix = _mm(before, oh) + run_s[0:1, :]
    rank1 = jnp.sum(jnp.where(oh1, prefix, 0.0), axis=-1, keepdims=True)
    rank2 = jnp.sum(jnp.where(oh2, prefix, 0.0), axis=-1, keepdims=True)
    run_s[0:1, :] = run_s[0:1, :] + jnp.sum(oh.astype(F32), axis=0, keepdims=True)
    goff = float(MOE_GROUPS)
    info = jnp.where(lane == 0, i1 - goff, 0.0)
    info = jnp.where(lane == 1, i2 - goff, info)
    info = jnp.where(lane == 2, rank1, info)
    info = jnp.where(lane == 3, rank2, info)
    info = jnp.where(lane == 4, w1, info)
    info = jnp.where(lane == 5, w2, info)
    info_ref[...] = info
    cnt_ref[...] = jnp.broadcast_to(run_s[0:1, :], cnt_ref.shape)


def _route(logits):
    t = logits.shape[0]
    tr = min(256, t)
    return pl.pallas_call(
        functools.partial(_route_body, tr=tr),
        grid=(t // tr,),
        in_specs=[pl.BlockSpec((tr, LANES), lambda i: (i, 0))],
        out_specs=[pl.BlockSpec((tr, LANES), lambda i: (i, 0)),
                   pl.BlockSpec((8, LANES), lambda i: (0, 0))],
        out_shape=[jax.ShapeDtypeStruct((t, LANES), F32),
                   jax.ShapeDtypeStruct((8, LANES), F32)],
        scratch_shapes=[pltpu.VMEM((8, LANES), F32)],
        compiler_params=_cparams("arbitrary"),
        name="route",
    )(logits)


def _dispatch_body(dest_ref, u_ref, xs_in_ref, xs_ref, sem, *, td):
    del xs_in_ref

    def row_copy(r, k):
        return pltpu.make_async_copy(u_ref.at[pl.ds(r, 1), :],
                                     xs_ref.at[pl.ds(dest_ref[MOE_TOPK * r + k], 1), :], sem)

    def start(r, carry):
        for k in range(MOE_TOPK):
            row_copy(r, k).start()
        return carry

    def wait(r, carry):
        for k in range(MOE_TOPK):
            row_copy(r, k).wait()
        return carry

    lax.fori_loop(0, td, start, 0)
    lax.fori_loop(0, td, wait, 0)


def _dispatch(dest_flat, u2, xs_init):
    t, d = u2.shape
    td = min(512, t)
    return pl.pallas_call(
        functools.partial(_dispatch_body, td=td),
        grid=(t // td,),
        in_specs=[pl.BlockSpec((MOE_TOPK * td,), lambda i: (i,), memory_space=pltpu.SMEM),
                  pl.BlockSpec((td, d), lambda i: (i, 0)),
                  pl.BlockSpec(memory_space=pl.ANY)],
        out_specs=pl.BlockSpec(memory_space=pl.ANY),
        out_shape=jax.ShapeDtypeStruct(xs_init.shape, xs_init.dtype),
        scratch_shapes=[pltpu.SemaphoreType.DMA(())],
        input_output_aliases={2: 0},
        compiler_params=_cparams("arbitrary"),
        name="dispatch",
    )(dest_flat, u2, xs_init)


def _experts_body(be_ref, nu_ref, x_ref, wg_ref, wu_ref, wd_ref, o_ref):
    del be_ref

    @pl.when(pl.program_id(0) < nu_ref[0])
    def _():
        x = x_ref[...].astype(BF16)
        hidden = _silu(_mm(x, wg_ref[...])) * _mm(x, wu_ref[...])
        o_ref[...] = _mm(hidden.astype(BF16), wd_ref[...])

    @pl.when(pl.program_id(0) >= nu_ref[0])
    def _():
        o_ref[...] = jnp.zeros_like(o_ref)


def _experts(block_expert, n_used, xs, wg, wu, wd):
    nr, d = xs.shape
    ff = wg.shape[2]
    rb = MOE_ROWS
    row_map = lambda i, be, nu: (jnp.minimum(i, nu[0] - 1), 0)
    grid_spec = pltpu.PrefetchScalarGridSpec(
        num_scalar_prefetch=2,
        grid=(nr // rb,),
        in_specs=[pl.BlockSpec((rb, d), row_map),
                  pl.BlockSpec((None, d, ff), lambda i, be, nu: (be[i], 0, 0)),
                  pl.BlockSpec((None, d, ff), lambda i, be, nu: (be[i], 0, 0)),
                  pl.BlockSpec((None, ff, d), lambda i, be, nu: (be[i], 0, 0))],
        out_specs=pl.BlockSpec((rb, d), lambda i, be, nu: (i, 0)),
    )
    return pl.pallas_call(
        _experts_body,
        grid_spec=grid_spec,
        out_shape=jax.ShapeDtypeStruct((nr, d), F32),
        compiler_params=_cparams("arbitrary"),
        name="experts",
    )(block_expert, n_used, xs, wg, wu, wd)


def _combine_body(dest_ref, info_ref, x1_ref, npost_ref, gt_ref, yb_ref, o_ref, buf, sem, *, tc, rc):
    def row_copy(r, k):
        return pltpu.make_async_copy(yb_ref.at[pl.ds(dest_ref[MOE_TOPK * r + k], 1), :],
                                     buf.at[k, pl.ds(r, 1), :], sem)

    def start(r, carry):
        for k in range(MOE_TOPK):
            row_copy(r, k).start()
        return carry

    def wait(r, carry):
        for k in range(MOE_TOPK):
            row_copy(r, k).wait()
        return carry

    lax.fori_loop(0, tc, start, 0)
    lax.fori_loop(0, tc, wait, 0)

    gain = npost_ref[...]
    gate = gt_ref[...]

    def chunk(i, carry):
        r0 = pl.multiple_of(i * rc, rc)
        info = info_ref[pl.ds(r0, rc), :]
        y = info[:, 4:5] * buf[0, pl.ds(r0, rc), :] + info[:, 5:6] * buf[1, pl.ds(r0, rc), :]
        yn = y * lax.rsqrt(jnp.mean(y * y, axis=-1, keepdims=True) + NORM_EPS) * gain
        o_ref[pl.ds(r0, rc), :] = x1_ref[pl.ds(r0, rc), :] + gate * yn
        return carry

    lax.fori_loop(0, tc // rc, chunk, 0)


def _combine(dest_flat, info, x1, npost, mod3, yb, seq):
    t, d = x1.shape
    tc = min(512, seq)
    rc = min(128, tc)
    per_b = seq // tc
    return pl.pallas_call(
        functools.partial(_combine_body, tc=tc, rc=rc),
        grid=(t // tc,),
        in_specs=[pl.BlockSpec((MOE_TOPK * tc,), lambda i: (i,), memory_space=pltpu.SMEM),
                  pl.BlockSpec((tc, LANES), lambda i: (i, 0)),
                  pl.BlockSpec((tc, d), lambda i: (i, 0)),
                  pl.BlockSpec((1, d), lambda i: (0, 0)),
                  pl.BlockSpec((None, 1, d), lambda i: (i // per_b, 0, 5)),
                  pl.BlockSpec(memory_space=pl.ANY)],
        out_specs=pl.BlockSpec((tc, d), lambda i: (i, 0)),
        out_shape=jax.ShapeDtypeStruct((t, d), F32),
        scratch_shapes=[pltpu.VMEM((MOE_TOPK, tc, d), F32), pltpu.SemaphoreType.DMA(())],
        compiler_params=_cparams("arbitrary"),
        name="combine",
    )(dest_flat, info, x1, npost, mod3, yb)


def _rope_tables(positions):
    half = ROPE_DIM // 2
    inv_freq = jnp.power(ROPE_THETA, -jnp.arange(half, dtype=F32) * (2.0 / ROPE_DIM))
    ang = positions.astype(F32)[..., None] * inv_freq
    cos, sin = jnp.cos(ang), jnp.sin(ang)
    rest = HEAD_DIM - ROPE_DIM
    cos_t = jnp.concatenate([cos, cos, jnp.ones(cos.shape[:-1] + (rest,), F32)], axis=-1)
    sin_t = jnp.concatenate([-sin, sin, jnp.zeros(sin.shape[:-1] + (rest,), F32)], axis=-1)
    return cos_t, sin_t


def _pad_lanes(v, n=LANES):
    return jnp.pad(v, [(0, 0)] * (v.ndim - 1) + [(0, n - v.shape[-1])])


def _mixer_and_router(x, mod3, positions, norm_mix_pre, norm_mix_post, norm_ffn_pre, w_in, conv_w,
                      dn_a_log, dn_dt_bias, dn_out_norm, w_branch_moba, w_branch_delta, w_out,
                      router_group_w, router_group_b, router_expert_w, router_expert_b):
    bsz, seq, d = x.shape
    t = bsz * seq
    mw = MOBA_HEADS * HEAD_DIM
    dw = DN_HEADS * HEAD_DIM
    o_qa, o_ka, o_va = 0, mw, 2 * mw
    o_dn = 3 * mw
    o_z = o_dn + 3 * dw
    o_ba = o_z + dw
    o_ga = o_ba + 2 * DN_HEADS
    o_gb = o_ga + d
    w_perm = jnp.concatenate(
        [w_in[:, o_ga:o_ga + d], w_in[:, o_gb:o_gb + d], w_in[:, o_qa:o_ba],
         _pad_lanes(w_in[:, o_ba:o_ga], 4 * LANES)], axis=1).astype(BF16)
    x2 = x.reshape(t, d)
    proj2 = _inproj(x2, norm_mix_pre[None, :], mod3, w_perm, seq, sc_chunk=1, sh_chunk=0)
    proj = proj2.reshape(bsz, seq, -1)
    c0 = 2 * d // LANES
    nh = MOBA_HEADS

    cos_t, sin_t = _rope_tables(positions)
    ya = _moba(proj, cos_t, sin_t, q_blk0=c0, k_blk0=c0 + nh, v_blk0=c0 + 2 * nh)

    ba_col0 = 2 * d + 3 * mw + 3 * dw + dw
    ba = proj[:, :, ba_col0:ba_col0 + LANES]
    par = jnp.zeros((8, LANES), F32)
    par = par.at[0, DN_HEADS:2 * DN_HEADS].set(dn_a_log.astype(F32))
    par = par.at[1, DN_HEADS:2 * DN_HEADS].set(dn_dt_bias.astype(F32))
    beta, gcum, glast = _gates(ba, par)
    ngrp = seq // DN_GROUP
    grow = jnp.transpose(gcum[:, :, DN_HEADS:2 * DN_HEADS], (0, 2, 1)).reshape(bsz * DN_HEADS, ngrp, 1, DN_GROUP)
    d0 = c0 + 3 * nh
    yb = _deltanet(proj, conv_w, beta, gcum, glast, grow, dn_out_norm[None, :].astype(F32),
                   q_blk0=d0, k_blk0=d0 + DN_HEADS, v_blk0=d0 + 2 * DN_HEADS, z_blk0=d0 + 3 * DN_HEADS)

    wr = _pad_lanes(jnp.concatenate([router_group_w, router_expert_w], axis=1))
    br = _pad_lanes(jnp.concatenate([router_group_b, router_expert_b])[None, :])
    return _merge(ya.reshape(t, mw), yb.reshape(t, dw), proj2, x2,
                  w_branch_moba.astype(BF16), w_branch_delta.astype(BF16), w_out.astype(BF16), wr, br,
                  norm_mix_post[None, :], norm_ffn_pre[None, :], mod3, seq, ga_blk=0, gb_blk=1)


def _moe(x1, u2, logits, mod3, norm_ffn_post, w_gate, w_up, w_down, seq):
    t, d = x1.shape
    info, counts = _route(logits)
    counts = counts[0, MOE_GROUPS:MOE_GROUPS + MOE_EXPERTS].astype(jnp.int32)
    rb = MOE_ROWS
    padded = (counts + rb - 1) // rb * rb
    pad_end = jnp.cumsum(padded)
    pad_start = pad_end - padded
    eid = info[:, 0:MOE_TOPK].astype(jnp.int32)
    rank = info[:, MOE_TOPK:2 * MOE_TOPK].astype(jnp.int32)
    dest_flat = (pad_start[eid] + rank).reshape(-1)
    n_blocks = (t * MOE_TOPK + MOE_EXPERTS * (rb - 1)) // rb + 1
    n_used = (pad_end[-1] // rb).astype(jnp.int32)
    blk = jnp.minimum(jnp.arange(n_blocks, dtype=jnp.int32), n_used - 1)
    block_expert = jnp.minimum(jnp.searchsorted(pad_end, blk * rb, side="right"), MOE_EXPERTS - 1).astype(jnp.int32)
    xs = _dispatch(dest_flat, u2, jnp.zeros((n_blocks * rb, d), F32))
    ys = _experts(block_expert, n_used[None], xs, w_gate.astype(BF16), w_up.astype(BF16), w_down.astype(BF16))
    return _combine(dest_flat, info, x1, norm_ffn_post[None, :], mod3, ys, seq)


def kernel(x, c, positions, w_ada, b_ada, norm_mix_pre, norm_mix_post, norm_ffn_pre, norm_ffn_post, w_in, conv_w, dn_a_log, dn_dt_bias, dn_out_norm, w_branch_moba, w_branch_delta, w_out, router_group_w, router_group_b, router_expert_w, router_expert_b, expert_w_gate, expert_w_up, expert_w_down):
    bsz, seq, d = x.shape
    depth = w_ada.shape[0]
    for layer in range(depth):
        mod = _ada(c, w_ada[layer], b_ada[layer][None, :])
        mod3 = mod.reshape(bsz, 1, -1)
        x1, u2, logits = _mixer_and_router(
            x, mod3, positions, norm_mix_pre[layer], norm_mix_post[layer], norm_ffn_pre[layer], w_in[layer],
            conv_w[layer], dn_a_log[layer], dn_dt_bias[layer], dn_out_norm[layer], w_branch_moba[layer],
            w_branch_delta[layer], w_out[layer], router_group_w[layer], router_group_b[layer],
            router_expert_w[layer], router_expert_b[layer])
        out = _moe(x1, u2, logits, mod3, norm_ffn_post[layer], expert_w_gate[layer], expert_w_up[layer],
                   expert_w_down[layer], seq)
        x = out.reshape(bsz, seq, d)
    return x
```

```python
import functools
import math

import jax
import jax.numpy as jnp
from jax import lax
from jax.experimental import pallas as pl
from jax.experimental.pallas import tpu as pltpu

F32 = jnp.float32
BF16 = jnp.bfloat16
HI = lax.Precision.HIGHEST

NORM_EPS = 1e-6
HEAD_DIM = 128
MOBA_HEADS = 8
MOBA_BLOCK = 256
MOBA_TOPK = 3
ROPE_THETA = 500000.0
ROPE_DIM = HEAD_DIM // 4
DN_HEADS = 8
DN_CONV_WIDTH = 4
DN_CHUNK = 64
DN_GROUP = 256
DN_HEADS_PER_STEP = 4
DN_CONV_PAD = 8
MOE_GROUPS = 4
MOE_EXPERTS_PER_GROUP = 8
MOE_EXPERTS = MOE_GROUPS * MOE_EXPERTS_PER_GROUP
MOE_TOPK = 2
MOE_ROWS = 256
LANES = 128
NEG = -1e30

VMEM_LIMIT = 56 * 1024 * 1024


def _cparams(*sem):
    return pltpu.CompilerParams(dimension_semantics=sem, vmem_limit_bytes=VMEM_LIMIT)


def _mm(a, b, precision=None):
    return jnp.dot(a, b, precision=precision, preferred_element_type=F32)


def _nt(a, b, precision=None):
    return lax.dot_general(a, b, (((1,), (1,)), ((), ())), precision=precision,
                           preferred_element_type=F32)


def _eye(n, dtype):
    r = lax.broadcasted_iota(jnp.int32, (n, n), 0)
    c = lax.broadcasted_iota(jnp.int32, (n, n), 1)
    return jnp.where(r == c, 1.0, 0.0).astype(dtype)


def _silu(x):
    return x * jax.nn.sigmoid(x)


def _softplus(x):
    return jnp.maximum(x, 0.0) + jnp.log1p(jnp.exp(-jnp.abs(x)))


def _ada_body(c_ref, w_ref, b_ref, o_ref):
    o_ref[...] = _mm(_silu(c_ref[...]), w_ref[...], HI) + b_ref[...]


def _ada(c, w, b):
    bsz, d = c.shape
    n = w.shape[1]
    tn = 1024
    return pl.pallas_call(
        _ada_body,
        grid=(n // tn,),
        in_specs=[pl.BlockSpec((bsz, d), lambda j: (0, 0)),
                  pl.BlockSpec((d, tn), lambda j: (0, j)),
                  pl.BlockSpec((1, tn), lambda j: (0, j))],
        out_specs=pl.BlockSpec((bsz, tn), lambda j: (0, j)),
        out_shape=jax.ShapeDtypeStruct((bsz, n), F32),
        compiler_params=_cparams("parallel"),
        name="ada",
    )(c, w, b)


def _inproj_body(x_ref, g_ref, sc_ref, sh_ref, w_ref, o_ref, u_ref, *, tm, rc):
    @pl.when(pl.program_id(1) == 0)
    def _():
        gain = g_ref[...]
        scale = 1.0 + sc_ref[...]
        shift = sh_ref[...]

        def chunk(i, carry):
            r0 = pl.multiple_of(i * rc, rc)
            x = x_ref[pl.ds(r0, rc), :]
            y = x * lax.rsqrt(jnp.mean(x * x, axis=-1, keepdims=True) + NORM_EPS) * gain
            u_ref[pl.ds(r0, rc), :] = (y * scale + shift).astype(BF16)
            return carry

        lax.fori_loop(0, tm // rc, chunk, 0)

    o_ref[...] = _mm(u_ref[...], w_ref[...]).astype(o_ref.dtype)


def _inproj(x2, gain, mod3, w, seq, *, sc_chunk, sh_chunk):
    t, d = x2.shape
    n = w.shape[1]
    tm = min(1024, seq)
    tn = 512
    per_b = seq // tm
    rc = min(256, tm)
    return pl.pallas_call(
        functools.partial(_inproj_body, tm=tm, rc=rc),
        grid=(t // tm, n // tn),
        in_specs=[pl.BlockSpec((tm, d), lambda i, j: (i, 0)),
                  pl.BlockSpec((1, d), lambda i, j: (0, 0)),
                  pl.BlockSpec((None, 1, d), lambda i, j: (i // per_b, 0, sc_chunk)),
                  pl.BlockSpec((None, 1, d), lambda i, j: (i // per_b, 0, sh_chunk)),
                  pl.BlockSpec((d, tn), lambda i, j: (0, j))],
        out_specs=pl.BlockSpec((tm, tn), lambda i, j: (i, j)),
        out_shape=jax.ShapeDtypeStruct((t, n), BF16),
        scratch_shapes=[pltpu.VMEM((tm, d), BF16)],
        compiler_params=_cparams("parallel", "arbitrary"),
        name="inproj",
    )(x2, gain, mod3, mod3, w)


def _moba_body(q_ref, k_ref, v_ref, cos_ref, sin_ref, o_ref, kr_s, vt_s, km_s, s_s, *, nblk):
    blk = MOBA_BLOCK
    half = ROPE_DIM // 2
    lane = lax.broadcasted_iota(jnp.int32, (blk, HEAD_DIM), 1)
    eye_d = _eye(HEAD_DIM, BF16)
    eye_b = _eye(blk, BF16)
    scale = HEAD_DIM ** -0.5

    def rope(xf, rows):
        partner = jnp.where(lane < half, pltpu.roll(xf, HEAD_DIM - half, 1), pltpu.roll(xf, half, 1))
        return xf * cos_ref[rows, :] + partner * sin_ref[rows, :]

    km_s[...] = jnp.zeros_like(km_s)
    for j in range(nblk):
        rows = slice(j * blk, (j + 1) * blk)
        kc = rope(k_ref[rows, :].astype(F32), rows)
        kr_s[rows, :] = kc.astype(BF16)
        km_s[j:j + 1, :] = jnp.mean(kc, axis=0, keepdims=True)
        vt_s[:, rows] = _nt(eye_d, v_ref[rows, :]).astype(BF16)

    key_i = lax.broadcasted_iota(jnp.int32, (blk, blk), 0)
    qry_i = lax.broadcasted_iota(jnp.int32, (blk, blk), 1)
    blk_i = lax.broadcasted_iota(jnp.int32, (8, blk), 0)

    for i in range(nblk):
        rows_i = slice(i * blk, (i + 1) * blk)
        qc = rope(q_ref[rows_i, :].astype(F32), rows_i)
        qs = (qc * scale).astype(BF16)
        if i > 0:
            g_t = _nt(km_s[...], qc, HI)
            rank = jnp.zeros((8, blk), F32)
            for jp in range(i):
                row = g_t[jp:jp + 1, :]
                beats = (row > g_t) | ((row == g_t) & (jp < blk_i))
                rank = rank + jnp.where(beats, 1.0, 0.0)
            sel = (rank < float(MOBA_TOPK)) & (blk_i < i)
            bias_t = jnp.where(sel, 0.0, NEG)
        m = None
        for j in range(i + 1):
            rows_j = slice(j * blk, (j + 1) * blk)
            st = _nt(kr_s[rows_j, :], qs)
            if j < i:
                st = st + bias_t[j:j + 1, :]
            else:
                st = jnp.where(key_i <= qry_i, st, NEG)
            s_s[rows_j, :] = st
            mj = jnp.max(st, axis=0, keepdims=True)
            m = mj if m is None else jnp.maximum(m, mj)
        den = jnp.zeros((1, blk), F32)
        acc = jnp.zeros((HEAD_DIM, blk), F32)
        for j in range(i + 1):
            rows_j = slice(j * blk, (j + 1) * blk)
            p = jnp.exp(s_s[rows_j, :] - m)
            den = den + jnp.sum(p, axis=0, keepdims=True)
            acc = acc + _mm(vt_s[:, rows_j], p.astype(BF16))
        o_t = (acc / den).astype(BF16)
        o_ref[rows_i, :] = _nt(eye_b, o_t).astype(o_ref.dtype)


def _moba(proj, cos_t, sin_t, *, q_blk0, k_blk0, v_blk0):
    bsz, seq, _ = proj.shape
    nblk = seq // MOBA_BLOCK
    assert seq % MOBA_BLOCK == 0 and 1 <= nblk <= 8
    hd = HEAD_DIM
    return pl.pallas_call(
        functools.partial(_moba_body, nblk=nblk),
        grid=(bsz, MOBA_HEADS),
        in_specs=[pl.BlockSpec((None, seq, hd), lambda b, h: (b, 0, q_blk0 + h)),
                  pl.BlockSpec((None, seq, hd), lambda b, h: (b, 0, k_blk0 + h)),
                  pl.BlockSpec((None, seq, hd), lambda b, h: (b, 0, v_blk0 + h)),
                  pl.BlockSpec((None, seq, hd), lambda b, h: (b, 0, 0)),
                  pl.BlockSpec((None, seq, hd), lambda b, h: (b, 0, 0))],
        out_specs=pl.BlockSpec((None, seq, hd), lambda b, h: (b, 0, h)),
        out_shape=jax.ShapeDtypeStruct((bsz, seq, MOBA_HEADS * hd), BF16),
        scratch_shapes=[pltpu.VMEM((seq, hd), BF16),
                        pltpu.VMEM((hd, seq), BF16),
                        pltpu.VMEM((8, hd), F32),
                        pltpu.VMEM((seq, MOBA_BLOCK), F32)],
        compiler_params=_cparams("parallel", "parallel"),
        name="moba",
    )(proj, proj, proj, cos_t, sin_t)


def _chunk_masks(n, chunk):
    r = lax.broadcasted_iota(jnp.int32, (n, n), 0)
    c = lax.broadcasted_iota(jnp.int32, (n, n), 1)
    shift = int(math.log2(chunk))
    same = jnp.right_shift(r, shift) == jnp.right_shift(c, shift)
    return r, c, same


def _gates_body(ba_ref, par_ref, beta_ref, g_ref, gl_ref, *, seq):
    grp = DN_GROUP
    r, c, same = _chunk_masks(grp, DN_CHUNK)
    low = jnp.where(same & (c <= r), 1.0, 0.0)
    ones = jnp.where(same, 1.0, 0.0)
    neg_a = -jnp.exp(par_ref[0:1, :])
    dt_b = par_ref[1:2, :]
    for i in range(seq // grp):
        rows = slice(i * grp, (i + 1) * grp)
        x = ba_ref[rows, :].astype(F32)
        beta_ref[rows, :] = jax.nn.sigmoid(x)
        g = neg_a * _softplus(x + dt_b)
        g_ref[rows, :] = _mm(low, g, HI)
        gl_ref[rows, :] = _mm(ones, g, HI)


def _gates(proj, par, *, ba_blk):
    bsz, seq, _ = proj.shape
    spec = pl.BlockSpec((None, seq, LANES), lambda b: (b, 0, 0))
    return pl.pallas_call(
        functools.partial(_gates_body, seq=seq),
        grid=(bsz,),
        in_specs=[pl.BlockSpec((None, seq, LANES), lambda b: (b, 0, ba_blk)),
                  pl.BlockSpec((8, LANES), lambda b: (0, 0))],
        out_specs=[spec, spec, spec],
        out_shape=[jax.ShapeDtypeStruct((bsz, seq, LANES), F32)] * 3,
        compiler_params=_cparams("parallel"),
        name="dn_gates",
    )(proj, par)


def _dn_body(q_ref, k_ref, v_ref, z_ref, cwq_ref, cwk_ref, cwv_ref, beta_ref, g_ref, gl_ref, grow_ref,
             gain_ref, o_ref, xpad_s, qn_s, kn_s, vn_s, state_s, vnew_s, *, seq, hp):
    hg = pl.program_id(1)
    grp = DN_GROUP
    chunk = DN_CHUNK
    ngrp = seq // grp
    hd = HEAD_DIM
    pad = DN_CONV_PAD

    crows = 128
    xpad_s[0:pad, :] = jnp.zeros((pad, hp * hd), F32)
    for src, cw_ref, dst, l2, post in ((q_ref, cwq_ref, qn_s, True, hd ** -0.5),
                                       (k_ref, cwk_ref, kn_s, True, None),
                                       (v_ref, cwv_ref, vn_s, False, None)):
        def copy(i, carry, src=src):
            r0 = pl.multiple_of(i * crows, crows)
            xpad_s[pl.ds(pad + r0, crows), :] = src[pl.ds(r0, crows), :].astype(F32)
            return carry

        lax.fori_loop(0, seq // crows, copy, 0)
        cw = cw_ref[...]

        def conv(i, carry, cw=cw, dst=dst, l2=l2, post=post):
            r0 = pl.multiple_of(i * crows, crows)
            win = xpad_s.at[pl.ds(r0, crows + pad), :]
            acc = None
            for j in range(DN_CONV_WIDTH):
                off = pad - (DN_CONV_WIDTH - 1) + j
                term = win[off:off + crows, :] * cw[j:j + 1, :]
                acc = term if acc is None else acc + term
            y = _silu(acc)
            for hh in range(hp):
                cols = slice(hh * hd, (hh + 1) * hd)
                yh = y[:, cols]
                if l2:
                    yh = yh * lax.rsqrt(jnp.sum(yh * yh, axis=-1, keepdims=True) + NORM_EPS)
                if post is not None:
                    yh = yh * post
                dst[pl.ds(r0, crows), cols] = yh
            return carry

        lax.fori_loop(0, seq // crows, conv, 0)

    r, c, same = _chunk_masks(grp, chunk)
    low_incl = same & (c <= r)
    low_strict = same & (c < r)
    eye_g = jnp.where(r == c, 1.0, 0.0)
    eye_d = _eye(hd, BF16)
    lane = lax.broadcasted_iota(jnp.int32, (grp, LANES), 1)
    col_chunk = jnp.right_shift(lax.broadcasted_iota(jnp.int32, (hd, grp), 1), int(math.log2(chunk)))
    gain = gain_ref[...]
    n_double = int(math.log2(chunk)) - 1

    def pick(ref, r0, lane_id):
        return jnp.sum(jnp.where(lane == lane_id, ref[pl.ds(r0, grp), :], 0.0), axis=-1, keepdims=True)

    state_s[...] = jnp.zeros_like(state_s)
    heads = range(hp)

    def group(gi, carry):
        r0 = pl.multiple_of(gi * grp, grp)
        cols = [slice(hh * hd, (hh + 1) * hd) for hh in heads]
        head = [hg * hp + hh for hh in heads]
        q = [qn_s[pl.ds(r0, grp), cols[hh]] for hh in heads]
        k = [kn_s[pl.ds(r0, grp), cols[hh]] for hh in heads]
        v = [vn_s[pl.ds(r0, grp), cols[hh]] for hh in heads]
        beta = [pick(beta_ref, r0, head[hh]) for hh in heads]
        g_col = [pick(g_ref, r0, DN_HEADS + head[hh]) for hh in heads]
        gl_col = [pick(gl_ref, r0, DN_HEADS + head[hh]) for hh in heads]
        decay = [jnp.exp(jnp.minimum(g_col[hh] - grow_ref[hh, gi], 0.0)) for hh in heads]
        e_g = [jnp.exp(g_col[hh]) for hh in heads]
        kb = [k[hh] * beta[hh] for hh in heads]
        vb = [(v[hh] * beta[hh]).astype(BF16) for hh in heads]
        k16 = [k[hh].astype(BF16) for hh in heads]
        n_mat = [jnp.where(low_strict, _nt(kb[hh].astype(BF16), k16[hh]) * decay[hh], 0.0) for hh in heads]
        p = [(-n_mat[hh]).astype(BF16) for hh in heads]
        x = [eye_g - n_mat[hh] for hh in heads]
        for _ in range(n_double):
            p2 = [_mm(p[hh], p[hh]).astype(BF16) for hh in heads]
            x = [x[hh] + _mm(x[hh].astype(BF16), p2[hh]) for hh in heads]
            p = p2
        x16 = [x[hh].astype(BF16) for hh in heads]
        u = [_mm(x16[hh], vb[hh]) for hh in heads]
        w = [_mm(x16[hh], (kb[hh] * e_g[hh]).astype(BF16)).astype(BF16) for hh in heads]
        qk = [jnp.where(low_incl, _nt(q[hh].astype(BF16), k16[hh]) * decay[hh], 0.0).astype(BF16) for hh in heads]
        q_dec = [(q[hh] * e_g[hh]).astype(BF16) for hh in heads]
        k_tail_t = [_nt(eye_d, (k[hh] * jnp.exp(gl_col[hh] - g_col[hh])).astype(BF16)).astype(BF16)
                    for hh in heads]
        vnew_s[...] = jnp.zeros_like(vnew_s)
        for ci in range(grp // chunk):
            rows = slice(ci * chunk, (ci + 1) * chunk)
            gl_row = jnp.exp(gl_ref[pl.ds(r0 + ci * chunk, 1), :])
            for hh in heads:
                state = state_s[hh]
                s16 = state.astype(BF16)
                v_new = u[hh][rows, :] - _mm(w[hh][rows, :], s16)
                vnew_s[hh, rows, :] = v_new.astype(BF16)
                o = _mm(q_dec[hh][rows, :], s16) + _mm(qk[hh][rows, :], vnew_s[hh])
                chunk_decay = jnp.sum(jnp.where(lane[0:1, :] == DN_HEADS + head[hh], gl_row, 0.0),
                                      axis=-1, keepdims=True)
                kt = jnp.where(col_chunk == ci, k_tail_t[hh], jnp.zeros_like(k_tail_t[hh]))
                state_s[hh] = state * chunk_decay + _mm(kt, vnew_s[hh])
                on = o * lax.rsqrt(jnp.mean(o * o, axis=-1, keepdims=True) + NORM_EPS) * gain
                zz = z_ref[pl.ds(r0 + ci * chunk, chunk), cols[hh]].astype(F32)
                o_ref[pl.ds(r0 + ci * chunk, chunk), cols[hh]] = (on * _silu(zz)).astype(o_ref.dtype)
        return carry

    lax.fori_loop(0, ngrp, group, 0)


def _deltanet(proj, conv_w, beta, gcum, glast, grow, gain, *, q_blk0, k_blk0, v_blk0, z_blk0):
    bsz, seq, _ = proj.shape
    hd = HEAD_DIM
    nh = DN_HEADS
    assert seq % DN_GROUP == 0
    ngrp = seq // DN_GROUP

    hp = DN_HEADS_PER_STEP
    wide = hp * hd
    assert nh % hp == 0 and all(b0 % hp == 0 for b0 in (q_blk0, k_blk0, v_blk0, z_blk0))

    def col(blk0):
        return pl.BlockSpec((None, seq, wide), lambda b, h: (b, 0, blk0 // hp + h))

    def cw(blk0):
        return pl.BlockSpec((DN_CONV_WIDTH, wide), lambda b, h: (0, blk0 // hp + h))

    full = pl.BlockSpec((None, seq, LANES), lambda b, h: (b, 0, 0))
    return pl.pallas_call(
        functools.partial(_dn_body, seq=seq, hp=hp),
        grid=(bsz, nh // hp),
        in_specs=[col(q_blk0), col(k_blk0), col(v_blk0), col(z_blk0),
                  cw(0), cw(nh), cw(2 * nh),
                  full, full, full,
                  pl.BlockSpec((hp, ngrp, 1, DN_GROUP), lambda b, h: (b * (nh // hp) + h, 0, 0, 0)),
                  pl.BlockSpec((1, hd), lambda b, h: (0, 0))],
        out_specs=pl.BlockSpec((None, seq, wide), lambda b, h: (b, 0, h)),
        out_shape=jax.ShapeDtypeStruct((bsz, seq, nh * hd), BF16),
        scratch_shapes=[pltpu.VMEM((seq + DN_CONV_PAD, wide), F32)] + [pltpu.VMEM((seq, wide), F32)] * 3
        + [pltpu.VMEM((hp, hd, hd), F32), pltpu.VMEM((hp, DN_GROUP, hd), BF16)],
        compiler_params=_cparams("parallel", "parallel"),
        name="deltanet",
    )(proj, proj, proj, proj, conv_w, conv_w, conv_w, beta, gcum, glast, grow, gain)


def _merge_body(ya_ref, yb_ref, ga_ref, gb_ref, x_ref, wm_ref, wd_ref, wo_ref, wr_ref, br_ref,
                npost_ref, npre_ref, gt_ref, sc_ref, sh_ref, x1_ref, u2_ref, lg_ref):
    ma = _mm(ya_ref[...], wm_ref[...])
    mb = _mm(yb_ref[...], wd_ref[...])
    merged = (jax.nn.sigmoid(ga_ref[...].astype(F32)) * ma
              + jax.nn.sigmoid(gb_ref[...].astype(F32)) * mb)
    y = _mm(merged.astype(BF16), wo_ref[...])
    yn = y * lax.rsqrt(jnp.mean(y * y, axis=-1, keepdims=True) + NORM_EPS) * npost_ref[...]
    x1 = x_ref[...] + gt_ref[...] * yn
    x1_ref[...] = x1
    un = x1 * lax.rsqrt(jnp.mean(x1 * x1, axis=-1, keepdims=True) + NORM_EPS) * npre_ref[...]
    u2 = un * (1.0 + sc_ref[...]) + sh_ref[...]
    u2_ref[...] = u2
    lg_ref[...] = _mm(u2, wr_ref[...], HI) + br_ref[...]


def _merge(ya, yb, proj2, x2, wm, wd, wo, wr, br, npost, npre, mod3, seq, *, ga_blk, gb_blk):
    t, d = x2.shape
    wa = ya.shape[1]
    tm = min(256, seq)
    per_b = seq // tm
    const = lambda i: (0, 0)
    once = dict(pipeline_mode=pl.Buffered(1))

    def modspec(chunk):
        return pl.BlockSpec((None, 1, d), lambda i: (i // per_b, 0, chunk))

    return pl.pallas_call(
        _merge_body,
        grid=(t // tm,),
        in_specs=[pl.BlockSpec((tm, wa), lambda i: (i, 0)),
                  pl.BlockSpec((tm, wa), lambda i: (i, 0)),
                  pl.BlockSpec((tm, d), lambda i: (i, ga_blk)),
                  pl.BlockSpec((tm, d), lambda i: (i, gb_blk)),
                  pl.BlockSpec((tm, d), lambda i: (i, 0)),
                  pl.BlockSpec((wa, d), const, **once),
                  pl.BlockSpec((wa, d), const, **once),
                  pl.BlockSpec((d, d), const, **once),
                  pl.BlockSpec((d, LANES), const, **once),
                  pl.BlockSpec((1, LANES), const),
                  pl.BlockSpec((1, d), const),
                  pl.BlockSpec((1, d), const),
                  modspec(2), modspec(4), modspec(3)],
        out_specs=[pl.BlockSpec((tm, d), lambda i: (i, 0)),
                   pl.BlockSpec((tm, d), lambda i: (i, 0)),
                   pl.BlockSpec((tm, LANES), lambda i: (i, 0))],
        out_shape=[jax.ShapeDtypeStruct((t, d), F32),
                   jax.ShapeDtypeStruct((t, d), F32),
                   jax.ShapeDtypeStruct((t, LANES), F32)],
        compiler_params=_cparams("parallel"),
        name="merge",
    )(ya, yb, proj2, proj2, x2, wm, wd, wo, wr, br, npost, npre, mod3, mod3, mod3)


def _route_body(lg_ref, info_ref, cnt_ref, run_s, *, tr):
    @pl.when(pl.program_id(0) == 0)
    def _():
        run_s[...] = jnp.zeros_like(run_s)

    lg = lg_ref[...]
    lane = lax.broadcasted_iota(jnp.int32, (tr, LANES), 1)
    lane_f = lane.astype(F32)
    big = float(LANES)

    def first_max(vals, mask):
        mx = jnp.max(jnp.where(mask, vals, NEG), axis=-1, keepdims=True)
        idx = jnp.min(jnp.where(mask & (vals == mx), lane_f, big), axis=-1, keepdims=True)
        return mx, idx

    gmask = lane < MOE_GROUPS
    gmax, gidx = first_max(lg, gmask)
    p_group = 1.0 / jnp.sum(jnp.where(gmask, jnp.exp(lg - gmax), 0.0), axis=-1, keepdims=True)
    lo = float(MOE_GROUPS) + gidx * float(MOE_EXPERTS_PER_GROUP)
    emask = (lane_f >= lo) & (lane_f < lo + float(MOE_EXPERTS_PER_GROUP))
    m1, i1 = first_max(lg, emask)
    m2, i2 = first_max(lg, emask & (lane_f != i1))
    e2 = jnp.exp(m2 - m1)
    w1 = p_group / (1.0 + e2)
    w2 = p_group * e2 / (1.0 + e2)
    oh1 = lane_f == i1
    oh2 = lane_f == i2
    oh = jnp.where(oh1 | oh2, 1.0, 0.0).astype(BF16)
    r = lax.broadcasted_iota(jnp.int32, (tr, tr), 0)
    c = lax.broadcasted_iota(jnp.int32, (tr, tr), 1)
    before = jnp.where(c < r, 1.0, 0.0).astype(BF16)
    prefix = _mm(before, oh) + run_s[0:1, :]
    rank1 = jnp.sum(jnp.where(oh1, prefix, 0.0), axis=-1, keepdims=True)
    rank2 = jnp.sum(jnp.where(oh2, prefix, 0.0), axis=-1, keepdims=True)
    run_s[0:1, :] = run_s[0:1, :] + jnp.sum(oh.astype(F32), axis=0, keepdims=True)
    goff = float(MOE_GROUPS)
    info = jnp.where(lane == 0, i1 - goff, 0.0)
    info = jnp.where(lane == 1, i2 - goff, info)
    info = jnp.where(lane == 2, rank1, info)
    info = jnp.where(lane == 3, rank2, info)
    info = jnp.where(lane == 4, w1, info)
    info = jnp.where(lane == 5, w2, info)
    info_ref[...] = info
    cnt_ref[...] = jnp.broadcast_to(run_s[0:1, :], cnt_ref.shape)


def _route(logits):
    t = logits.shape[0]
    tr = min(256, t)
    return pl.pallas_call(
        functools.partial(_route_body, tr=tr),
        grid=(t // tr,),
        in_specs=[pl.BlockSpec((tr, LANES), lambda i: (i, 0))],
        out_specs=[pl.BlockSpec((tr, LANES), lambda i: (i, 0)),
                   pl.BlockSpec((8, LANES), lambda i: (0, 0))],
        out_shape=[jax.ShapeDtypeStruct((t, LANES), F32),
                   jax.ShapeDtypeStruct((8, LANES), F32)],
        scratch_shapes=[pltpu.VMEM((8, LANES), F32)],
        compiler_params=_cparams("arbitrary"),
        name="route",
    )(logits)


def _dispatch_body(dest_ref, u_ref, xs_ref, sem, *, td):
    def row_copy(r, k):
        return pltpu.make_async_copy(u_ref.at[pl.ds(r, 1), :],
                                     xs_ref.at[pl.ds(dest_ref[MOE_TOPK * r + k], 1), :], sem)

    def start(r, carry):
        for k in range(MOE_TOPK):
            row_copy(r, k).start()
        return carry

    def wait(r, carry):
        for k in range(MOE_TOPK):
            row_copy(r, k).wait()
        return carry

    lax.fori_loop(0, td, start, 0)
    lax.fori_loop(0, td, wait, 0)


def _dispatch(dest_flat, u2, n_rows):
    t, d = u2.shape
    td = min(512, t)
    return pl.pallas_call(
        functools.partial(_dispatch_body, td=td),
        grid=(t // td,),
        in_specs=[pl.BlockSpec((MOE_TOPK * td,), lambda i: (i,), memory_space=pltpu.SMEM),
                  pl.BlockSpec((td, d), lambda i: (i, 0))],
        out_specs=pl.BlockSpec(memory_space=pl.ANY),
        out_shape=jax.ShapeDtypeStruct((n_rows, d), u2.dtype),
        scratch_shapes=[pltpu.SemaphoreType.DMA(())],
        compiler_params=_cparams("arbitrary"),
        name="dispatch",
    )(dest_flat, u2)


def _experts_body(be_ref, nv_ref, nu_ref, x_ref, wg_ref, wu_ref, wd_ref, o_ref, wg_s, wu_s, wd_s):
    i = pl.program_id(0)

    @pl.when(jnp.logical_or(i == 0, be_ref[i] != be_ref[jnp.maximum(i - 1, 0)]))
    def _():
        wg_s[...] = wg_ref[...].astype(BF16)
        wu_s[...] = wu_ref[...].astype(BF16)
        wd_s[...] = wd_ref[...].astype(BF16)

    @pl.when(i < nu_ref[0])
    def _():
        row = lax.broadcasted_iota(jnp.int32, (x_ref.shape[0], 1), 0)
        x = jnp.where(row < nv_ref[i], x_ref[...], 0.0).astype(BF16)
        hidden = _silu(_mm(x, wg_s[...])) * _mm(x, wu_s[...])
        o_ref[...] = _mm(hidden.astype(BF16), wd_s[...])

    @pl.when(i >= nu_ref[0])
    def _():
        o_ref[...] = jnp.zeros_like(o_ref)


def _experts(block_expert, block_valid, n_used, xs, wg, wu, wd):
    nr, d = xs.shape
    ff = wg.shape[2]
    rb = MOE_ROWS
    row_map = lambda i, be, nv, nu: (jnp.minimum(i, nu[0] - 1), 0)
    w_map = lambda i, be, nv, nu: (be[i], 0, 0)
    grid_spec = pltpu.PrefetchScalarGridSpec(
        num_scalar_prefetch=3,
        grid=(nr // rb,),
        in_specs=[pl.BlockSpec((rb, d), row_map),
                  pl.BlockSpec((None, d, ff), w_map),
                  pl.BlockSpec((None, d, ff), w_map),
                  pl.BlockSpec((None, ff, d), w_map)],
        out_specs=pl.BlockSpec((rb, d), lambda i, be, nv, nu: (i, 0)),
        scratch_shapes=[pltpu.VMEM((d, ff), BF16), pltpu.VMEM((d, ff), BF16), pltpu.VMEM((ff, d), BF16)],
    )
    return pl.pallas_call(
        _experts_body,
        grid_spec=grid_spec,
        out_shape=jax.ShapeDtypeStruct((nr, d), F32),
        compiler_params=_cparams("arbitrary"),
        name="experts",
    )(block_expert, block_valid, n_used, xs, wg, wu, wd)


def _combine_body(dest_ref, info_ref, x1_ref, npost_ref, gt_ref, yb_ref, o_ref, buf, sem, *, tc, rc):
    def row_copy(r, k):
        return pltpu.make_async_copy(yb_ref.at[pl.ds(dest_ref[MOE_TOPK * r + k], 1), :],
                                     buf.at[k, pl.ds(r, 1), :], sem)

    def start(r, carry):
        for k in range(MOE_TOPK):
            row_copy(r, k).start()
        return carry

    def wait(r, carry):
        for k in range(MOE_TOPK):
            row_copy(r, k).wait()
        return carry

    lax.fori_loop(0, tc, start, 0)
    lax.fori_loop(0, tc, wait, 0)

    gain = npost_ref[...]
    gate = gt_ref[...]

    def chunk(i, carry):
        r0 = pl.multiple_of(i * rc, rc)
        info = info_ref[pl.ds(r0, rc), :]
        y = info[:, 4:5] * buf[0, pl.ds(r0, rc), :] + info[:, 5:6] * buf[1, pl.ds(r0, rc), :]
        yn = y * lax.rsqrt(jnp.mean(y * y, axis=-1, keepdims=True) + NORM_EPS) * gain
        o_ref[pl.ds(r0, rc), :] = x1_ref[pl.ds(r0, rc), :] + gate * yn
        return carry

    lax.fori_loop(0, tc // rc, chunk, 0)


def _combine(dest_flat, info, x1, npost, mod3, yb, seq):
    t, d = x1.shape
    tc = min(512, seq)
    rc = min(128, tc)
    per_b = seq // tc
    return pl.pallas_call(
        functools.partial(_combine_body, tc=tc, rc=rc),
        grid=(t // tc,),
        in_specs=[pl.BlockSpec((MOE_TOPK * tc,), lambda i: (i,), memory_space=pltpu.SMEM),
                  pl.BlockSpec((tc, LANES), lambda i: (i, 0)),
                  pl.BlockSpec((tc, d), lambda i: (i, 0)),
                  pl.BlockSpec((1, d), lambda i: (0, 0)),
                  pl.BlockSpec((None, 1, d), lambda i: (i // per_b, 0, 5)),
                  pl.BlockSpec(memory_space=pl.ANY)],
        out_specs=pl.BlockSpec((tc, d), lambda i: (i, 0)),
        out_shape=jax.ShapeDtypeStruct((t, d), F32),
        scratch_shapes=[pltpu.VMEM((MOE_TOPK, tc, d), F32), pltpu.SemaphoreType.DMA(())],
        compiler_params=_cparams("arbitrary"),
        name="combine",
    )(dest_flat, info, x1, npost, mod3, yb)


def _rope_tables(positions):
    half = ROPE_DIM // 2
    inv_freq = jnp.power(ROPE_THETA, -jnp.arange(half, dtype=F32) * (2.0 / ROPE_DIM))
    ang = positions.astype(F32)[..., None] * inv_freq
    cos, sin = jnp.cos(ang), jnp.sin(ang)
    rest = HEAD_DIM - ROPE_DIM
    cos_t = jnp.concatenate([cos, cos, jnp.ones(cos.shape[:-1] + (rest,), F32)], axis=-1)
    sin_t = jnp.concatenate([-sin, sin, jnp.zeros(sin.shape[:-1] + (rest,), F32)], axis=-1)
    return cos_t, sin_t


def _pad_lanes(v, n=LANES):
    return jnp.pad(v, [(0, 0)] * (v.ndim - 1) + [(0, n - v.shape[-1])])


def _mixer_and_router(x, mod3, positions, norm_mix_pre, norm_mix_post, norm_ffn_pre, w_in, conv_w,
                      dn_a_log, dn_dt_bias, dn_out_norm, w_branch_moba, w_branch_delta, w_out,
                      router_group_w, router_group_b, router_expert_w, router_expert_b):
    bsz, seq, d = x.shape
    t = bsz * seq
    mw = MOBA_HEADS * HEAD_DIM
    dw = DN_HEADS * HEAD_DIM
    o_qa, o_ka, o_va = 0, mw, 2 * mw
    o_dn = 3 * mw
    o_z = o_dn + 3 * dw
    o_ba = o_z + dw
    o_ga = o_ba + 2 * DN_HEADS
    o_gb = o_ga + d
    w_perm = jnp.concatenate(
        [w_in[:, o_ga:o_ga + d], w_in[:, o_gb:o_gb + d], w_in[:, o_qa:o_ba],
         _pad_lanes(w_in[:, o_ba:o_ga], 4 * LANES)], axis=1).astype(BF16)
    x2 = x.reshape(t, d)
    proj2 = _inproj(x2, norm_mix_pre[None, :], mod3, w_perm, seq, sc_chunk=1, sh_chunk=0)
    proj = proj2.reshape(bsz, seq, -1)
    c0 = 2 * d // LANES
    nh = MOBA_HEADS

    cos_t, sin_t = _rope_tables(positions)
    ya = _moba(proj, cos_t, sin_t, q_blk0=c0, k_blk0=c0 + nh, v_blk0=c0 + 2 * nh)

    ba_blk = (2 * d + 3 * mw + 3 * dw + dw) // LANES
    par = jnp.zeros((8, LANES), F32)
    par = par.at[0, DN_HEADS:2 * DN_HEADS].set(dn_a_log.astype(F32))
    par = par.at[1, DN_HEADS:2 * DN_HEADS].set(dn_dt_bias.astype(F32))
    beta, gcum, glast = _gates(proj, par, ba_blk=ba_blk)
    ngrp = seq // DN_GROUP
    grow = jnp.transpose(gcum[:, :, DN_HEADS:2 * DN_HEADS], (0, 2, 1)).reshape(bsz * DN_HEADS, ngrp, 1, DN_GROUP)
    d0 = c0 + 3 * nh
    yb = _deltanet(proj, conv_w, beta, gcum, glast, grow, dn_out_norm[None, :].astype(F32),
                   q_blk0=d0, k_blk0=d0 + DN_HEADS, v_blk0=d0 + 2 * DN_HEADS, z_blk0=d0 + 3 * DN_HEADS)

    wr = _pad_lanes(jnp.concatenate([router_group_w, router_expert_w], axis=1))
    br = _pad_lanes(jnp.concatenate([router_group_b, router_expert_b])[None, :])
    return _merge(ya.reshape(t, mw), yb.reshape(t, dw), proj2, x2,
                  w_branch_moba.astype(BF16), w_branch_delta.astype(BF16), w_out.astype(BF16), wr, br,
                  norm_mix_post[None, :], norm_ffn_pre[None, :], mod3, seq, ga_blk=0, gb_blk=1)


def _moe(x1, u2, logits, mod3, norm_ffn_post, w_gate, w_up, w_down, seq):
    t, d = x1.shape
    info, counts = _route(logits)
    counts = counts[0, MOE_GROUPS:MOE_GROUPS + MOE_EXPERTS].astype(jnp.int32)
    rb = MOE_ROWS
    padded = (counts + rb - 1) // rb * rb
    pad_end = jnp.cumsum(padded)
    pad_start = pad_end - padded
    eid = info[:, 0:MOE_TOPK].astype(jnp.int32)
    rank = info[:, MOE_TOPK:2 * MOE_TOPK].astype(jnp.int32)
    dest_flat = (pad_start[eid] + rank).reshape(-1)
    n_blocks = (t * MOE_TOPK + MOE_EXPERTS * (rb - 1)) // rb + 1
    n_used = (pad_end[-1] // rb).astype(jnp.int32)
    blk_row = jnp.minimum(jnp.arange(n_blocks, dtype=jnp.int32), n_used - 1) * rb
    block_expert = jnp.minimum(jnp.sum(pad_end[None, :] <= blk_row[:, None], axis=1),
                               MOE_EXPERTS - 1).astype(jnp.int32)
    block_valid = jnp.clip(counts[block_expert] - (blk_row - pad_start[block_expert]), 0, rb).astype(jnp.int32)
    xs = _dispatch(dest_flat, u2, n_blocks * rb)
    ys = _experts(block_expert, block_valid, n_used[None], xs, w_gate, w_up, w_down)
    return _combine(dest_flat, info, x1, norm_ffn_post[None, :], mod3, ys, seq)


def kernel(x, c, positions, w_ada, b_ada, norm_mix_pre, norm_mix_post, norm_ffn_pre, norm_ffn_post, w_in, conv_w, dn_a_log, dn_dt_bias, dn_out_norm, w_branch_moba, w_branch_delta, w_out, router_group_w, router_group_b, router_expert_w, router_expert_b, expert_w_gate, expert_w_up, expert_w_down):
    bsz, seq, d = x.shape
    depth = w_ada.shape[0]
    for layer in range(depth):
        mod = _ada(c, w_ada[layer], b_ada[layer][None, :])
        mod3 = mod.reshape(bsz, 1, -1)
        x1, u2, logits = _mixer_and_router(
            x, mod3, positions, norm_mix_pre[layer], norm_mix_post[layer], norm_ffn_pre[layer], w_in[layer],
            conv_w[layer], dn_a_log[layer], dn_dt_bias[layer], dn_out_norm[layer], w_branch_moba[layer],
            w_branch_delta[layer], w_out[layer], router_group_w[layer], router_group_b[layer],
            router_expert_w[layer], router_expert_b[layer])
        out = _moe(x1, u2, logits, mod3, norm_ffn_post[layer], expert_w_gate[layer], expert_w_up[layer],
                   expert_w_down[layer], seq)
        x = out.reshape(bsz, seq, d)
    return x
```

```python
import functools
import math

import jax
import jax.numpy as jnp
from jax import lax
from jax.experimental import pallas as pl
from jax.experimental.pallas import tpu as pltpu

F32 = jnp.float32
BF16 = jnp.bfloat16
HI = lax.Precision.HIGHEST

NORM_EPS = 1e-6
HEAD_DIM = 128
MOBA_HEADS = 8
MOBA_BLOCK = 256
MOBA_TOPK = 3
ROPE_THETA = 500000.0
ROPE_DIM = HEAD_DIM // 4
DN_HEADS = 8
DN_CONV_WIDTH = 4
DN_CHUNK = 64
DN_GROUP = 256
DN_HEADS_PER_STEP = 4
DN_CONV_PAD = 8
MOE_GROUPS = 4
MOE_EXPERTS_PER_GROUP = 8
MOE_EXPERTS = MOE_GROUPS * MOE_EXPERTS_PER_GROUP
MOE_TOPK = 2
MOE_ROWS = 256
DMA_LOOP_UNROLL = 8
MERGE_SUB_ROWS = 128
LANES = 128
NEG = -1e30

VMEM_LIMIT = 56 * 1024 * 1024


def _cparams(*sem):
    return pltpu.CompilerParams(dimension_semantics=sem, vmem_limit_bytes=VMEM_LIMIT)


def _mm(a, b, precision=None):
    return jnp.dot(a, b, precision=precision, preferred_element_type=F32)


def _nt(a, b, precision=None):
    return lax.dot_general(a, b, (((1,), (1,)), ((), ())), precision=precision,
                           preferred_element_type=F32)


def _eye(n, dtype):
    r = lax.broadcasted_iota(jnp.int32, (n, n), 0)
    c = lax.broadcasted_iota(jnp.int32, (n, n), 1)
    return jnp.where(r == c, 1.0, 0.0).astype(dtype)


def _silu(x):
    return x * jax.nn.sigmoid(x)


def _softplus(x):
    return jnp.maximum(x, 0.0) + jnp.log1p(jnp.exp(-jnp.abs(x)))


def _pack_bf16_pair(lo, hi):
    def rne(x):
        b = pltpu.bitcast(x, jnp.uint32)
        return b + jnp.uint32(0x7FFF) + ((b >> 16) & jnp.uint32(1))
    return (rne(hi) & jnp.uint32(0xFFFF0000)) | (rne(lo) >> 16)


def _unpack_bf16_pair(p):
    return pltpu.bitcast(p << 16, F32), pltpu.bitcast(p & jnp.uint32(0xFFFF0000), F32)


def _ada_body(c_ref, w_ref, b_ref, o_ref):
    o_ref[...] = _mm(_silu(c_ref[...]), w_ref[...], HI) + b_ref[...]


def _ada(c, w, b):
    bsz, d = c.shape
    n = w.shape[1]
    tn = 1024
    return pl.pallas_call(
        _ada_body,
        grid=(n // tn,),
        in_specs=[pl.BlockSpec((bsz, d), lambda j: (0, 0)),
                  pl.BlockSpec((d, tn), lambda j: (0, j)),
                  pl.BlockSpec((1, tn), lambda j: (0, j))],
        out_specs=pl.BlockSpec((bsz, tn), lambda j: (0, j)),
        out_shape=jax.ShapeDtypeStruct((bsz, n), F32),
        compiler_params=_cparams("parallel"),
        name="ada",
    )(c, w, b)


def _inproj_body(x_ref, g_ref, sc_ref, sh_ref, w_ref, ws_ref, o_ref, os_ref, u_ref, *, tm, rc):
    @pl.when(pl.program_id(1) == 0)
    def _():
        gain = g_ref[...]
        scale = 1.0 + sc_ref[...]
        shift = sh_ref[...]

        def chunk(i, carry):
            r0 = pl.multiple_of(i * rc, rc)
            x = x_ref[pl.ds(r0, rc), :]
            y = x * lax.rsqrt(jnp.mean(x * x, axis=-1, keepdims=True) + NORM_EPS) * gain
            u_ref[pl.ds(r0, rc), :] = (y * scale + shift).astype(BF16)
            return carry

        lax.fori_loop(0, tm // rc, chunk, 0)
        os_ref[...] = _mm(u_ref[...], ws_ref[...])

    o_ref[...] = _mm(u_ref[...], w_ref[...]).astype(o_ref.dtype)


def _inproj(x2, gain, mod3, w, w_small, seq, *, sc_chunk, sh_chunk):
    t, d = x2.shape
    n = w.shape[1]
    tm = min(1024, seq)
    tn = 1024
    per_b = seq // tm
    rc = min(256, tm)
    return pl.pallas_call(
        functools.partial(_inproj_body, tm=tm, rc=rc),
        grid=(t // tm, n // tn),
        in_specs=[pl.BlockSpec((tm, d), lambda i, j: (i, 0)),
                  pl.BlockSpec((1, d), lambda i, j: (0, 0)),
                  pl.BlockSpec((None, 1, d), lambda i, j: (i // per_b, 0, sc_chunk)),
                  pl.BlockSpec((None, 1, d), lambda i, j: (i // per_b, 0, sh_chunk)),
                  pl.BlockSpec((d, tn), lambda i, j: (0, j)),
                  pl.BlockSpec((d, LANES), lambda i, j: (0, 0))],
        out_specs=[pl.BlockSpec((tm, tn), lambda i, j: (i, j)),
                   pl.BlockSpec((tm, LANES), lambda i, j: (i, 0))],
        out_shape=[jax.ShapeDtypeStruct((t, n), BF16),
                   jax.ShapeDtypeStruct((t, LANES), F32)],
        scratch_shapes=[pltpu.VMEM((tm, d), BF16)],
        compiler_params=_cparams("parallel", "arbitrary"),
        name="inproj",
    )(x2, gain, mod3, mod3, w, w_small)


def _moba_body(q_ref, k_ref, v_ref, cos_ref, sin_ref, o_ref, kr_s, vt_s, km_s, *, nblk):
    blk = MOBA_BLOCK
    half = ROPE_DIM // 2
    lane = lax.broadcasted_iota(jnp.int32, (blk, HEAD_DIM), 1)
    eye_d = _eye(HEAD_DIM, BF16)
    eye_b = _eye(blk, BF16)
    scale = HEAD_DIM ** -0.5

    def rope(xf, rows):
        partner = jnp.where(lane < half, pltpu.roll(xf, HEAD_DIM - half, 1), pltpu.roll(xf, half, 1))
        return xf * cos_ref[rows, :] + partner * sin_ref[rows, :]

    km_s[...] = jnp.zeros_like(km_s)
    for j in range(nblk):
        rows = slice(j * blk, (j + 1) * blk)
        kc = rope(k_ref[rows, :].astype(F32), rows)
        kr_s[rows, :] = kc.astype(BF16)
        km_s[j:j + 1, :] = jnp.mean(kc, axis=0, keepdims=True)
        vt_s[:, rows] = _nt(eye_d, v_ref[rows, :]).astype(BF16)

    key_i = lax.broadcasted_iota(jnp.int32, (blk, blk), 0)
    qry_i = lax.broadcasted_iota(jnp.int32, (blk, blk), 1)
    blk_i = lax.broadcasted_iota(jnp.int32, (8, blk), 0)
    causal_bias = jnp.where(key_i <= qry_i, 0.0, NEG)

    def rows(j):
        return slice(j * blk, (j + 1) * blk)

    def begin(i):
        qc = rope(q_ref[rows(i), :].astype(F32), rows(i))
        st = dict(i=i, qs=(qc * scale).astype(BF16), scores=[], m=None, bias=None)
        if i > 0:
            g_t = _nt(km_s[...], qc, HI)
            rank = jnp.zeros((8, blk), F32)
            for jp in range(i):
                row = g_t[jp:jp + 1, :]
                beats = (row > g_t) | ((row == g_t) & (jp < blk_i))
                rank = rank + jnp.where(beats, 1.0, 0.0)
            sel = (rank < float(MOBA_TOPK)) & (blk_i < i)
            st["bias"] = jnp.where(sel, 0.0, NEG)
        return st

    def score(st, j):
        i = st["i"]
        s = _nt(kr_s[rows(j), :], st["qs"])
        if j == i:
            s = s + causal_bias
        st["scores"].append(s)
        mj = jnp.max(s, axis=0, keepdims=True)
        if j < i:
            mj = mj + st["bias"][j:j + 1, :]
        st["m"] = mj if st["m"] is None else jnp.maximum(st["m"], mj)

    def accumulate(st, j):
        i = st["i"]
        if j == 0:
            st["den"] = jnp.zeros((1, blk), F32)
            st["acc"] = jnp.zeros((HEAD_DIM, blk), F32)
        shift = (st["bias"][j:j + 1, :] - st["m"]) if j < i else -st["m"]
        p = jnp.exp(st["scores"][j] + shift)
        st["den"] = st["den"] + jnp.sum(p, axis=0, keepdims=True)
        st["acc"] = st["acc"] + _mm(vt_s[:, rows(j)], p.astype(BF16))
        if j == i:
            o_t = (st["acc"] / st["den"]).astype(BF16)
            o_ref[rows(i), :] = _nt(eye_b, o_t).astype(o_ref.dtype)

    prev = None
    for i in range(nblk):
        cur = begin(i)
        for j in range(i + 1):
            score(cur, j)
            if prev is not None and j <= prev["i"]:
                accumulate(prev, j)
        prev = cur
    for j in range(nblk):
        accumulate(prev, j)


def _moba(proj, cos_t, sin_t, *, q_blk0, k_blk0, v_blk0):
    bsz, seq, _ = proj.shape
    nblk = seq // MOBA_BLOCK
    assert seq % MOBA_BLOCK == 0 and 1 <= nblk <= 8
    hd = HEAD_DIM
    return pl.pallas_call(
        functools.partial(_moba_body, nblk=nblk),
        grid=(bsz, MOBA_HEADS),
        in_specs=[pl.BlockSpec((None, seq, hd), lambda b, h: (b, 0, q_blk0 + h)),
                  pl.BlockSpec((None, seq, hd), lambda b, h: (b, 0, k_blk0 + h)),
                  pl.BlockSpec((None, seq, hd), lambda b, h: (b, 0, v_blk0 + h)),
                  pl.BlockSpec((None, seq, hd), lambda b, h: (b, 0, 0)),
                  pl.BlockSpec((None, seq, hd), lambda b, h: (b, 0, 0))],
        out_specs=pl.BlockSpec((None, seq, hd), lambda b, h: (b, 0, h)),
        out_shape=jax.ShapeDtypeStruct((bsz, seq, MOBA_HEADS * hd), BF16),
        scratch_shapes=[pltpu.VMEM((seq, hd), BF16),
                        pltpu.VMEM((hd, seq), BF16),
                        pltpu.VMEM((8, hd), F32)],
        compiler_params=_cparams("parallel", "parallel"),
        name="moba",
    )(proj, proj, proj, cos_t, sin_t)


def _chunk_masks(n, chunk):
    r = lax.broadcasted_iota(jnp.int32, (n, n), 0)
    c = lax.broadcasted_iota(jnp.int32, (n, n), 1)
    shift = int(math.log2(chunk))
    same = jnp.right_shift(r, shift) == jnp.right_shift(c, shift)
    return r, c, same


def _gates_body(ba_ref, par_ref, beta_ref, g_ref, gl_ref, *, seq):
    grp = DN_GROUP
    r, c, same = _chunk_masks(grp, DN_CHUNK)
    low = jnp.where(same & (c <= r), 1.0, 0.0)
    ones = jnp.where(same, 1.0, 0.0)
    neg_a = -jnp.exp(par_ref[0:1, :])
    dt_b = par_ref[1:2, :]
    for i in range(seq // grp):
        rows = slice(i * grp, (i + 1) * grp)
        x = ba_ref[rows, :].astype(F32)
        beta_ref[rows, :] = jax.nn.sigmoid(x)
        g = neg_a * _softplus(x + dt_b)
        g_ref[rows, :] = _mm(low, g, HI)
        gl_ref[rows, :] = _mm(ones, g, HI)


def _gates(ba, par):
    bsz, seq, _ = ba.shape
    spec = pl.BlockSpec((None, seq, LANES), lambda b: (b, 0, 0))
    return pl.pallas_call(
        functools.partial(_gates_body, seq=seq),
        grid=(bsz,),
        in_specs=[spec, pl.BlockSpec((8, LANES), lambda b: (0, 0))],
        out_specs=[spec, spec, spec],
        out_shape=[jax.ShapeDtypeStruct((bsz, seq, LANES), F32)] * 3,
        compiler_params=_cparams("parallel"),
        name="dn_gates",
    )(ba, par)


def _dn_body(q_ref, k_ref, v_ref, z_ref, cwq_ref, cwk_ref, cwv_ref, beta_ref, g_ref, gl_ref, grow_ref,
             gain_ref, o_ref, xpad_s, qn_s, kn_s, vn_s, state_s, vnew_s, *, seq, hp):
    hg = pl.program_id(1)
    grp = DN_GROUP
    chunk = DN_CHUNK
    ngrp = seq // grp
    hd = HEAD_DIM
    pad = DN_CONV_PAD

    crows = 128
    xpad_s[0:pad, :] = jnp.zeros((pad, hp * hd), F32)
    for src, cw_ref, dst, l2, post in ((q_ref, cwq_ref, qn_s, True, hd ** -0.5),
                                       (k_ref, cwk_ref, kn_s, True, None),
                                       (v_ref, cwv_ref, vn_s, False, None)):
        def copy(i, carry, src=src):
            r0 = pl.multiple_of(i * crows, crows)
            xpad_s[pl.ds(pad + r0, crows), :] = src[pl.ds(r0, crows), :].astype(F32)
            return carry

        lax.fori_loop(0, seq // crows, copy, 0)
        cw = cw_ref[...]

        def conv(i, carry, cw=cw, dst=dst, l2=l2, post=post):
            r0 = pl.multiple_of(i * crows, crows)
            win = xpad_s.at[pl.ds(r0, crows + pad), :]
            acc = None
            for j in range(DN_CONV_WIDTH):
                off = pad - (DN_CONV_WIDTH - 1) + j
                term = win[off:off + crows, :] * cw[j:j + 1, :]
                acc = term if acc is None else acc + term
            y = _silu(acc)
            for hh in range(hp):
                cols = slice(hh * hd, (hh + 1) * hd)
                yh = y[:, cols]
                if l2:
                    yh = yh * lax.rsqrt(jnp.sum(yh * yh, axis=-1, keepdims=True) + NORM_EPS)
                if post is not None:
                    yh = yh * post
                dst[pl.ds(r0, crows), cols] = yh
            return carry

        lax.fori_loop(0, seq // crows, conv, 0)

    r, c, same = _chunk_masks(grp, chunk)
    low_incl = same & (c <= r)
    low_strict = same & (c < r)
    eye_g = jnp.where(r == c, 1.0, 0.0)
    eye_d = _eye(hd, BF16)
    lane = lax.broadcasted_iota(jnp.int32, (grp, LANES), 1)
    col_chunk = jnp.right_shift(lax.broadcasted_iota(jnp.int32, (hd, grp), 1), int(math.log2(chunk)))
    gain = gain_ref[...]
    n_double = int(math.log2(chunk)) - 1

    def pick(ref, r0, lane_id):
        return jnp.sum(jnp.where(lane == lane_id, ref[pl.ds(r0, grp), :], 0.0), axis=-1, keepdims=True)

    state_s[...] = jnp.zeros_like(state_s)
    heads = range(hp)

    def group(gi, carry):
        r0 = pl.multiple_of(gi * grp, grp)
        cols = [slice(hh * hd, (hh + 1) * hd) for hh in heads]
        head = [hg * hp + hh for hh in heads]
        q = [qn_s[pl.ds(r0, grp), cols[hh]] for hh in heads]
        k = [kn_s[pl.ds(r0, grp), cols[hh]] for hh in heads]
        v = [vn_s[pl.ds(r0, grp), cols[hh]] for hh in heads]
        beta = [pick(beta_ref, r0, head[hh]) for hh in heads]
        g_col = [pick(g_ref, r0, DN_HEADS + head[hh]) for hh in heads]
        gl_col = [pick(gl_ref, r0, DN_HEADS + head[hh]) for hh in heads]
        decay = [jnp.exp(jnp.minimum(g_col[hh] - grow_ref[hh, gi], 0.0)) for hh in heads]
        e_g = [jnp.exp(g_col[hh]) for hh in heads]
        kb = [k[hh] * beta[hh] for hh in heads]
        vb = [(v[hh] * beta[hh]).astype(BF16) for hh in heads]
        k16 = [k[hh].astype(BF16) for hh in heads]
        n_mat = [jnp.where(low_strict, _nt(kb[hh].astype(BF16), k16[hh]) * decay[hh], 0.0) for hh in heads]
        p = [(-n_mat[hh]).astype(BF16) for hh in heads]
        x = [eye_g - n_mat[hh] for hh in heads]
        for _ in range(n_double):
            p2 = [_mm(p[hh], p[hh]).astype(BF16) for hh in heads]
            x = [x[hh] + _mm(x[hh].astype(BF16), p2[hh]) for hh in heads]
            p = p2
        x16 = [x[hh].astype(BF16) for hh in heads]
        u = [_mm(x16[hh], vb[hh]) for hh in heads]
        w = [_mm(x16[hh], (kb[hh] * e_g[hh]).astype(BF16)).astype(BF16) for hh in heads]
        qk = [jnp.where(low_incl, _nt(q[hh].astype(BF16), k16[hh]) * decay[hh], 0.0).astype(BF16) for hh in heads]
        q_dec = [(q[hh] * e_g[hh]).astype(BF16) for hh in heads]
        k_tail_t = [_nt(eye_d, (k[hh] * jnp.exp(gl_col[hh] - g_col[hh])).astype(BF16)).astype(BF16)
                    for hh in heads]
        vnew_s[...] = jnp.zeros_like(vnew_s)
        for ci in range(grp // chunk):
            rows = slice(ci * chunk, (ci + 1) * chunk)
            gl_row = jnp.exp(gl_ref[pl.ds(r0 + ci * chunk, 1), :])
            for hh in heads:
                state = state_s[hh]
                s16 = state.astype(BF16)
                v_new = u[hh][rows, :] - _mm(w[hh][rows, :], s16)
                vnew_s[hh, rows, :] = v_new.astype(BF16)
                o = _mm(q_dec[hh][rows, :], s16) + _mm(qk[hh][rows, :], vnew_s[hh])
                chunk_decay = jnp.sum(jnp.where(lane[0:1, :] == DN_HEADS + head[hh], gl_row, 0.0),
                                      axis=-1, keepdims=True)
                kt = jnp.where(col_chunk == ci, k_tail_t[hh], jnp.zeros_like(k_tail_t[hh]))
                state_s[hh] = state * chunk_decay + _mm(kt, vnew_s[hh])
                on = o * lax.rsqrt(jnp.mean(o * o, axis=-1, keepdims=True) + NORM_EPS) * gain
                zz = z_ref[pl.ds(r0 + ci * chunk, chunk), cols[hh]].astype(F32)
                o_ref[pl.ds(r0 + ci * chunk, chunk), cols[hh]] = (on * _silu(zz)).astype(o_ref.dtype)
        return carry

    lax.fori_loop(0, ngrp, group, 0)


def _deltanet(proj, conv_w, beta, gcum, glast, grow, gain, *, q_blk0, k_blk0, v_blk0, z_blk0):
    bsz, seq, _ = proj.shape
    hd = HEAD_DIM
    nh = DN_HEADS
    assert seq % DN_GROUP == 0
    ngrp = seq // DN_GROUP

    hp = DN_HEADS_PER_STEP
    wide = hp * hd
    assert nh % hp == 0 and all(b0 % hp == 0 for b0 in (q_blk0, k_blk0, v_blk0, z_blk0))

    def col(blk0):
        return pl.BlockSpec((None, seq, wide), lambda b, h: (b, 0, blk0 // hp + h))

    def cw(blk0):
        return pl.BlockSpec((DN_CONV_WIDTH, wide), lambda b, h: (0, blk0 // hp + h))

    full = pl.BlockSpec((None, seq, LANES), lambda b, h: (b, 0, 0))
    return pl.pallas_call(
        functools.partial(_dn_body, seq=seq, hp=hp),
        grid=(bsz, nh // hp),
        in_specs=[col(q_blk0), col(k_blk0), col(v_blk0), col(z_blk0),
                  cw(0), cw(nh), cw(2 * nh),
                  full, full, full,
                  pl.BlockSpec((hp, ngrp, 1, DN_GROUP), lambda b, h: (b * (nh // hp) + h, 0, 0, 0)),
                  pl.BlockSpec((1, hd), lambda b, h: (0, 0))],
        out_specs=pl.BlockSpec((None, seq, wide), lambda b, h: (b, 0, h)),
        out_shape=jax.ShapeDtypeStruct((bsz, seq, nh * hd), BF16),
        scratch_shapes=[pltpu.VMEM((seq + DN_CONV_PAD, wide), F32)] + [pltpu.VMEM((seq, wide), F32)] * 3
        + [pltpu.VMEM((hp, hd, hd), F32), pltpu.VMEM((hp, DN_GROUP, hd), BF16)],
        compiler_params=_cparams("parallel", "parallel"),
        name="deltanet",
    )(proj, proj, proj, proj, conv_w, conv_w, conv_w, beta, gcum, glast, grow, gain)


def _merge_body(ya_ref, yb_ref, ga_ref, gb_ref, x_ref, wm_ref, wd_ref, wo_ref, wr_ref, br_ref,
                npost_ref, npre_ref, gt_ref, sc_ref, sh_ref, x1_ref, u2_ref, lg_ref):
    sub = MERGE_SUB_ROWS
    for s in range(x_ref.shape[0] // sub):
        rows = slice(s * sub, (s + 1) * sub)
        ma = _mm(ya_ref[rows, :], wm_ref[...])
        mb = _mm(yb_ref[rows, :], wd_ref[...])
        merged = (jax.nn.sigmoid(ga_ref[rows, :].astype(F32)) * ma
                  + jax.nn.sigmoid(gb_ref[rows, :].astype(F32)) * mb)
        y = _mm(merged.astype(BF16), wo_ref[...])
        yn = y * lax.rsqrt(jnp.mean(y * y, axis=-1, keepdims=True) + NORM_EPS) * npost_ref[...]
        x1 = x_ref[rows, :] + gt_ref[...] * yn
        x1_ref[rows, :] = x1
        un = x1 * lax.rsqrt(jnp.mean(x1 * x1, axis=-1, keepdims=True) + NORM_EPS) * npre_ref[...]
        u2 = un * (1.0 + sc_ref[...]) + sh_ref[...]
        half = u2.shape[1] // 2
        u2_ref[rows, :] = _pack_bf16_pair(u2[:, :half], u2[:, half:])
        u_hi = u2.astype(BF16)
        u_lo = (u2 - u_hi.astype(F32)).astype(BF16)
        lg2 = _nt(wr_ref[...], u_hi)
        lg_ref[:, rows] = lg2[:LANES, :] + lg2[LANES:, :] + _nt(wr_ref[:LANES, :], u_lo) + br_ref[...]


def _merge(ya, yb, proj2, x2, wm, wd, wo, wr, br, npost, npre, mod3, seq, *, ga_blk, gb_blk):
    t, d = x2.shape
    wa = ya.shape[1]
    tm = min(256, seq)
    per_b = seq // tm
    const = lambda i: (0, 0)
    once = dict(pipeline_mode=pl.Buffered(1))

    def modspec(chunk):
        return pl.BlockSpec((None, 1, d), lambda i: (i // per_b, 0, chunk))

    return pl.pallas_call(
        _merge_body,
        grid=(t // tm,),
        in_specs=[pl.BlockSpec((tm, wa), lambda i: (i, 0)),
                  pl.BlockSpec((tm, wa), lambda i: (i, 0)),
                  pl.BlockSpec((tm, d), lambda i: (i, ga_blk)),
                  pl.BlockSpec((tm, d), lambda i: (i, gb_blk)),
                  pl.BlockSpec((tm, d), lambda i: (i, 0)),
                  pl.BlockSpec((wa, d), const, **once),
                  pl.BlockSpec((wa, d), const, **once),
                  pl.BlockSpec((d, d), const, **once),
                  pl.BlockSpec((2 * LANES, d), const, **once),
                  pl.BlockSpec((LANES, 1), const),
                  pl.BlockSpec((1, d), const),
                  pl.BlockSpec((1, d), const),
                  modspec(2), modspec(4), modspec(3)],
        out_specs=[pl.BlockSpec((tm, d), lambda i: (i, 0)),
                   pl.BlockSpec((tm, d // 2), lambda i: (i, 0)),
                   pl.BlockSpec((LANES, tm), lambda i: (0, i))],
        out_shape=[jax.ShapeDtypeStruct((t, d), F32),
                   jax.ShapeDtypeStruct((t, d // 2), jnp.uint32),
                   jax.ShapeDtypeStruct((LANES, t), F32)],
        compiler_params=_cparams("parallel"),
        name="merge",
    )(ya, yb, proj2, proj2, x2, wm, wd, wo, wr, br, npost, npre, mod3, mod3, mod3)


def _route_body(lg_ref, info_ref, col_ref, cnt_ref, run_s, *, tr):
    @pl.when(pl.program_id(0) == 0)
    def _():
        run_s[...] = jnp.zeros_like(run_s)

    lg = lg_ref[...]
    row = lax.broadcasted_iota(jnp.int32, (LANES, tr), 0)
    row_f = row.astype(F32)
    big = float(LANES)

    def first_max(vals, mask):
        mx = jnp.max(jnp.where(mask, vals, NEG), axis=0, keepdims=True)
        idx = jnp.min(jnp.where(mask & (vals == mx), row_f, big), axis=0, keepdims=True)
        return mx, idx

    gmask = row < MOE_GROUPS
    gmax, gidx = first_max(lg, gmask)
    p_group = 1.0 / jnp.sum(jnp.where(gmask, jnp.exp(lg - gmax), 0.0), axis=0, keepdims=True)
    lo = float(MOE_GROUPS) + gidx * float(MOE_EXPERTS_PER_GROUP)
    emask = (row_f >= lo) & (row_f < lo + float(MOE_EXPERTS_PER_GROUP))
    m1, i1 = first_max(lg, emask)
    m2, i2 = first_max(lg, emask & (row_f != i1))
    e2 = jnp.exp(m2 - m1)
    w1 = p_group / (1.0 + e2)
    w2 = p_group * e2 / (1.0 + e2)
    oh1 = row_f == i1
    oh2 = row_f == i2
    oh = jnp.where(oh1 | oh2, 1.0, 0.0).astype(BF16)
    r = lax.broadcasted_iota(jnp.int32, (tr, tr), 0)
    c = lax.broadcasted_iota(jnp.int32, (tr, tr), 1)
    before = jnp.where(r < c, 1.0, 0.0).astype(BF16)
    prefix = _mm(oh, before) + run_s[:, 0:1]
    rank1 = jnp.sum(jnp.where(oh1, prefix, 0.0), axis=0, keepdims=True)
    rank2 = jnp.sum(jnp.where(oh2, prefix, 0.0), axis=0, keepdims=True)
    run_s[...] = run_s[...] + jnp.sum(oh.astype(F32), axis=1, keepdims=True)
    goff = float(MOE_GROUPS)
    info = jnp.where(row == 0, i1 - goff, 0.0)
    info = jnp.where(row == 1, i2 - goff, info)
    info = jnp.where(row == 2, rank1, info)
    info = jnp.where(row == 3, rank2, info)
    info = jnp.where(row == 4, w1, info)
    info = jnp.where(row == 5, w2, info)
    info_ref[...] = info[0:8, :]
    col_ref[...] = _nt(_eye(tr, F32), info, HI)
    cnt_ref[...] = run_s[...]


def _route(logits_t):
    t = logits_t.shape[1]
    tr = min(256, t)
    return pl.pallas_call(
        functools.partial(_route_body, tr=tr),
        grid=(t // tr,),
        in_specs=[pl.BlockSpec((LANES, tr), lambda i: (0, i))],
        out_specs=[pl.BlockSpec((8, tr), lambda i: (0, i)),
                   pl.BlockSpec((tr, LANES), lambda i: (i, 0)),
                   pl.BlockSpec((LANES, LANES), lambda i: (0, 0))],
        out_shape=[jax.ShapeDtypeStruct((8, t), F32),
                   jax.ShapeDtypeStruct((t, LANES), F32),
                   jax.ShapeDtypeStruct((LANES, LANES), F32)],
        scratch_shapes=[pltpu.VMEM((LANES, LANES), F32)],
        compiler_params=_cparams("arbitrary"),
        name="route",
    )(logits_t)


def _dispatch_body(d0_ref, d1_ref, u_ref, xs_ref, sem, *, td):
    dests = (d0_ref, d1_ref)

    def row_copy(r, k):
        return pltpu.make_async_copy(u_ref.at[pl.ds(r, 1), :], xs_ref.at[pl.ds(dests[k][r], 1), :], sem)

    def start(r, carry):
        for k in range(MOE_TOPK):
            row_copy(r, k).start()
        return carry

    def wait(r, carry):
        for k in range(MOE_TOPK):
            row_copy(r, k).wait()
        return carry

    lax.fori_loop(0, td, start, 0, unroll=DMA_LOOP_UNROLL)
    lax.fori_loop(0, td, wait, 0, unroll=DMA_LOOP_UNROLL)


def _dispatch(dest, u2, n_rows):
    t, d = u2.shape
    td = min(1024, t)
    nt = t // td
    return pl.pallas_call(
        functools.partial(_dispatch_body, td=td),
        grid=(nt,),
        in_specs=[pl.BlockSpec((td,), lambda i: (i,), memory_space=pltpu.SMEM),
                  pl.BlockSpec((td,), lambda i: (nt + i,), memory_space=pltpu.SMEM),
                  pl.BlockSpec((td, d), lambda i: (i, 0))],
        out_specs=pl.BlockSpec(memory_space=pl.ANY),
        out_shape=jax.ShapeDtypeStruct((n_rows, d), u2.dtype),
        scratch_shapes=[pltpu.SemaphoreType.DMA(())],
        compiler_params=_cparams("arbitrary"),
        name="dispatch",
    )(dest, dest, u2)


def _experts_body(be_ref, nv_ref, nu_ref, x_ref, wg_ref, wu_ref, wd_ref, o_ref, wg_s, wu_s, wd_s):
    i = pl.program_id(0)

    @pl.when(jnp.logical_or(i == 0, be_ref[i] != be_ref[jnp.maximum(i - 1, 0)]))
    def _():
        wg_s[...] = wg_ref[...].astype(BF16)
        wu_s[...] = wu_ref[...].astype(BF16)
        wd_s[...] = wd_ref[...].astype(BF16)

    @pl.when(i < nu_ref[0])
    def _():
        row = lax.broadcasted_iota(jnp.int32, (x_ref.shape[0], 1), 0)
        packed = jnp.where(row < nv_ref[i], x_ref[...], jnp.uint32(0))
        x_lo, x_hi = (v.astype(BF16) for v in _unpack_bf16_pair(packed))
        half = packed.shape[1]
        gate = _mm(x_lo, wg_s[0:half, :]) + _mm(x_hi, wg_s[half:, :])
        up = _mm(x_lo, wu_s[0:half, :]) + _mm(x_hi, wu_s[half:, :])
        y = _mm((_silu(gate) * up).astype(BF16), wd_s[...])
        o_ref[...] = _pack_bf16_pair(y[:, :half], y[:, half:])

    @pl.when(i >= nu_ref[0])
    def _():
        o_ref[...] = jnp.zeros_like(o_ref)


def _experts(block_expert, block_valid, n_used, xs, wg, wu, wd):
    nr, dp = xs.shape
    d, ff = wg.shape[1], wg.shape[2]
    assert d == 2 * dp
    rb = MOE_ROWS
    row_map = lambda i, be, nv, nu: (jnp.minimum(i, nu[0] - 1), 0)
    w_map = lambda i, be, nv, nu: (be[i], 0, 0)
    grid_spec = pltpu.PrefetchScalarGridSpec(
        num_scalar_prefetch=3,
        grid=(nr // rb,),
        in_specs=[pl.BlockSpec((rb, dp), row_map),
                  pl.BlockSpec((None, d, ff), w_map),
                  pl.BlockSpec((None, d, ff), w_map),
                  pl.BlockSpec((None, ff, d), w_map)],
        out_specs=pl.BlockSpec((rb, dp), lambda i, be, nv, nu: (i, 0)),
        scratch_shapes=[pltpu.VMEM((d, ff), BF16), pltpu.VMEM((d, ff), BF16), pltpu.VMEM((ff, d), BF16)],
    )
    return pl.pallas_call(
        _experts_body,
        grid_spec=grid_spec,
        out_shape=jax.ShapeDtypeStruct((nr, dp), jnp.uint32),
        compiler_params=_cparams("arbitrary"),
        name="experts",
    )(block_expert, block_valid, n_used, xs, wg, wu, wd)


def _combine_body(d0_ref, d1_ref, info_ref, x1_ref, npost_ref, gt_ref, yb_ref, o_ref, buf, sem, *, tc, rc):
    dests = (d0_ref, d1_ref)

    def row_copy(r, k):
        return pltpu.make_async_copy(yb_ref.at[pl.ds(dests[k][r], 1), :], buf.at[k, pl.ds(r, 1), :], sem)

    def start(r, carry):
        for k in range(MOE_TOPK):
            row_copy(r, k).start()
        return carry

    def wait(r, carry):
        for k in range(MOE_TOPK):
            row_copy(r, k).wait()
        return carry

    lax.fori_loop(0, tc, start, 0, unroll=DMA_LOOP_UNROLL)
    lax.fori_loop(0, tc, wait, 0, unroll=DMA_LOOP_UNROLL)

    half = buf.shape[2]
    gain_lo, gain_hi = npost_ref[:, :half], npost_ref[:, half:]
    gate_lo, gate_hi = gt_ref[:, :half], gt_ref[:, half:]

    def chunk(i, carry):
        r0 = pl.multiple_of(i * rc, rc)
        info = info_ref[pl.ds(r0, rc), :]
        w0, w1 = info[:, 4:5], info[:, 5:6]
        a_lo, a_hi = _unpack_bf16_pair(buf[0, pl.ds(r0, rc), :])
        b_lo, b_hi = _unpack_bf16_pair(buf[1, pl.ds(r0, rc), :])
        y_lo = w0 * a_lo + w1 * b_lo
        y_hi = w0 * a_hi + w1 * b_hi
        ms = (jnp.sum(y_lo * y_lo, axis=-1, keepdims=True)
              + jnp.sum(y_hi * y_hi, axis=-1, keepdims=True)) * (1.0 / (2 * half))
        inv = lax.rsqrt(ms + NORM_EPS)
        o_ref[pl.ds(r0, rc), :half] = x1_ref[pl.ds(r0, rc), :half] + gate_lo * (y_lo * inv * gain_lo)
        o_ref[pl.ds(r0, rc), half:] = x1_ref[pl.ds(r0, rc), half:] + gate_hi * (y_hi * inv * gain_hi)
        return carry

    lax.fori_loop(0, tc // rc, chunk, 0)


def _combine(dest, info, x1, npost, mod3, yb, seq):
    t, d = x1.shape
    tc = min(1024, seq)
    rc = min(128, tc)
    per_b = seq // tc
    nt = t // tc
    return pl.pallas_call(
        functools.partial(_combine_body, tc=tc, rc=rc),
        grid=(nt,),
        in_specs=[pl.BlockSpec((tc,), lambda i: (i,), memory_space=pltpu.SMEM),
                  pl.BlockSpec((tc,), lambda i: (nt + i,), memory_space=pltpu.SMEM),
                  pl.BlockSpec((tc, LANES), lambda i: (i, 0)),
                  pl.BlockSpec((tc, d), lambda i: (i, 0)),
                  pl.BlockSpec((1, d), lambda i: (0, 0)),
                  pl.BlockSpec((None, 1, d), lambda i: (i // per_b, 0, 5)),
                  pl.BlockSpec(memory_space=pl.ANY)],
        out_specs=pl.BlockSpec((tc, d), lambda i: (i, 0)),
        out_shape=jax.ShapeDtypeStruct((t, d), F32),
        scratch_shapes=[pltpu.VMEM((MOE_TOPK, tc, d // 2), jnp.uint32), pltpu.SemaphoreType.DMA(())],
        compiler_params=_cparams("arbitrary"),
        name="combine",
    )(dest, dest, info, x1, npost, mod3, yb)


def _rope_tables(positions):
    half = ROPE_DIM // 2
    inv_freq = jnp.power(ROPE_THETA, -jnp.arange(half, dtype=F32) * (2.0 / ROPE_DIM))
    ang = positions.astype(F32)[..., None] * inv_freq
    cos, sin = jnp.cos(ang), jnp.sin(ang)
    rest = HEAD_DIM - ROPE_DIM
    cos_t = jnp.concatenate([cos, cos, jnp.ones(cos.shape[:-1] + (rest,), F32)], axis=-1)
    sin_t = jnp.concatenate([-sin, sin, jnp.zeros(sin.shape[:-1] + (rest,), F32)], axis=-1)
    return cos_t, sin_t


def _pad_lanes(v, n=LANES):
    return jnp.pad(v, [(0, 0)] * (v.ndim - 1) + [(0, n - v.shape[-1])])


def _mixer_and_router(x, mod3, positions, norm_mix_pre, norm_mix_post, norm_ffn_pre, w_in, conv_w,
                      dn_a_log, dn_dt_bias, dn_out_norm, w_branch_moba, w_branch_delta, w_out,
                      router_group_w, router_group_b, router_expert_w, router_expert_b):
    bsz, seq, d = x.shape
    t = bsz * seq
    mw = MOBA_HEADS * HEAD_DIM
    dw = DN_HEADS * HEAD_DIM
    o_qa, o_ka, o_va = 0, mw, 2 * mw
    o_dn = 3 * mw
    o_z = o_dn + 3 * dw
    o_ba = o_z + dw
    o_ga = o_ba + 2 * DN_HEADS
    o_gb = o_ga + d
    w_perm = jnp.concatenate(
        [w_in[:, o_ga:o_ga + d], w_in[:, o_gb:o_gb + d], w_in[:, o_qa:o_ba]], axis=1).astype(BF16)
    w_small = _pad_lanes(w_in[:, o_ba:o_ga]).astype(BF16)
    x2 = x.reshape(t, d)
    proj2, ba2 = _inproj(x2, norm_mix_pre[None, :], mod3, w_perm, w_small, seq, sc_chunk=1, sh_chunk=0)
    proj = proj2.reshape(bsz, seq, -1)
    c0 = 2 * d // LANES
    nh = MOBA_HEADS

    cos_t, sin_t = _rope_tables(positions)
    ya = _moba(proj, cos_t, sin_t, q_blk0=c0, k_blk0=c0 + nh, v_blk0=c0 + 2 * nh)

    par = jnp.zeros((8, LANES), F32)
    par = par.at[0, DN_HEADS:2 * DN_HEADS].set(dn_a_log.astype(F32))
    par = par.at[1, DN_HEADS:2 * DN_HEADS].set(dn_dt_bias.astype(F32))
    beta, gcum, glast = _gates(ba2.reshape(bsz, seq, LANES), par)
    ngrp = seq // DN_GROUP
    grow = jnp.transpose(gcum[:, :, DN_HEADS:2 * DN_HEADS], (0, 2, 1)).reshape(bsz * DN_HEADS, ngrp, 1, DN_GROUP)
    d0 = c0 + 3 * nh
    yb = _deltanet(proj, conv_w, beta, gcum, glast, grow, dn_out_norm[None, :].astype(F32),
                   q_blk0=d0, k_blk0=d0 + DN_HEADS, v_blk0=d0 + 2 * DN_HEADS, z_blk0=d0 + 3 * DN_HEADS)

    wr = _pad_lanes(jnp.concatenate([router_group_w, router_expert_w], axis=1)).T
    wr_hi = wr.astype(BF16)
    wr = jnp.concatenate([wr_hi, (wr - wr_hi.astype(F32)).astype(BF16)], axis=0)
    br = _pad_lanes(jnp.concatenate([router_group_b, router_expert_b])[None, :]).T
    return _merge(ya.reshape(t, mw), yb.reshape(t, dw), proj2, x2,
                  w_branch_moba.astype(BF16), w_branch_delta.astype(BF16), w_out.astype(BF16), wr, br,
                  norm_mix_post[None, :], norm_ffn_pre[None, :], mod3, seq, ga_blk=0, gb_blk=1)


def _moe(x1, u2, logits, mod3, norm_ffn_post, w_gate, w_up, w_down, seq):
    t, d = x1.shape
    info_t, info, counts = _route(logits)
    counts = counts[MOE_GROUPS:MOE_GROUPS + MOE_EXPERTS, 0].astype(jnp.int32)
    rb = MOE_ROWS
    padded = (counts + rb - 1) // rb * rb
    pad_end = jnp.cumsum(padded)
    pad_start = pad_end - padded
    eid = info_t[0:MOE_TOPK].astype(jnp.int32)
    rank = info_t[MOE_TOPK:2 * MOE_TOPK].astype(jnp.int32)
    dest_flat = (pad_start[eid] + rank).reshape(-1)
    n_blocks = (t * MOE_TOPK + MOE_EXPERTS * (rb - 1)) // rb + 1
    n_used = (pad_end[-1] // rb).astype(jnp.int32)
    blk_row = jnp.minimum(jnp.arange(n_blocks, dtype=jnp.int32), n_used - 1) * rb
    block_expert = jnp.minimum(jnp.sum(pad_end[None, :] <= blk_row[:, None], axis=1),
                               MOE_EXPERTS - 1).astype(jnp.int32)
    block_valid = jnp.clip(counts[block_expert] - (blk_row - pad_start[block_expert]), 0, rb).astype(jnp.int32)
    xs = _dispatch(dest_flat, u2, n_blocks * rb)
    ys = _experts(block_expert, block_valid, n_used[None], xs, w_gate, w_up, w_down)
    return _combine(dest_flat, info, x1, norm_ffn_post[None, :], mod3, ys, seq)


def kernel(x, c, positions, w_ada, b_ada, norm_mix_pre, norm_mix_post, norm_ffn_pre, norm_ffn_post, w_in, conv_w, dn_a_log, dn_dt_bias, dn_out_norm, w_branch_moba, w_branch_delta, w_out, router_group_w, router_group_b, router_expert_w, router_expert_b, expert_w_gate, expert_w_up, expert_w_down):
    bsz, seq, d = x.shape
    depth = w_ada.shape[0]
    for layer in range(depth):
        mod = _ada(c, w_ada[layer], b_ada[layer][None, :])
        mod3 = mod.reshape(bsz, 1, -1)
        x1, u2, logits = _mixer_and_router(
            x, mod3, positions, norm_mix_pre[layer], norm_mix_post[layer], norm_ffn_pre[layer], w_in[layer],
            conv_w[layer], dn_a_log[layer], dn_dt_bias[layer], dn_out_norm[layer], w_branch_moba[layer],
            w_branch_delta[layer], w_out[layer], router_group_w[layer], router_group_b[layer],
            router_expert_w[layer], router_expert_b[layer])
        out = _moe(x1, u2, logits, mod3, norm_ffn_post[layer], expert_w_gate[layer], expert_w_up[layer],
                   expert_w_down[layer], seq)
        x = out.reshape(bsz, seq, d)
    return x
```

```python
import functools
import math

import jax
import jax.numpy as jnp
from jax import lax
from jax.experimental import pallas as pl
from jax.experimental.pallas import tpu as pltpu

F32 = jnp.float32
BF16 = jnp.bfloat16
HI = lax.Precision.HIGHEST

NORM_EPS = 1e-6
HEAD_DIM = 128
MOBA_HEADS = 8
MOBA_BLOCK = 256
MOBA_TOPK = 3
ROPE_THETA = 500000.0
ROPE_DIM = HEAD_DIM // 4
DN_HEADS = 8
DN_CONV_WIDTH = 4
DN_CHUNK = 64
DN_GROUP = 256
DN_HEADS_PER_STEP = 4
DN_CONV_PAD = 8
DN_CONV_ROWS = 128
MOE_GROUPS = 4
MOE_EXPERTS_PER_GROUP = 8
MOE_EXPERTS = MOE_GROUPS * MOE_EXPERTS_PER_GROUP
MOE_TOPK = 2
MOE_ROWS = 256
DMA_LOOP_UNROLL = 8
MERGE_SUB_ROWS = 128
LANES = 128
NEG = -1e30

VMEM_LIMIT = 56 * 1024 * 1024


def _cparams(*sem):
    return pltpu.CompilerParams(dimension_semantics=sem, vmem_limit_bytes=VMEM_LIMIT)


def _mm(a, b, precision=None):
    return jnp.dot(a, b, precision=precision, preferred_element_type=F32)


def _nt(a, b, precision=None):
    return lax.dot_general(a, b, (((1,), (1,)), ((), ())), precision=precision,
                           preferred_element_type=F32)


def _eye(n, dtype):
    r = lax.broadcasted_iota(jnp.int32, (n, n), 0)
    c = lax.broadcasted_iota(jnp.int32, (n, n), 1)
    return jnp.where(r == c, 1.0, 0.0).astype(dtype)


def _silu(x):
    return x * jax.nn.sigmoid(x)


def _softplus(x):
    return jnp.maximum(x, 0.0) + jnp.log1p(jnp.exp(-jnp.abs(x)))


def _pack_bf16_pair(lo, hi):
    def rne(x):
        b = pltpu.bitcast(x, jnp.uint32)
        return b + jnp.uint32(0x7FFF) + ((b >> 16) & jnp.uint32(1))
    return (rne(hi) & jnp.uint32(0xFFFF0000)) | (rne(lo) >> 16)


def _unpack_bf16_pair(p):
    return pltpu.bitcast(p << 16, F32), pltpu.bitcast(p & jnp.uint32(0xFFFF0000), F32)


def _ada_body(c_ref, w_ref, b_ref, o_ref):
    o_ref[...] = _mm(_silu(c_ref[...]), w_ref[...], HI) + b_ref[...]


def _ada(c, w, b):
    bsz, d = c.shape
    n = w.shape[1]
    tn = 1024
    return pl.pallas_call(
        _ada_body,
        grid=(n // tn,),
        in_specs=[pl.BlockSpec((bsz, d), lambda j: (0, 0)),
                  pl.BlockSpec((d, tn), lambda j: (0, j)),
                  pl.BlockSpec((1, tn), lambda j: (0, j))],
        out_specs=pl.BlockSpec((bsz, tn), lambda j: (0, j)),
        out_shape=jax.ShapeDtypeStruct((bsz, n), F32),
        compiler_params=_cparams("parallel"),
        name="ada",
    )(c, w, b)


def _inproj_body(x_ref, g_ref, sc_ref, sh_ref, w_ref, ws_ref, o_ref, os_ref, u_ref, *, tm, rc):
    @pl.when(pl.program_id(1) == 0)
    def _():
        gain = g_ref[...]
        scale = 1.0 + sc_ref[...]
        shift = sh_ref[...]

        def chunk(i, carry):
            r0 = pl.multiple_of(i * rc, rc)
            x = x_ref[pl.ds(r0, rc), :]
            y = x * lax.rsqrt(jnp.mean(x * x, axis=-1, keepdims=True) + NORM_EPS) * gain
            u_ref[pl.ds(r0, rc), :] = (y * scale + shift).astype(BF16)
            return carry

        lax.fori_loop(0, tm // rc, chunk, 0)
        os_ref[...] = _mm(u_ref[...], ws_ref[...])

    o_ref[...] = _mm(u_ref[...], w_ref[...]).astype(o_ref.dtype)


def _inproj(x2, gain, mod3, w, w_small, seq, *, sc_chunk, sh_chunk):
    t, d = x2.shape
    n = w.shape[1]
    tm = min(1024, seq)
    tn = 1024
    per_b = seq // tm
    rc = min(256, tm)
    return pl.pallas_call(
        functools.partial(_inproj_body, tm=tm, rc=rc),
        grid=(t // tm, n // tn),
        in_specs=[pl.BlockSpec((tm, d), lambda i, j: (i, 0)),
                  pl.BlockSpec((1, d), lambda i, j: (0, 0)),
                  pl.BlockSpec((None, 1, d), lambda i, j: (i // per_b, 0, sc_chunk)),
                  pl.BlockSpec((None, 1, d), lambda i, j: (i // per_b, 0, sh_chunk)),
                  pl.BlockSpec((d, tn), lambda i, j: (0, j)),
                  pl.BlockSpec((d, LANES), lambda i, j: (0, 0))],
        out_specs=[pl.BlockSpec((tm, tn), lambda i, j: (i, j)),
                   pl.BlockSpec((tm, LANES), lambda i, j: (i, 0))],
        out_shape=[jax.ShapeDtypeStruct((t, n), BF16),
                   jax.ShapeDtypeStruct((t, LANES), F32)],
        scratch_shapes=[pltpu.VMEM((tm, d), BF16)],
        compiler_params=_cparams("parallel", "arbitrary"),
        name="inproj",
    )(x2, gain, mod3, mod3, w, w_small)


def _moba_body(q_ref, k_ref, v_ref, cos_ref, sin_ref, o_ref, kr_s, vt_s, km_s, *, nblk):
    blk = MOBA_BLOCK
    half = ROPE_DIM // 2
    lane = lax.broadcasted_iota(jnp.int32, (blk, HEAD_DIM), 1)
    eye_d = _eye(HEAD_DIM, BF16)
    eye_b = _eye(blk, BF16)
    scale = HEAD_DIM ** -0.5

    def rope(xf, rows):
        partner = jnp.where(lane < half, pltpu.roll(xf, HEAD_DIM - half, 1), pltpu.roll(xf, half, 1))
        return xf * cos_ref[rows, :] + partner * sin_ref[rows, :]

    km_s[...] = jnp.zeros_like(km_s)
    for j in range(nblk):
        rows = slice(j * blk, (j + 1) * blk)
        kc = rope(k_ref[rows, :].astype(F32), rows)
        kr_s[rows, :] = kc.astype(BF16)
        km_s[j:j + 1, :] = jnp.mean(kc, axis=0, keepdims=True)
        vt_s[:, rows] = _nt(eye_d, v_ref[rows, :]).astype(BF16)

    key_i = lax.broadcasted_iota(jnp.int32, (blk, blk), 0)
    qry_i = lax.broadcasted_iota(jnp.int32, (blk, blk), 1)
    blk_i = lax.broadcasted_iota(jnp.int32, (8, blk), 0)
    causal_bias = jnp.where(key_i <= qry_i, 0.0, NEG)

    def rows(j):
        return slice(j * blk, (j + 1) * blk)

    def begin(i):
        qc = rope(q_ref[rows(i), :].astype(F32), rows(i))
        st = dict(i=i, qs=(qc * scale).astype(BF16), scores=[], m=None, bias=None)
        if i > 0:
            g_t = _nt(km_s[...], qc, HI)
            rank = jnp.zeros((8, blk), F32)
            for jp in range(i):
                row = g_t[jp:jp + 1, :]
                beats = (row > g_t) | ((row == g_t) & (jp < blk_i))
                rank = rank + jnp.where(beats, 1.0, 0.0)
            sel = (rank < float(MOBA_TOPK)) & (blk_i < i)
            st["bias"] = jnp.where(sel, 0.0, NEG)
        return st

    def score(st, j):
        i = st["i"]
        s = _nt(kr_s[rows(j), :], st["qs"])
        if j == i:
            s = s + causal_bias
        st["scores"].append(s)
        mj = jnp.max(s, axis=0, keepdims=True)
        if j < i:
            mj = mj + st["bias"][j:j + 1, :]
        st["m"] = mj if st["m"] is None else jnp.maximum(st["m"], mj)

    def accumulate(st, j):
        i = st["i"]
        if j == 0:
            st["den"] = jnp.zeros((1, blk), F32)
            st["acc"] = jnp.zeros((HEAD_DIM, blk), F32)
        shift = (st["bias"][j:j + 1, :] - st["m"]) if j < i else -st["m"]
        p = jnp.exp(st["scores"][j] + shift)
        st["den"] = st["den"] + jnp.sum(p, axis=0, keepdims=True)
        st["acc"] = st["acc"] + _mm(vt_s[:, rows(j)], p.astype(BF16))
        if j == i:
            o_t = (st["acc"] / st["den"]).astype(BF16)
            o_ref[rows(i), :] = _nt(eye_b, o_t).astype(o_ref.dtype)

    prev = None
    for i in range(nblk):
        cur = begin(i)
        for j in range(i + 1):
            score(cur, j)
            if prev is not None and j <= prev["i"]:
                accumulate(prev, j)
        prev = cur
    for j in range(nblk):
        accumulate(prev, j)


def _moba(proj, cos_t, sin_t, *, q_blk0, k_blk0, v_blk0):
    bsz, seq, _ = proj.shape
    nblk = seq // MOBA_BLOCK
    assert seq % MOBA_BLOCK == 0 and 1 <= nblk <= 8
    hd = HEAD_DIM
    return pl.pallas_call(
        functools.partial(_moba_body, nblk=nblk),
        grid=(bsz, MOBA_HEADS),
        in_specs=[pl.BlockSpec((None, seq, hd), lambda b, h: (b, 0, q_blk0 + h)),
                  pl.BlockSpec((None, seq, hd), lambda b, h: (b, 0, k_blk0 + h)),
                  pl.BlockSpec((None, seq, hd), lambda b, h: (b, 0, v_blk0 + h)),
                  pl.BlockSpec((None, seq, hd), lambda b, h: (b, 0, 0)),
                  pl.BlockSpec((None, seq, hd), lambda b, h: (b, 0, 0))],
        out_specs=pl.BlockSpec((None, seq, hd), lambda b, h: (b, 0, h)),
        out_shape=jax.ShapeDtypeStruct((bsz, seq, MOBA_HEADS * hd), BF16),
        scratch_shapes=[pltpu.VMEM((seq, hd), BF16),
                        pltpu.VMEM((hd, seq), BF16),
                        pltpu.VMEM((8, hd), F32)],
        compiler_params=_cparams("parallel", "parallel"),
        name="moba",
    )(proj, proj, proj, cos_t, sin_t)


def _chunk_masks(n, chunk):
    r = lax.broadcasted_iota(jnp.int32, (n, n), 0)
    c = lax.broadcasted_iota(jnp.int32, (n, n), 1)
    shift = int(math.log2(chunk))
    same = jnp.right_shift(r, shift) == jnp.right_shift(c, shift)
    return r, c, same


def _gates_body(ba_ref, par_ref, beta_ref, g_ref, gl_ref, *, seq):
    grp = DN_GROUP
    r, c, same = _chunk_masks(grp, DN_CHUNK)
    low = jnp.where(same & (c <= r), 1.0, 0.0)
    ones = jnp.where(same, 1.0, 0.0)
    neg_a = -jnp.exp(par_ref[0:1, :])
    dt_b = par_ref[1:2, :]
    for i in range(seq // grp):
        rows = slice(i * grp, (i + 1) * grp)
        x = ba_ref[rows, :].astype(F32)
        beta_ref[rows, :] = jax.nn.sigmoid(x)
        g = neg_a * _softplus(x + dt_b)
        g_ref[rows, :] = _mm(low, g, HI)
        gl_ref[rows, :] = _mm(ones, g, HI)


def _gates(ba, par):
    bsz, seq, _ = ba.shape
    spec = pl.BlockSpec((None, seq, LANES), lambda b: (b, 0, 0))
    return pl.pallas_call(
        functools.partial(_gates_body, seq=seq),
        grid=(bsz,),
        in_specs=[spec, pl.BlockSpec((8, LANES), lambda b: (0, 0))],
        out_specs=[spec, spec, spec],
        out_shape=[jax.ShapeDtypeStruct((bsz, seq, LANES), F32)] * 3,
        compiler_params=_cparams("parallel"),
        name="dn_gates",
    )(ba, par)


def _dn_body(q_ref, k_ref, v_ref, z_ref, cwq_ref, cwk_ref, cwv_ref, beta_ref, g_ref, gl_ref, grow_ref,
             gain_ref, o_ref, xpad_s, qn_s, kn_s, vn_s, state_s, vnew_s, *, seq, hp):
    hg = pl.program_id(1)
    grp = DN_GROUP
    chunk = DN_CHUNK
    ngrp = seq // grp
    hd = HEAD_DIM
    pad = DN_CONV_PAD

    crows = DN_CONV_ROWS
    nchunk = seq // crows
    for src, cw_ref, dst, l2, post in ((q_ref, cwq_ref, qn_s, True, hd ** -0.5),
                                       (k_ref, cwk_ref, kn_s, True, None),
                                       (v_ref, cwv_ref, vn_s, False, None)):
        def copy(c, carry, src=src):
            r0 = pl.multiple_of(c * crows, crows)
            xpad_s[c, pad:pad + crows, :] = src[pl.ds(r0, crows), :].astype(F32)
            return carry

        def history(c, carry):
            xpad_s[c, 0:pad, :] = xpad_s[c - 1, crows:crows + pad, :]
            return carry

        lax.fori_loop(0, nchunk, copy, 0)
        xpad_s[0, 0:pad, :] = jnp.zeros((pad, hp * hd), F32)
        lax.fori_loop(1, nchunk, history, 0)
        cw = cw_ref[...]

        def conv(c, carry, cw=cw, dst=dst, l2=l2, post=post):
            r0 = pl.multiple_of(c * crows, crows)
            for hh in range(hp):
                cols = slice(hh * hd, (hh + 1) * hd)
                acc = None
                for j in range(DN_CONV_WIDTH):
                    off = pad - (DN_CONV_WIDTH - 1) + j
                    term = xpad_s[c, off:off + crows, cols] * cw[j:j + 1, cols]
                    acc = term if acc is None else acc + term
                yh = _silu(acc)
                if l2:
                    yh = yh * lax.rsqrt(jnp.sum(yh * yh, axis=-1, keepdims=True) + NORM_EPS)
                if post is not None:
                    yh = yh * post
                dst[pl.ds(r0, crows), cols] = yh
            return carry

        lax.fori_loop(0, nchunk, conv, 0)

    r, c, same = _chunk_masks(grp, chunk)
    low_incl = same & (c <= r)
    low_strict = same & (c < r)
    eye_g = jnp.where(r == c, 1.0, 0.0)
    eye_d = _eye(hd, BF16)
    lane = lax.broadcasted_iota(jnp.int32, (grp, LANES), 1)
    col_chunk = jnp.right_shift(lax.broadcasted_iota(jnp.int32, (hd, grp), 1), int(math.log2(chunk)))
    gain = gain_ref[...]
    n_double = int(math.log2(chunk)) - 1

    def pick(ref, r0, lane_id):
        return jnp.sum(jnp.where(lane == lane_id, ref[pl.ds(r0, grp), :], 0.0), axis=-1, keepdims=True)

    state_s[...] = jnp.zeros_like(state_s)
    heads = range(hp)

    def group(gi, carry):
        r0 = pl.multiple_of(gi * grp, grp)
        cols = [slice(hh * hd, (hh + 1) * hd) for hh in heads]
        head = [hg * hp + hh for hh in heads]
        q = [qn_s[pl.ds(r0, grp), cols[hh]] for hh in heads]
        k = [kn_s[pl.ds(r0, grp), cols[hh]] for hh in heads]
        v = [vn_s[pl.ds(r0, grp), cols[hh]] for hh in heads]
        beta = [pick(beta_ref, r0, head[hh]) for hh in heads]
        g_col = [pick(g_ref, r0, DN_HEADS + head[hh]) for hh in heads]
        gl_col = [pick(gl_ref, r0, DN_HEADS + head[hh]) for hh in heads]
        decay = [jnp.exp(jnp.minimum(g_col[hh] - grow_ref[hh, gi], 0.0)) for hh in heads]
        e_g = [jnp.exp(g_col[hh]) for hh in heads]
        kb = [k[hh] * beta[hh] for hh in heads]
        vb = [(v[hh] * beta[hh]).astype(BF16) for hh in heads]
        k16 = [k[hh].astype(BF16) for hh in heads]
        n_mat = [jnp.where(low_strict, _nt(kb[hh].astype(BF16), k16[hh]) * decay[hh], 0.0) for hh in heads]
        p = [(-n_mat[hh]).astype(BF16) for hh in heads]
        x = [eye_g - n_mat[hh] for hh in heads]
        for _ in range(n_double):
            p2 = [_mm(p[hh], p[hh]).astype(BF16) for hh in heads]
            x = [x[hh] + _mm(x[hh].astype(BF16), p2[hh]) for hh in heads]
            p = p2
        x16 = [x[hh].astype(BF16) for hh in heads]
        u = [_mm(x16[hh], vb[hh]) for hh in heads]
        w = [_mm(x16[hh], (kb[hh] * e_g[hh]).astype(BF16)).astype(BF16) for hh in heads]
        qk = [jnp.where(low_incl, _nt(q[hh].astype(BF16), k16[hh]) * decay[hh], 0.0).astype(BF16) for hh in heads]
        q_dec = [(q[hh] * e_g[hh]).astype(BF16) for hh in heads]
        k_tail_t = [_nt(eye_d, (k[hh] * jnp.exp(gl_col[hh] - g_col[hh])).astype(BF16)).astype(BF16)
                    for hh in heads]
        vnew_s[...] = jnp.zeros_like(vnew_s)
        for ci in range(grp // chunk):
            rows = slice(ci * chunk, (ci + 1) * chunk)
            gl_row = jnp.exp(gl_ref[pl.ds(r0 + ci * chunk, 1), :])
            for hh in heads:
                state = state_s[hh]
                s16 = state.astype(BF16)
                v_new = u[hh][rows, :] - _mm(w[hh][rows, :], s16)
                vnew_s[hh, rows, :] = v_new.astype(BF16)
                o = _mm(q_dec[hh][rows, :], s16) + _mm(qk[hh][rows, :], vnew_s[hh])
                chunk_decay = jnp.sum(jnp.where(lane[0:1, :] == DN_HEADS + head[hh], gl_row, 0.0),
                                      axis=-1, keepdims=True)
                kt = jnp.where(col_chunk == ci, k_tail_t[hh], jnp.zeros_like(k_tail_t[hh]))
                state_s[hh] = state * chunk_decay + _mm(kt, vnew_s[hh])
                on = o * lax.rsqrt(jnp.mean(o * o, axis=-1, keepdims=True) + NORM_EPS) * gain
                zz = z_ref[pl.ds(r0 + ci * chunk, chunk), cols[hh]].astype(F32)
                o_ref[pl.ds(r0 + ci * chunk, chunk), cols[hh]] = (on * _silu(zz)).astype(o_ref.dtype)
        return carry

    lax.fori_loop(0, ngrp, group, 0)


def _deltanet(proj, conv_w, beta, gcum, glast, grow, gain, *, q_blk0, k_blk0, v_blk0, z_blk0):
    bsz, seq, _ = proj.shape
    hd = HEAD_DIM
    nh = DN_HEADS
    assert seq % DN_GROUP == 0
    ngrp = seq // DN_GROUP

    hp = DN_HEADS_PER_STEP
    wide = hp * hd
    assert nh % hp == 0 and all(b0 % hp == 0 for b0 in (q_blk0, k_blk0, v_blk0, z_blk0))

    def col(blk0):
        return pl.BlockSpec((None, seq, wide), lambda b, h: (b, 0, blk0 // hp + h))

    def cw(blk0):
        return pl.BlockSpec((DN_CONV_WIDTH, wide), lambda b, h: (0, blk0 // hp + h))

    full = pl.BlockSpec((None, seq, LANES), lambda b, h: (b, 0, 0))
    return pl.pallas_call(
        functools.partial(_dn_body, seq=seq, hp=hp),
        grid=(bsz, nh // hp),
        in_specs=[col(q_blk0), col(k_blk0), col(v_blk0), col(z_blk0),
                  cw(0), cw(nh), cw(2 * nh),
                  full, full, full,
                  pl.BlockSpec((hp, ngrp, 1, DN_GROUP), lambda b, h: (b * (nh // hp) + h, 0, 0, 0)),
                  pl.BlockSpec((1, hd), lambda b, h: (0, 0))],
        out_specs=pl.BlockSpec((None, seq, wide), lambda b, h: (b, 0, h)),
        out_shape=jax.ShapeDtypeStruct((bsz, seq, nh * hd), BF16),
        scratch_shapes=[pltpu.VMEM((seq // DN_CONV_ROWS, DN_CONV_ROWS + DN_CONV_PAD, wide), F32)]
        + [pltpu.VMEM((seq, wide), F32)] * 3
        + [pltpu.VMEM((hp, hd, hd), F32), pltpu.VMEM((hp, DN_GROUP, hd), BF16)],
        compiler_params=_cparams("parallel", "parallel"),
        name="deltanet",
    )(proj, proj, proj, proj, conv_w, conv_w, conv_w, beta, gcum, glast, grow, gain)


def _merge_body(ya_ref, yb_ref, ga_ref, gb_ref, x_ref, wm_ref, wd_ref, wo_ref, wr_ref, br_ref,
                npost_ref, npre_ref, gt_ref, sc_ref, sh_ref, x1_ref, u2_ref, lg_ref):
    sub = MERGE_SUB_ROWS
    for s in range(x_ref.shape[0] // sub):
        rows = slice(s * sub, (s + 1) * sub)
        ma = _mm(ya_ref[rows, :], wm_ref[...])
        mb = _mm(yb_ref[rows, :], wd_ref[...])
        merged = (jax.nn.sigmoid(ga_ref[rows, :].astype(F32)) * ma
                  + jax.nn.sigmoid(gb_ref[rows, :].astype(F32)) * mb)
        y = _mm(merged.astype(BF16), wo_ref[...])
        yn = y * lax.rsqrt(jnp.mean(y * y, axis=-1, keepdims=True) + NORM_EPS) * npost_ref[...]
        x1 = x_ref[rows, :] + gt_ref[...] * yn
        x1_ref[rows, :] = x1
        un = x1 * lax.rsqrt(jnp.mean(x1 * x1, axis=-1, keepdims=True) + NORM_EPS) * npre_ref[...]
        u2 = un * (1.0 + sc_ref[...]) + sh_ref[...]
        half = u2.shape[1] // 2
        u2_ref[rows, :] = _pack_bf16_pair(u2[:, :half], u2[:, half:])
        u_hi = u2.astype(BF16)
        u_lo = (u2 - u_hi.astype(F32)).astype(BF16)
        lg2 = _nt(wr_ref[...], u_hi)
        lg_ref[:, rows] = lg2[:LANES, :] + lg2[LANES:, :] + _nt(wr_ref[:LANES, :], u_lo) + br_ref[...]


def _merge(ya, yb, proj2, x2, wm, wd, wo, wr, br, npost, npre, mod3, seq, *, ga_blk, gb_blk):
    t, d = x2.shape
    wa = ya.shape[1]
    tm = min(256, seq)
    per_b = seq // tm
    const = lambda i: (0, 0)
    once = dict(pipeline_mode=pl.Buffered(1))

    def modspec(chunk):
        return pl.BlockSpec((None, 1, d), lambda i: (i // per_b, 0, chunk))

    return pl.pallas_call(
        _merge_body,
        grid=(t // tm,),
        in_specs=[pl.BlockSpec((tm, wa), lambda i: (i, 0)),
                  pl.BlockSpec((tm, wa), lambda i: (i, 0)),
                  pl.BlockSpec((tm, d), lambda i: (i, ga_blk)),
                  pl.BlockSpec((tm, d), lambda i: (i, gb_blk)),
                  pl.BlockSpec((tm, d), lambda i: (i, 0)),
                  pl.BlockSpec((wa, d), const, **once),
                  pl.BlockSpec((wa, d), const, **once),
                  pl.BlockSpec((d, d), const, **once),
                  pl.BlockSpec((2 * LANES, d), const, **once),
                  pl.BlockSpec((LANES, 1), const),
                  pl.BlockSpec((1, d), const),
                  pl.BlockSpec((1, d), const),
                  modspec(2), modspec(4), modspec(3)],
        out_specs=[pl.BlockSpec((tm, d), lambda i: (i, 0)),
                   pl.BlockSpec((tm, d // 2), lambda i: (i, 0)),
                   pl.BlockSpec((LANES, tm), lambda i: (0, i))],
        out_shape=[jax.ShapeDtypeStruct((t, d), F32),
                   jax.ShapeDtypeStruct((t, d // 2), jnp.uint32),
                   jax.ShapeDtypeStruct((LANES, t), F32)],
        compiler_params=_cparams("parallel"),
        name="merge",
    )(ya, yb, proj2, proj2, x2, wm, wd, wo, wr, br, npost, npre, mod3, mod3, mod3)


def _route_body(lg_ref, info_ref, col_ref, cnt_ref, run_s, *, tr):
    @pl.when(pl.program_id(0) == 0)
    def _():
        run_s[...] = jnp.zeros_like(run_s)

    lg = lg_ref[...]
    row = lax.broadcasted_iota(jnp.int32, (LANES, tr), 0)
    row_f = row.astype(F32)
    big = float(LANES)

    def first_max(vals, mask):
        mx = jnp.max(jnp.where(mask, vals, NEG), axis=0, keepdims=True)
        idx = jnp.min(jnp.where(mask & (vals == mx), row_f, big), axis=0, keepdims=True)
        return mx, idx

    gmask = row < MOE_GROUPS
    gmax, gidx = first_max(lg, gmask)
    p_group = 1.0 / jnp.sum(jnp.where(gmask, jnp.exp(lg - gmax), 0.0), axis=0, keepdims=True)
    lo = float(MOE_GROUPS) + gidx * float(MOE_EXPERTS_PER_GROUP)
    emask = (row_f >= lo) & (row_f < lo + float(MOE_EXPERTS_PER_GROUP))
    m1, i1 = first_max(lg, emask)
    m2, i2 = first_max(lg, emask & (row_f != i1))
    e2 = jnp.exp(m2 - m1)
    w1 = p_group / (1.0 + e2)
    w2 = p_group * e2 / (1.0 + e2)
    oh1 = row_f == i1
    oh2 = row_f == i2
    oh = jnp.where(oh1 | oh2, 1.0, 0.0).astype(BF16)
    r = lax.broadcasted_iota(jnp.int32, (tr, tr), 0)
    c = lax.broadcasted_iota(jnp.int32, (tr, tr), 1)
    before = jnp.where(r < c, 1.0, 0.0).astype(BF16)
    prefix = _mm(oh, before) + run_s[:, 0:1]
    rank1 = jnp.sum(jnp.where(oh1, prefix, 0.0), axis=0, keepdims=True)
    rank2 = jnp.sum(jnp.where(oh2, prefix, 0.0), axis=0, keepdims=True)
    run_s[...] = run_s[...] + jnp.sum(oh.astype(F32), axis=1, keepdims=True)
    goff = float(MOE_GROUPS)
    info = jnp.where(row == 0, i1 - goff, 0.0)
    info = jnp.where(row == 1, i2 - goff, info)
    info = jnp.where(row == 2, rank1, info)
    info = jnp.where(row == 3, rank2, info)
    info = jnp.where(row == 4, w1, info)
    info = jnp.where(row == 5, w2, info)
    info_ref[...] = info[0:8, :]
    col_ref[...] = _nt(_eye(tr, F32), info, HI)
    cnt_ref[...] = run_s[...]


def _route(logits_t):
    t = logits_t.shape[1]
    tr = min(256, t)
    return pl.pallas_call(
        functools.partial(_route_body, tr=tr),
        grid=(t // tr,),
        in_specs=[pl.BlockSpec((LANES, tr), lambda i: (0, i))],
        out_specs=[pl.BlockSpec((8, tr), lambda i: (0, i)),
                   pl.BlockSpec((tr, LANES), lambda i: (i, 0)),
                   pl.BlockSpec((LANES, LANES), lambda i: (0, 0))],
        out_shape=[jax.ShapeDtypeStruct((8, t), F32),
                   jax.ShapeDtypeStruct((t, LANES), F32),
                   jax.ShapeDtypeStruct((LANES, LANES), F32)],
        scratch_shapes=[pltpu.VMEM((LANES, LANES), F32)],
        compiler_params=_cparams("arbitrary"),
        name="route",
    )(logits_t)


def _dispatch_body(d0_ref, d1_ref, u_ref, xs_ref, sem, *, td):
    dests = (d0_ref, d1_ref)

    def row_copy(r, k):
        return pltpu.make_async_copy(u_ref.at[pl.ds(r, 1), :], xs_ref.at[pl.ds(dests[k][r], 1), :], sem)

    def start(r, carry):
        for k in range(MOE_TOPK):
            row_copy(r, k).start()
        return carry

    def wait(r, carry):
        for k in range(MOE_TOPK):
            row_copy(r, k).wait()
        return carry

    lax.fori_loop(0, td, start, 0, unroll=DMA_LOOP_UNROLL)
    lax.fori_loop(0, td, wait, 0, unroll=DMA_LOOP_UNROLL)


def _dispatch(dest, u2, n_rows):
    t, d = u2.shape
    td = min(1024, t)
    nt = t // td
    return pl.pallas_call(
        functools.partial(_dispatch_body, td=td),
        grid=(nt,),
        in_specs=[pl.BlockSpec((td,), lambda i: (i,), memory_space=pltpu.SMEM),
                  pl.BlockSpec((td,), lambda i: (nt + i,), memory_space=pltpu.SMEM),
                  pl.BlockSpec((td, d), lambda i: (i, 0))],
        out_specs=pl.BlockSpec(memory_space=pl.ANY),
        out_shape=jax.ShapeDtypeStruct((n_rows, d), u2.dtype),
        scratch_shapes=[pltpu.SemaphoreType.DMA(())],
        compiler_params=_cparams("arbitrary"),
        name="dispatch",
    )(dest, dest, u2)


def _experts_body(be_ref, nx_ref, nv_ref, nu_ref, x_ref, wg_ref, wu_ref, wd_ref, o_ref,
                  stage_g, stage_u, stage_d, wg_s, wu_s, wd_s, sems):
    i = pl.program_id(0)

    def fetch(e):
        return (pltpu.make_async_copy(wg_ref.at[e], stage_g, sems.at[0]),
                pltpu.make_async_copy(wu_ref.at[e], stage_u, sems.at[1]),
                pltpu.make_async_copy(wd_ref.at[e], stage_d, sems.at[2]))

    @pl.when(i == 0)
    def _():
        for cp in fetch(be_ref[0]):
            cp.start()

    @pl.when(jnp.logical_or(i == 0, be_ref[i] != be_ref[jnp.maximum(i - 1, 0)]))
    def _():
        for cp in fetch(be_ref[i]):
            cp.wait()
        wg_s[...] = stage_g[...].astype(BF16)
        wu_s[...] = stage_u[...].astype(BF16)
        wd_s[...] = stage_d[...].astype(BF16)

        @pl.when(nx_ref[i] >= 0)
        def _():
            for cp in fetch(nx_ref[i]):
                cp.start()

    @pl.when(i < nu_ref[0])
    def _():
        row = lax.broadcasted_iota(jnp.int32, (x_ref.shape[0], 1), 0)
        packed = jnp.where(row < nv_ref[i], x_ref[...], jnp.uint32(0))
        x_lo, x_hi = (v.astype(BF16) for v in _unpack_bf16_pair(packed))
        half = packed.shape[1]
        gate = _mm(x_lo, wg_s[0:half, :]) + _mm(x_hi, wg_s[half:, :])
        up = _mm(x_lo, wu_s[0:half, :]) + _mm(x_hi, wu_s[half:, :])
        y = _mm((_silu(gate) * up).astype(BF16), wd_s[...])
        o_ref[...] = _pack_bf16_pair(y[:, :half], y[:, half:])

    @pl.when(i >= nu_ref[0])
    def _():
        o_ref[...] = jnp.zeros_like(o_ref)


def _experts(block_expert, next_expert, block_valid, n_used, xs, wg, wu, wd):
    nr, dp = xs.shape
    d, ff = wg.shape[1], wg.shape[2]
    assert d == 2 * dp
    rb = MOE_ROWS
    row_map = lambda i, be, nx, nv, nu: (jnp.minimum(i, nu[0] - 1), 0)
    hbm = pl.BlockSpec(memory_space=pl.ANY)
    grid_spec = pltpu.PrefetchScalarGridSpec(
        num_scalar_prefetch=4,
        grid=(nr // rb,),
        in_specs=[pl.BlockSpec((rb, dp), row_map), hbm, hbm, hbm],
        out_specs=pl.BlockSpec((rb, dp), lambda i, be, nx, nv, nu: (i, 0)),
        scratch_shapes=[pltpu.VMEM((d, ff), F32), pltpu.VMEM((d, ff), F32), pltpu.VMEM((ff, d), F32),
                        pltpu.VMEM((d, ff), BF16), pltpu.VMEM((d, ff), BF16), pltpu.VMEM((ff, d), BF16),
                        pltpu.SemaphoreType.DMA((3,))],
    )
    return pl.pallas_call(
        _experts_body,
        grid_spec=grid_spec,
        out_shape=jax.ShapeDtypeStruct((nr, dp), jnp.uint32),
        compiler_params=_cparams("arbitrary"),
        name="experts",
    )(block_expert, next_expert, block_valid, n_used, xs, wg, wu, wd)


def _combine_body(d0_ref, d1_ref, info_ref, x1_ref, npost_ref, gt_ref, yb_ref, o_ref, buf, sem, *, tc, rc):
    dests = (d0_ref, d1_ref)

    def row_copy(r, k):
        return pltpu.make_async_copy(yb_ref.at[pl.ds(dests[k][r], 1), :], buf.at[k, pl.ds(r, 1), :], sem)

    def start(r, carry):
        for k in range(MOE_TOPK):
            row_copy(r, k).start()
        return carry

    def wait(r, carry):
        for k in range(MOE_TOPK):
            row_copy(r, k).wait()
        return carry

    lax.fori_loop(0, tc, start, 0, unroll=DMA_LOOP_UNROLL)
    lax.fori_loop(0, tc, wait, 0, unroll=DMA_LOOP_UNROLL)

    half = buf.shape[2]
    gain_lo, gain_hi = npost_ref[:, :half], npost_ref[:, half:]
    gate_lo, gate_hi = gt_ref[:, :half], gt_ref[:, half:]

    def chunk(i, carry):
        r0 = pl.multiple_of(i * rc, rc)
        info = info_ref[pl.ds(r0, rc), :]
        w0, w1 = info[:, 4:5], info[:, 5:6]
        a_lo, a_hi = _unpack_bf16_pair(buf[0, pl.ds(r0, rc), :])
        b_lo, b_hi = _unpack_bf16_pair(buf[1, pl.ds(r0, rc), :])
        y_lo = w0 * a_lo + w1 * b_lo
        y_hi = w0 * a_hi + w1 * b_hi
        ms = (jnp.sum(y_lo * y_lo, axis=-1, keepdims=True)
              + jnp.sum(y_hi * y_hi, axis=-1, keepdims=True)) * (1.0 / (2 * half))
        inv = lax.rsqrt(ms + NORM_EPS)
        o_ref[pl.ds(r0, rc), :half] = x1_ref[pl.ds(r0, rc), :half] + gate_lo * (y_lo * inv * gain_lo)
        o_ref[pl.ds(r0, rc), half:] = x1_ref[pl.ds(r0, rc), half:] + gate_hi * (y_hi * inv * gain_hi)
        return carry

    lax.fori_loop(0, tc // rc, chunk, 0)


def _combine(dest, info, x1, npost, mod3, yb, seq):
    t, d = x1.shape
    tc = min(1024, seq)
    rc = min(128, tc)
    per_b = seq // tc
    nt = t // tc
    return pl.pallas_call(
        functools.partial(_combine_body, tc=tc, rc=rc),
        grid=(nt,),
        in_specs=[pl.BlockSpec((tc,), lambda i: (i,), memory_space=pltpu.SMEM),
                  pl.BlockSpec((tc,), lambda i: (nt + i,), memory_space=pltpu.SMEM),
                  pl.BlockSpec((tc, LANES), lambda i: (i, 0)),
                  pl.BlockSpec((tc, d), lambda i: (i, 0)),
                  pl.BlockSpec((1, d), lambda i: (0, 0)),
                  pl.BlockSpec((None, 1, d), lambda i: (i // per_b, 0, 5)),
                  pl.BlockSpec(memory_space=pl.ANY)],
        out_specs=pl.BlockSpec((tc, d), lambda i: (i, 0)),
        out_shape=jax.ShapeDtypeStruct((t, d), F32),
        scratch_shapes=[pltpu.VMEM((MOE_TOPK, tc, d // 2), jnp.uint32), pltpu.SemaphoreType.DMA(())],
        compiler_params=_cparams("arbitrary"),
        name="combine",
    )(dest, dest, info, x1, npost, mod3, yb)


def _rope_tables(positions):
    half = ROPE_DIM // 2
    inv_freq = jnp.power(ROPE_THETA, -jnp.arange(half, dtype=F32) * (2.0 / ROPE_DIM))
    ang = positions.astype(F32)[..., None] * inv_freq
    cos, sin = jnp.cos(ang), jnp.sin(ang)
    rest = HEAD_DIM - ROPE_DIM
    cos_t = jnp.concatenate([cos, cos, jnp.ones(cos.shape[:-1] + (rest,), F32)], axis=-1)
    sin_t = jnp.concatenate([-sin, sin, jnp.zeros(sin.shape[:-1] + (rest,), F32)], axis=-1)
    return cos_t, sin_t


def _pad_lanes(v, n=LANES):
    return jnp.pad(v, [(0, 0)] * (v.ndim - 1) + [(0, n - v.shape[-1])])


def _mixer_and_router(x, mod3, positions, norm_mix_pre, norm_mix_post, norm_ffn_pre, w_in, conv_w,
                      dn_a_log, dn_dt_bias, dn_out_norm, w_branch_moba, w_branch_delta, w_out,
                      router_group_w, router_group_b, router_expert_w, router_expert_b):
    bsz, seq, d = x.shape
    t = bsz * seq
    mw = MOBA_HEADS * HEAD_DIM
    dw = DN_HEADS * HEAD_DIM
    o_qa, o_ka, o_va = 0, mw, 2 * mw
    o_dn = 3 * mw
    o_z = o_dn + 3 * dw
    o_ba = o_z + dw
    o_ga = o_ba + 2 * DN_HEADS
    o_gb = o_ga + d
    w_perm = jnp.concatenate(
        [w_in[:, o_ga:o_ga + d], w_in[:, o_gb:o_gb + d], w_in[:, o_qa:o_ba]], axis=1).astype(BF16)
    w_small = _pad_lanes(w_in[:, o_ba:o_ga]).astype(BF16)
    x2 = x.reshape(t, d)
    proj2, ba2 = _inproj(x2, norm_mix_pre[None, :], mod3, w_perm, w_small, seq, sc_chunk=1, sh_chunk=0)
    proj = proj2.reshape(bsz, seq, -1)
    c0 = 2 * d // LANES
    nh = MOBA_HEADS

    cos_t, sin_t = _rope_tables(positions)
    ya = _moba(proj, cos_t, sin_t, q_blk0=c0, k_blk0=c0 + nh, v_blk0=c0 + 2 * nh)

    par = jnp.zeros((8, LANES), F32)
    par = par.at[0, DN_HEADS:2 * DN_HEADS].set(dn_a_log.astype(F32))
    par = par.at[1, DN_HEADS:2 * DN_HEADS].set(dn_dt_bias.astype(F32))
    beta, gcum, glast = _gates(ba2.reshape(bsz, seq, LANES), par)
    ngrp = seq // DN_GROUP
    grow = jnp.transpose(gcum[:, :, DN_HEADS:2 * DN_HEADS], (0, 2, 1)).reshape(bsz * DN_HEADS, ngrp, 1, DN_GROUP)
    d0 = c0 + 3 * nh
    yb = _deltanet(proj, conv_w, beta, gcum, glast, grow, dn_out_norm[None, :].astype(F32),
                   q_blk0=d0, k_blk0=d0 + DN_HEADS, v_blk0=d0 + 2 * DN_HEADS, z_blk0=d0 + 3 * DN_HEADS)

    wr = _pad_lanes(jnp.concatenate([router_group_w, router_expert_w], axis=1)).T
    wr_hi = wr.astype(BF16)
    wr = jnp.concatenate([wr_hi, (wr - wr_hi.astype(F32)).astype(BF16)], axis=0)
    br = _pad_lanes(jnp.concatenate([router_group_b, router_expert_b])[None, :]).T
    return _merge(ya.reshape(t, mw), yb.reshape(t, dw), proj2, x2,
                  w_branch_moba.astype(BF16), w_branch_delta.astype(BF16), w_out.astype(BF16), wr, br,
                  norm_mix_post[None, :], norm_ffn_pre[None, :], mod3, seq, ga_blk=0, gb_blk=1)


def _moe(x1, u2, logits, mod3, norm_ffn_post, w_gate, w_up, w_down, seq):
    t, d = x1.shape
    info_t, info, counts = _route(logits)
    counts = counts[MOE_GROUPS:MOE_GROUPS + MOE_EXPERTS, 0].astype(jnp.int32)
    rb = MOE_ROWS
    padded = (counts + rb - 1) // rb * rb
    pad_end = jnp.cumsum(padded)
    pad_start = pad_end - padded
    eid = info_t[0:MOE_TOPK].astype(jnp.int32)
    rank = info_t[MOE_TOPK:2 * MOE_TOPK].astype(jnp.int32)
    experts = jnp.arange(MOE_EXPERTS, dtype=jnp.int32)[:, None, None]
    start = jnp.sum(jnp.where(eid[None] == experts, pad_start[:, None, None], 0), axis=0)
    dest_flat = (start + rank).reshape(-1)
    n_blocks = (t * MOE_TOPK + MOE_EXPERTS * (rb - 1)) // rb + 1
    n_used = (pad_end[-1] // rb).astype(jnp.int32)
    blk_row = jnp.minimum(jnp.arange(n_blocks, dtype=jnp.int32), n_used - 1) * rb
    block_expert = jnp.minimum(jnp.sum(pad_end[None, :] <= blk_row[:, None], axis=1),
                               MOE_EXPERTS - 1).astype(jnp.int32)
    block_valid = jnp.clip(counts[block_expert] - (blk_row - pad_start[block_expert]), 0, rb).astype(jnp.int32)
    ids = jnp.arange(MOE_EXPERTS, dtype=jnp.int32)
    later_used = (ids[None, :] > ids[:, None]) & (counts[None, :] > 0)
    next_used = jnp.min(jnp.where(later_used, ids[None, :], MOE_EXPERTS), axis=1)
    next_used = jnp.where(next_used < MOE_EXPERTS, next_used, -1).astype(jnp.int32)
    xs = _dispatch(dest_flat, u2, n_blocks * rb)
    ys = _experts(block_expert, next_used[block_expert], block_valid, n_used[None], xs, w_gate, w_up, w_down)
    return _combine(dest_flat, info, x1, norm_ffn_post[None, :], mod3, ys, seq)


def kernel(x, c, positions, w_ada, b_ada, norm_mix_pre, norm_mix_post, norm_ffn_pre, norm_ffn_post, w_in, conv_w, dn_a_log, dn_dt_bias, dn_out_norm, w_branch_moba, w_branch_delta, w_out, router_group_w, router_group_b, router_expert_w, router_expert_b, expert_w_gate, expert_w_up, expert_w_down):
    bsz, seq, d = x.shape
    depth = w_ada.shape[0]
    for layer in range(depth):
        mod = _ada(c, w_ada[layer], b_ada[layer][None, :])
        mod3 = mod.reshape(bsz, 1, -1)
        x1, u2, logits = _mixer_and_router(
            x, mod3, positions, norm_mix_pre[layer], norm_mix_post[layer], norm_ffn_pre[layer], w_in[layer],
            conv_w[layer], dn_a_log[layer], dn_dt_bias[layer], dn_out_norm[layer], w_branch_moba[layer],
            w_branch_delta[layer], w_out[layer], router_group_w[layer], router_group_b[layer],
            router_expert_w[layer], router_expert_b[layer])
        out = _moe(x1, u2, logits, mod3, norm_ffn_post[layer], expert_w_gate[layer], expert_w_up[layer],
                   expert_w_down[layer], seq)
        x = out.reshape(bsz, seq, d)
    return x
```

```python
import functools
import math

import jax
import jax.numpy as jnp
from jax import lax
from jax.experimental import pallas as pl
from jax.experimental.pallas import tpu as pltpu

F32 = jnp.float32
BF16 = jnp.bfloat16
HI = lax.Precision.HIGHEST

NORM_EPS = 1e-6
HEAD_DIM = 128
MOBA_HEADS = 8
MOBA_BLOCK = 256
MOBA_TOPK = 3
MOBA_HEADS_PER_STEP = 2
ROPE_THETA = 500000.0
ROPE_DIM = HEAD_DIM // 4
DN_HEADS = 8
DN_CONV_WIDTH = 4
DN_CHUNK = 64
DN_GROUP = 256
DN_HEADS_PER_STEP = 4
DN_CONV_PAD = 8
DN_CONV_ROWS = 128
MOE_GROUPS = 4
MOE_EXPERTS_PER_GROUP = 8
MOE_EXPERTS = MOE_GROUPS * MOE_EXPERTS_PER_GROUP
MOE_TOPK = 2
MOE_ROWS = 256
DMA_LOOP_UNROLL = 8
MERGE_SUB_ROWS = 128
LANES = 128
NEG = -1e30

VMEM_LIMIT = 60 * 1024 * 1024


def _cparams(*sem):
    return pltpu.CompilerParams(dimension_semantics=sem, vmem_limit_bytes=VMEM_LIMIT)


def _mm(a, b, precision=None):
    return jnp.dot(a, b, precision=precision, preferred_element_type=F32)


def _nt(a, b, precision=None):
    return lax.dot_general(a, b, (((1,), (1,)), ((), ())), precision=precision,
                           preferred_element_type=F32)


def _eye(n, dtype):
    r = lax.broadcasted_iota(jnp.int32, (n, n), 0)
    c = lax.broadcasted_iota(jnp.int32, (n, n), 1)
    return jnp.where(r == c, 1.0, 0.0).astype(dtype)


def _silu(x):
    return x * jax.nn.sigmoid(x)


def _softplus(x):
    return jnp.maximum(x, 0.0) + jnp.log1p(jnp.exp(-jnp.abs(x)))


def _pack_bf16_pair(lo, hi):
    def rne(x):
        b = pltpu.bitcast(x, jnp.uint32)
        return b + jnp.uint32(0x7FFF) + ((b >> 16) & jnp.uint32(1))
    return (rne(hi) & jnp.uint32(0xFFFF0000)) | (rne(lo) >> 16)


def _unpack_bf16_pair(p):
    return pltpu.bitcast(p << 16, F32), pltpu.bitcast(p & jnp.uint32(0xFFFF0000), F32)


def _ada_body(c_ref, w_ref, b_ref, o_ref):
    o_ref[...] = _mm(_silu(c_ref[...]), w_ref[...], HI) + b_ref[...]


def _ada(c, w, b):
    bsz, d = c.shape
    n = w.shape[1]
    tn = 1024
    return pl.pallas_call(
        _ada_body,
        grid=(n // tn,),
        in_specs=[pl.BlockSpec((bsz, d), lambda j: (0, 0)),
                  pl.BlockSpec((d, tn), lambda j: (0, j)),
                  pl.BlockSpec((1, tn), lambda j: (0, j))],
        out_specs=pl.BlockSpec((bsz, tn), lambda j: (0, j)),
        out_shape=jax.ShapeDtypeStruct((bsz, n), F32),
        compiler_params=_cparams("parallel"),
        name="ada",
    )(c, w, b)


def _prenorm_body(x_ref, g_ref, sc_ref, sh_ref, ws_ref, u_ref, os_ref):
    x = x_ref[...]
    y = x * lax.rsqrt(jnp.mean(x * x, axis=-1, keepdims=True) + NORM_EPS) * g_ref[...]
    u = (y * (1.0 + sc_ref[...]) + sh_ref[...]).astype(BF16)
    u_ref[...] = u
    os_ref[...] = _mm(u, ws_ref[...])


def _prenorm(x2, gain, mod3, w_small, seq, *, sc_chunk, sh_chunk):
    t, d = x2.shape
    tm = min(256, seq)
    per_b = seq // tm
    return pl.pallas_call(
        _prenorm_body,
        grid=(t // tm,),
        in_specs=[pl.BlockSpec((tm, d), lambda i: (i, 0)),
                  pl.BlockSpec((1, d), lambda i: (0, 0)),
                  pl.BlockSpec((None, 1, d), lambda i: (i // per_b, 0, sc_chunk)),
                  pl.BlockSpec((None, 1, d), lambda i: (i // per_b, 0, sh_chunk)),
                  pl.BlockSpec((d, LANES), lambda i: (0, 0))],
        out_specs=[pl.BlockSpec((tm, d), lambda i: (i, 0)),
                   pl.BlockSpec((tm, LANES), lambda i: (i, 0))],
        out_shape=[jax.ShapeDtypeStruct((t, d), BF16),
                   jax.ShapeDtypeStruct((t, LANES), F32)],
        compiler_params=_cparams("parallel"),
        name="prenorm",
    )(x2, gain, mod3, mod3, w_small)


def _inproj_body(u_ref, wa_ref, wb_ref, o_ref, w_s, *, n_a):
    j = pl.program_id(0)

    @pl.when(jnp.logical_and(pl.program_id(1) == 0, j < n_a))
    def _():
        w_s[...] = wa_ref[...].astype(BF16)

    @pl.when(jnp.logical_and(pl.program_id(1) == 0, j >= n_a))
    def _():
        w_s[...] = wb_ref[...].astype(BF16)

    o_ref[...] = _mm(u_ref[...], w_s[...]).astype(o_ref.dtype)


def _inproj(u, w_a, n_a, w_b, *, out_first_b):
    t, d = u.shape
    tm = min(1024, t)
    tn = 1024
    n_b = w_b.shape[1] // tn
    once = dict(pipeline_mode=pl.Buffered(1))

    def out_col(j):
        if not out_first_b:
            return j
        return jnp.where(j < n_a, j + n_b, j - n_a)

    return pl.pallas_call(
        functools.partial(_inproj_body, n_a=n_a),
        grid=(n_a + n_b, t // tm),
        in_specs=[pl.BlockSpec((tm, d), lambda j, i: (i, 0)),
                  pl.BlockSpec((d, tn), lambda j, i: (0, jnp.minimum(j, n_a - 1)), **once),
                  pl.BlockSpec((d, tn), lambda j, i: (0, jnp.maximum(j - n_a, 0)), **once)],
        out_specs=pl.BlockSpec((tm, tn), lambda j, i: (i, out_col(j))),
        out_shape=jax.ShapeDtypeStruct((t, (n_a + n_b) * tn), BF16),
        scratch_shapes=[pltpu.VMEM((d, tn), BF16)],
        compiler_params=_cparams("arbitrary", "arbitrary"),
        name="inproj",
    )(u, w_a, w_b)


def _moba_body(q_ref, k_ref, v_ref, cos_ref, sin_ref, o_ref, kr_s, vt_s, km_s, *, nblk, hp):
    blk = MOBA_BLOCK
    half = ROPE_DIM // 2
    lane = lax.broadcasted_iota(jnp.int32, (blk, HEAD_DIM), 1)
    eye_d = _eye(HEAD_DIM, BF16)
    eye_b = _eye(blk, BF16)
    scale = HEAD_DIM ** -0.5

    def rope(xf, rows):
        partner = jnp.where(lane < half, pltpu.roll(xf, HEAD_DIM - half, 1), pltpu.roll(xf, half, 1))
        return xf * cos_ref[rows, :] + partner * sin_ref[rows, :]

    def cols(h):
        return slice(h * HEAD_DIM, (h + 1) * HEAD_DIM)

    km_s[...] = jnp.zeros_like(km_s)
    for j in range(nblk):
        rows = slice(j * blk, (j + 1) * blk)
        for h in range(hp):
            kc = rope(k_ref[rows, cols(h)].astype(F32), rows)
            kr_s[rows, cols(h)] = kc.astype(BF16)
            km_s[j:j + 1, cols(h)] = jnp.mean(kc, axis=0, keepdims=True)
            vt_s[cols(h), rows] = _nt(eye_d, v_ref[rows, cols(h)]).astype(BF16)

    key_i = lax.broadcasted_iota(jnp.int32, (blk, blk), 0)
    qry_i = lax.broadcasted_iota(jnp.int32, (blk, blk), 1)
    blk_i = lax.broadcasted_iota(jnp.int32, (8, blk), 0)
    causal_bias = jnp.where(key_i <= qry_i, 0.0, NEG)

    def rows(j):
        return slice(j * blk, (j + 1) * blk)

    def begin(i, h):
        qc = rope(q_ref[rows(i), cols(h)].astype(F32), rows(i))
        st = dict(i=i, h=h, qs=(qc * scale).astype(BF16), scores=[], m=None, bias=None)
        if i > 0:
            g_t = _nt(km_s[:, cols(h)], qc, HI)
            rank = jnp.zeros((8, blk), F32)
            for jp in range(i):
                row = g_t[jp:jp + 1, :]
                beats = (row > g_t) | ((row == g_t) & (jp < blk_i))
                rank = rank + jnp.where(beats, 1.0, 0.0)
            sel = (rank < float(MOBA_TOPK)) & (blk_i < i)
            st["bias"] = jnp.where(sel, 0.0, NEG)
        return st

    def score(st, j):
        i = st["i"]
        s = _nt(kr_s[rows(j), cols(st["h"])], st["qs"])
        if j == i:
            s = s + causal_bias
        st["scores"].append(s)
        mj = jnp.max(s, axis=0, keepdims=True)
        if j < i:
            mj = mj + st["bias"][j:j + 1, :]
        st["m"] = mj if st["m"] is None else jnp.maximum(st["m"], mj)

    def accumulate(st, j):
        i = st["i"]
        if j == 0:
            st["den"] = jnp.zeros((1, blk), F32)
            st["acc"] = jnp.zeros((HEAD_DIM, blk), F32)
        shift = (st["bias"][j:j + 1, :] - st["m"]) if j < i else -st["m"]
        p = jnp.exp(st["scores"][j] + shift)
        st["den"] = st["den"] + jnp.sum(p, axis=0, keepdims=True)
        st["acc"] = st["acc"] + _mm(vt_s[cols(st["h"]), rows(j)], p.astype(BF16))
        if j == i:
            o_t = (st["acc"] / st["den"]).astype(BF16)
            o_ref[rows(i), cols(st["h"])] = _nt(eye_b, o_t).astype(o_ref.dtype)

    prev = None
    for i in range(nblk):
        cur = [begin(i, h) for h in range(hp)]
        for j in range(i + 1):
            for h in range(hp):
                score(cur[h], j)
                if prev is not None and j <= i - 1:
                    accumulate(prev[h], j)
        prev = cur
    for j in range(nblk):
        for h in range(hp):
            accumulate(prev[h], j)


def _moba(proj, cos_t, sin_t, *, q_blk0, k_blk0, v_blk0):
    bsz, seq, _ = proj.shape
    nblk = seq // MOBA_BLOCK
    assert seq % MOBA_BLOCK == 0 and 1 <= nblk <= 8
    hd = HEAD_DIM
    hp = MOBA_HEADS_PER_STEP
    wide = hp * hd
    assert MOBA_HEADS % hp == 0 and all(b0 % hp == 0 for b0 in (q_blk0, k_blk0, v_blk0))
    return pl.pallas_call(
        functools.partial(_moba_body, nblk=nblk, hp=hp),
        grid=(bsz, MOBA_HEADS // hp),
        in_specs=[pl.BlockSpec((None, seq, wide), lambda b, h: (b, 0, q_blk0 // hp + h)),
                  pl.BlockSpec((None, seq, wide), lambda b, h: (b, 0, k_blk0 // hp + h)),
                  pl.BlockSpec((None, seq, wide), lambda b, h: (b, 0, v_blk0 // hp + h)),
                  pl.BlockSpec((None, seq, hd), lambda b, h: (b, 0, 0)),
                  pl.BlockSpec((None, seq, hd), lambda b, h: (b, 0, 0))],
        out_specs=pl.BlockSpec((None, seq, wide), lambda b, h: (b, 0, h)),
        out_shape=jax.ShapeDtypeStruct((bsz, seq, MOBA_HEADS * hd), BF16),
        scratch_shapes=[pltpu.VMEM((seq, wide), BF16),
                        pltpu.VMEM((wide, seq), BF16),
                        pltpu.VMEM((8, wide), F32)],
        compiler_params=_cparams("parallel", "parallel"),
        name="moba",
    )(proj, proj, proj, cos_t, sin_t)


def _chunk_masks(n, chunk):
    r = lax.broadcasted_iota(jnp.int32, (n, n), 0)
    c = lax.broadcasted_iota(jnp.int32, (n, n), 1)
    shift = int(math.log2(chunk))
    same = jnp.right_shift(r, shift) == jnp.right_shift(c, shift)
    return r, c, same


def _gates_body(ba_ref, par_ref, beta_ref, g_ref, gl_ref, *, seq):
    grp = DN_GROUP
    r, c, same = _chunk_masks(grp, DN_CHUNK)
    low = jnp.where(same & (c <= r), 1.0, 0.0)
    ones = jnp.where(same, 1.0, 0.0)
    neg_a = -jnp.exp(par_ref[0:1, :])
    dt_b = par_ref[1:2, :]
    for i in range(seq // grp):
        rows = slice(i * grp, (i + 1) * grp)
        x = ba_ref[rows, :].astype(F32)
        beta_ref[rows, :] = jax.nn.sigmoid(x)
        g = neg_a * _softplus(x + dt_b)
        g_ref[rows, :] = _mm(low, g, HI)
        gl_ref[rows, :] = _mm(ones, g, HI)


def _gates(ba, par):
    bsz, seq, _ = ba.shape
    spec = pl.BlockSpec((None, seq, LANES), lambda b: (b, 0, 0))
    return pl.pallas_call(
        functools.partial(_gates_body, seq=seq),
        grid=(bsz,),
        in_specs=[spec, pl.BlockSpec((8, LANES), lambda b: (0, 0))],
        out_specs=[spec, spec, spec],
        out_shape=[jax.ShapeDtypeStruct((bsz, seq, LANES), F32)] * 3,
        compiler_params=_cparams("parallel"),
        name="dn_gates",
    )(ba, par)


def _dn_body(q_ref, k_ref, v_ref, z_ref, cwq_ref, cwk_ref, cwv_ref, beta_ref, g_ref, gl_ref, grow_ref,
             gain_ref, o_ref, xpad_s, qn_s, kn_s, vn_s, state_s, vnew_s, *bufs, seq, hp):
    bufs_a, bufs_b = bufs[:5], bufs[5:]
    hg = pl.program_id(1)
    grp = DN_GROUP
    chunk = DN_CHUNK
    ngrp = seq // grp
    hd = HEAD_DIM
    pad = DN_CONV_PAD

    crows = DN_CONV_ROWS
    nchunk = seq // crows
    for src, cw_ref, dst, l2, post in ((q_ref, cwq_ref, qn_s, True, hd ** -0.5),
                                       (k_ref, cwk_ref, kn_s, True, None),
                                       (v_ref, cwv_ref, vn_s, False, None)):
        def copy(c, carry, src=src):
            r0 = pl.multiple_of(c * crows, crows)
            xpad_s[c, pad:pad + crows, :] = src[pl.ds(r0, crows), :].astype(F32)
            return carry

        def history(c, carry):
            xpad_s[c, 0:pad, :] = xpad_s[c - 1, crows:crows + pad, :]
            return carry

        lax.fori_loop(0, nchunk, copy, 0)
        xpad_s[0, 0:pad, :] = jnp.zeros((pad, hp * hd), F32)
        lax.fori_loop(1, nchunk, history, 0)
        cw = cw_ref[...]

        def conv(c, carry, cw=cw, dst=dst, l2=l2, post=post):
            r0 = pl.multiple_of(c * crows, crows)
            for hh in range(hp):
                cols = slice(hh * hd, (hh + 1) * hd)
                acc = None
                for j in range(DN_CONV_WIDTH):
                    off = pad - (DN_CONV_WIDTH - 1) + j
                    term = xpad_s[c, off:off + crows, cols] * cw[j:j + 1, cols]
                    acc = term if acc is None else acc + term
                yh = _silu(acc)
                if l2:
                    yh = yh * lax.rsqrt(jnp.sum(yh * yh, axis=-1, keepdims=True) + NORM_EPS)
                if post is not None:
                    yh = yh * post
                dst[pl.ds(r0, crows), cols] = yh
            return carry

        lax.fori_loop(0, nchunk, conv, 0)

    r, c, same = _chunk_masks(grp, chunk)
    low_incl = same & (c <= r)
    low_strict = same & (c < r)
    eye_g = jnp.where(r == c, 1.0, 0.0)
    eye_d = _eye(hd, BF16)
    lane = lax.broadcasted_iota(jnp.int32, (grp, LANES), 1)
    col_chunk = jnp.right_shift(lax.broadcasted_iota(jnp.int32, (hd, grp), 1), int(math.log2(chunk)))
    gain = gain_ref[...]
    n_double = int(math.log2(chunk)) - 1

    def pick(ref, r0, lane_id):
        return jnp.sum(jnp.where(lane == lane_id, ref[pl.ds(r0, grp), :], 0.0), axis=-1, keepdims=True)

    state_s[...] = jnp.zeros_like(state_s)
    heads = range(hp)

    cols = [slice(hh * hd, (hh + 1) * hd) for hh in heads]
    head = [hg * hp + hh for hh in heads]

    def prepare(gi, bufs):
        u_r, w_r, qk_r, qd_r, kt_r = bufs
        r0 = pl.multiple_of(gi * grp, grp)
        q = [qn_s[pl.ds(r0, grp), cols[hh]] for hh in heads]
        k = [kn_s[pl.ds(r0, grp), cols[hh]] for hh in heads]
        v = [vn_s[pl.ds(r0, grp), cols[hh]] for hh in heads]
        beta = [pick(beta_ref, r0, head[hh]) for hh in heads]
        g_col = [pick(g_ref, r0, DN_HEADS + head[hh]) for hh in heads]
        gl_col = [pick(gl_ref, r0, DN_HEADS + head[hh]) for hh in heads]
        decay = [jnp.exp(jnp.minimum(g_col[hh] - grow_ref[hh, gi], 0.0)) for hh in heads]
        e_g = [jnp.exp(g_col[hh]) for hh in heads]
        kb = [k[hh] * beta[hh] for hh in heads]
        vb = [(v[hh] * beta[hh]).astype(BF16) for hh in heads]
        k16 = [k[hh].astype(BF16) for hh in heads]
        n_mat = [jnp.where(low_strict, _nt(kb[hh].astype(BF16), k16[hh]) * decay[hh], 0.0) for hh in heads]
        p = [(-n_mat[hh]).astype(BF16) for hh in heads]
        x = [eye_g - n_mat[hh] for hh in heads]
        qk = [jnp.where(low_incl, _nt(q[hh].astype(BF16), k16[hh]) * decay[hh], 0.0).astype(BF16) for hh in heads]
        for hh in heads:
            qk_r[hh] = qk[hh]
            qd_r[hh] = (q[hh] * e_g[hh]).astype(BF16)
            kt_r[hh] = _nt(eye_d, (k[hh] * jnp.exp(gl_col[hh] - g_col[hh])).astype(BF16)).astype(BF16)
        yield
        for _ in range(n_double):
            p2 = [_mm(p[hh], p[hh]).astype(BF16) for hh in heads]
            x = [x[hh] + _mm(x[hh].astype(BF16), p2[hh]) for hh in heads]
            p = p2
            yield
        x16 = [x[hh].astype(BF16) for hh in heads]
        for hh in heads:
            u_r[hh] = _mm(x16[hh], vb[hh])
            w_r[hh] = _mm(x16[hh], (kb[hh] * e_g[hh]).astype(BF16)).astype(BF16)
        yield

    def scan(gi, bufs):
        u_r, w_r, qk_r, qd_r, kt_r = bufs
        r0 = pl.multiple_of(gi * grp, grp)
        vnew_s[...] = jnp.zeros_like(vnew_s)
        for ci in range(grp // chunk):
            rows = slice(ci * chunk, (ci + 1) * chunk)
            gl_row = jnp.exp(gl_ref[pl.ds(r0 + ci * chunk, 1), :])
            for hh in heads:
                state = state_s[hh]
                s16 = state.astype(BF16)
                v_new = u_r[hh, rows, :] - _mm(w_r[hh, rows, :], s16)
                vnew_s[hh, rows, :] = v_new.astype(BF16)
                o = _mm(qd_r[hh, rows, :], s16) + _mm(qk_r[hh, rows, :], vnew_s[hh])
                chunk_decay = jnp.sum(jnp.where(lane[0:1, :] == DN_HEADS + head[hh], gl_row, 0.0),
                                      axis=-1, keepdims=True)
                k_tail_t = kt_r[hh]
                kt = jnp.where(col_chunk == ci, k_tail_t, jnp.zeros_like(k_tail_t))
                state_s[hh] = state * chunk_decay + _mm(kt, vnew_s[hh])
                on = o * lax.rsqrt(jnp.mean(o * o, axis=-1, keepdims=True) + NORM_EPS) * gain
                zz = z_ref[pl.ds(r0 + ci * chunk, chunk), cols[hh]].astype(F32)
                o_ref[pl.ds(r0 + ci * chunk, chunk), cols[hh]] = (on * _silu(zz)).astype(o_ref.dtype)
            yield

    def interleave(scan_steps, prepare_steps):
        for _ in scan_steps:
            next(prepare_steps, None)
            next(prepare_steps, None)
        for _ in prepare_steps:
            pass

    for _ in prepare(0, bufs_a):
        pass

    def pair(pi, carry):
        g0 = 2 * pi
        interleave(scan(g0, bufs_a), prepare(g0 + 1, bufs_b))
        interleave(scan(g0 + 1, bufs_b), prepare(g0 + 2, bufs_a))
        return carry

    lax.fori_loop(0, ngrp // 2 - 1, pair, 0)
    interleave(scan(ngrp - 2, bufs_a), prepare(ngrp - 1, bufs_b))
    for _ in scan(ngrp - 1, bufs_b):
        pass


def _deltanet(proj, conv_w, beta, gcum, glast, grow, gain, *, q_blk0, k_blk0, v_blk0, z_blk0):
    bsz, seq, _ = proj.shape
    hd = HEAD_DIM
    nh = DN_HEADS
    assert seq % DN_GROUP == 0
    ngrp = seq // DN_GROUP

    hp = DN_HEADS_PER_STEP
    wide = hp * hd
    assert nh % hp == 0 and all(b0 % hp == 0 for b0 in (q_blk0, k_blk0, v_blk0, z_blk0))

    def col(blk0):
        return pl.BlockSpec((None, seq, wide), lambda b, h: (b, 0, blk0 // hp + h))

    def cw(blk0):
        return pl.BlockSpec((DN_CONV_WIDTH, wide), lambda b, h: (0, blk0 // hp + h))

    full = pl.BlockSpec((None, seq, LANES), lambda b, h: (b, 0, 0))
    return pl.pallas_call(
        functools.partial(_dn_body, seq=seq, hp=hp),
        grid=(bsz, nh // hp),
        in_specs=[col(q_blk0), col(k_blk0), col(v_blk0), col(z_blk0),
                  cw(0), cw(nh), cw(2 * nh),
                  full, full, full,
                  pl.BlockSpec((hp, ngrp, 1, DN_GROUP), lambda b, h: (b * (nh // hp) + h, 0, 0, 0)),
                  pl.BlockSpec((1, hd), lambda b, h: (0, 0))],
        out_specs=pl.BlockSpec((None, seq, wide), lambda b, h: (b, 0, h)),
        out_shape=jax.ShapeDtypeStruct((bsz, seq, nh * hd), BF16),
        scratch_shapes=[pltpu.VMEM((seq // DN_CONV_ROWS, DN_CONV_ROWS + DN_CONV_PAD, wide), F32)]
        + [pltpu.VMEM((seq, wide), F32)] * 3
        + [pltpu.VMEM((hp, hd, hd), F32), pltpu.VMEM((hp, DN_GROUP, hd), BF16)]
        + [pltpu.VMEM((hp, DN_GROUP, hd), F32), pltpu.VMEM((hp, DN_GROUP, hd), BF16),
           pltpu.VMEM((hp, DN_GROUP, DN_GROUP), BF16), pltpu.VMEM((hp, DN_GROUP, hd), BF16),
           pltpu.VMEM((hp, hd, DN_GROUP), BF16)] * 2,
        compiler_params=_cparams("parallel", "parallel"),
        name="deltanet",
    )(proj, proj, proj, proj, conv_w, conv_w, conv_w, beta, gcum, glast, grow, gain)


def _merge_body(ya_ref, yb_ref, ga_ref, gb_ref, x_ref, wm_ref, wd_ref, wo_ref, wr_ref, br_ref,
                npost_ref, npre_ref, gt_ref, sc_ref, sh_ref, x1_ref, u2_ref, lg_ref):
    sub = MERGE_SUB_ROWS
    for s in range(x_ref.shape[0] // sub):
        rows = slice(s * sub, (s + 1) * sub)
        ma = _mm(ya_ref[rows, :], wm_ref[...])
        mb = _mm(yb_ref[rows, :], wd_ref[...])
        merged = (jax.nn.sigmoid(ga_ref[rows, :].astype(F32)) * ma
                  + jax.nn.sigmoid(gb_ref[rows, :].astype(F32)) * mb)
        y = _mm(merged.astype(BF16), wo_ref[...])
        yn = y * lax.rsqrt(jnp.mean(y * y, axis=-1, keepdims=True) + NORM_EPS) * npost_ref[...]
        x1 = x_ref[rows, :] + gt_ref[...] * yn
        x1_ref[rows, :] = x1
        un = x1 * lax.rsqrt(jnp.mean(x1 * x1, axis=-1, keepdims=True) + NORM_EPS) * npre_ref[...]
        u2 = un * (1.0 + sc_ref[...]) + sh_ref[...]
        half = u2.shape[1] // 2
        u2_ref[rows, :] = _pack_bf16_pair(u2[:, :half], u2[:, half:])
        u_hi = u2.astype(BF16)
        u_lo = (u2 - u_hi.astype(F32)).astype(BF16)
        lg2 = _nt(wr_ref[...], u_hi)
        lg_ref[:, rows] = lg2[:LANES, :] + lg2[LANES:, :] + _nt(wr_ref[:LANES, :], u_lo) + br_ref[...]


def _merge(ya, yb, proj2, x2, wm, wd, wo, wr, br, npost, npre, mod3, seq, *, ga_blk, gb_blk):
    t, d = x2.shape
    wa = ya.shape[1]
    tm = min(256, seq)
    per_b = seq // tm
    const = lambda i: (0, 0)
    once = dict(pipeline_mode=pl.Buffered(1))

    def modspec(chunk):
        return pl.BlockSpec((None, 1, d), lambda i: (i // per_b, 0, chunk))

    return pl.pallas_call(
        _merge_body,
        grid=(t // tm,),
        in_specs=[pl.BlockSpec((tm, wa), lambda i: (i, 0)),
                  pl.BlockSpec((tm, wa), lambda i: (i, 0)),
                  pl.BlockSpec((tm, d), lambda i: (i, ga_blk)),
                  pl.BlockSpec((tm, d), lambda i: (i, gb_blk)),
                  pl.BlockSpec((tm, d), lambda i: (i, 0)),
                  pl.BlockSpec((wa, d), const, **once),
                  pl.BlockSpec((wa, d), const, **once),
                  pl.BlockSpec((d, d), const, **once),
                  pl.BlockSpec((2 * LANES, d), const, **once),
                  pl.BlockSpec((LANES, 1), const),
                  pl.BlockSpec((1, d), const),
                  pl.BlockSpec((1, d), const),
                  modspec(2), modspec(4), modspec(3)],
        out_specs=[pl.BlockSpec((tm, d), lambda i: (i, 0)),
                   pl.BlockSpec((tm, d // 2), lambda i: (i, 0)),
                   pl.BlockSpec((LANES, tm), lambda i: (0, i))],
        out_shape=[jax.ShapeDtypeStruct((t, d), F32),
                   jax.ShapeDtypeStruct((t, d // 2), jnp.uint32),
                   jax.ShapeDtypeStruct((LANES, t), F32)],
        compiler_params=_cparams("parallel"),
        name="merge",
    )(ya, yb, proj2, proj2, x2, wm, wd, wo, wr, br, npost, npre, mod3, mod3, mod3)


def _route_body(lg_ref, info_ref, col_ref, cnt_ref, run_s, *, tr):
    @pl.when(pl.program_id(0) == 0)
    def _():
        run_s[...] = jnp.zeros_like(run_s)

    lg = lg_ref[...]
    row = lax.broadcasted_iota(jnp.int32, (LANES, tr), 0)
    row_f = row.astype(F32)
    big = float(LANES)

    def first_max(vals, mask):
        mx = jnp.max(jnp.where(mask, vals, NEG), axis=0, keepdims=True)
        idx = jnp.min(jnp.where(mask & (vals == mx), row_f, big), axis=0, keepdims=True)
        return mx, idx

    gmask = row < MOE_GROUPS
    gmax, gidx = first_max(lg, gmask)
    p_group = 1.0 / jnp.sum(jnp.where(gmask, jnp.exp(lg - gmax), 0.0), axis=0, keepdims=True)
    lo = float(MOE_GROUPS) + gidx * float(MOE_EXPERTS_PER_GROUP)
    emask = (row_f >= lo) & (row_f < lo + float(MOE_EXPERTS_PER_GROUP))
    m1, i1 = first_max(lg, emask)
    m2, i2 = first_max(lg, emask & (row_f != i1))
    e2 = jnp.exp(m2 - m1)
    w1 = p_group / (1.0 + e2)
    w2 = p_group * e2 / (1.0 + e2)
    oh1 = row_f == i1
    oh2 = row_f == i2
    oh = jnp.where(oh1 | oh2, 1.0, 0.0).astype(BF16)
    r = lax.broadcasted_iota(jnp.int32, (tr, tr), 0)
    c = lax.broadcasted_iota(jnp.int32, (tr, tr), 1)
    before = jnp.where(r < c, 1.0, 0.0).astype(BF16)
    prefix = _mm(oh, before) + run_s[:, 0:1]
    rank1 = jnp.sum(jnp.where(oh1, prefix, 0.0), axis=0, keepdims=True)
    rank2 = jnp.sum(jnp.where(oh2, prefix, 0.0), axis=0, keepdims=True)
    run_s[...] = run_s[...] + jnp.sum(oh.astype(F32), axis=1, keepdims=True)
    goff = float(MOE_GROUPS)
    info = jnp.where(row == 0, i1 - goff, 0.0)
    info = jnp.where(row == 1, i2 - goff, info)
    info = jnp.where(row == 2, rank1, info)
    info = jnp.where(row == 3, rank2, info)
    info = jnp.where(row == 4, w1, info)
    info = jnp.where(row == 5, w2, info)
    info_ref[...] = info[0:8, :]
    col_ref[...] = _nt(_eye(tr, F32), info, HI)
    cnt_ref[...] = run_s[...]


def _route(logits_t):
    t = logits_t.shape[1]
    tr = min(256, t)
    return pl.pallas_call(
        functools.partial(_route_body, tr=tr),
        grid=(t // tr,),
        in_specs=[pl.BlockSpec((LANES, tr), lambda i: (0, i))],
        out_specs=[pl.BlockSpec((8, tr), lambda i: (0, i)),
                   pl.BlockSpec((tr, LANES), lambda i: (i, 0)),
                   pl.BlockSpec((LANES, LANES), lambda i: (0, 0))],
        out_shape=[jax.ShapeDtypeStruct((8, t), F32),
                   jax.ShapeDtypeStruct((t, LANES), F32),
                   jax.ShapeDtypeStruct((LANES, LANES), F32)],
        scratch_shapes=[pltpu.VMEM((LANES, LANES), F32)],
        compiler_params=_cparams("arbitrary"),
        name="route",
    )(logits_t)


def _dispatch_body(d0_ref, d1_ref, u_ref, xs_ref, sem, *, td):
    dests = (d0_ref, d1_ref)

    def row_copy(r, k):
        return pltpu.make_async_copy(u_ref.at[pl.ds(r, 1), :], xs_ref.at[pl.ds(dests[k][r], 1), :], sem)

    def start(r, carry):
        for k in range(MOE_TOPK):
            row_copy(r, k).start()
        return carry

    def wait(r, carry):
        for k in range(MOE_TOPK):
            row_copy(r, k).wait()
        return carry

    lax.fori_loop(0, td, start, 0, unroll=DMA_LOOP_UNROLL)
    lax.fori_loop(0, td, wait, 0, unroll=DMA_LOOP_UNROLL)


def _dispatch(dest, u2, n_rows):
    t, d = u2.shape
    td = min(1024, t)
    nt = t // td
    return pl.pallas_call(
        functools.partial(_dispatch_body, td=td),
        grid=(nt,),
        in_specs=[pl.BlockSpec((td,), lambda i: (i,), memory_space=pltpu.SMEM),
                  pl.BlockSpec((td,), lambda i: (nt + i,), memory_space=pltpu.SMEM),
                  pl.BlockSpec((td, d), lambda i: (i, 0))],
        out_specs=pl.BlockSpec(memory_space=pl.ANY),
        out_shape=jax.ShapeDtypeStruct((n_rows, d), u2.dtype),
        scratch_shapes=[pltpu.SemaphoreType.DMA(())],
        compiler_params=_cparams("arbitrary"),
        name="dispatch",
    )(dest, dest, u2)


def _experts_body(be_ref, nx_ref, nv_ref, nu_ref, x_ref, wg_ref, wu_ref, wd_ref, o_ref,
                  stage_g, stage_u, stage_d, wg_s, wu_s, wd_s, sems):
    i = pl.program_id(0)

    def fetch(e):
        return (pltpu.make_async_copy(wg_ref.at[e], stage_g, sems.at[0]),
                pltpu.make_async_copy(wu_ref.at[e], stage_u, sems.at[1]),
                pltpu.make_async_copy(wd_ref.at[e], stage_d, sems.at[2]))

    @pl.when(i == 0)
    def _():
        for cp in fetch(be_ref[0]):
            cp.start()

    @pl.when(jnp.logical_or(i == 0, be_ref[i] != be_ref[jnp.maximum(i - 1, 0)]))
    def _():
        for cp in fetch(be_ref[i]):
            cp.wait()
        wg_s[...] = stage_g[...].astype(BF16)
        wu_s[...] = stage_u[...].astype(BF16)
        wd_s[...] = stage_d[...].astype(BF16)

        @pl.when(nx_ref[i] >= 0)
        def _():
            for cp in fetch(nx_ref[i]):
                cp.start()

    @pl.when(i < nu_ref[0])
    def _():
        row = lax.broadcasted_iota(jnp.int32, (x_ref.shape[0], 1), 0)
        packed = jnp.where(row < nv_ref[i], x_ref[...], jnp.uint32(0))
        x_lo, x_hi = (v.astype(BF16) for v in _unpack_bf16_pair(packed))
        half = packed.shape[1]
        gate = _mm(x_lo, wg_s[0:half, :]) + _mm(x_hi, wg_s[half:, :])
        up = _mm(x_lo, wu_s[0:half, :]) + _mm(x_hi, wu_s[half:, :])
        y = _mm((_silu(gate) * up).astype(BF16), wd_s[...])
        o_ref[...] = _pack_bf16_pair(y[:, :half], y[:, half:])

    @pl.when(i >= nu_ref[0])
    def _():
        o_ref[...] = jnp.zeros_like(o_ref)


def _experts(block_expert, next_expert, block_valid, n_used, xs, wg, wu, wd):
    nr, dp = xs.shape
    d, ff = wg.shape[1], wg.shape[2]
    assert d == 2 * dp
    rb = MOE_ROWS
    row_map = lambda i, be, nx, nv, nu: (jnp.minimum(i, nu[0] - 1), 0)
    hbm = pl.BlockSpec(memory_space=pl.ANY)
    grid_spec = pltpu.PrefetchScalarGridSpec(
        num_scalar_prefetch=4,
        grid=(nr // rb,),
        in_specs=[pl.BlockSpec((rb, dp), row_map), hbm, hbm, hbm],
        out_specs=pl.BlockSpec((rb, dp), lambda i, be, nx, nv, nu: (i, 0)),
        scratch_shapes=[pltpu.VMEM((d, ff), F32), pltpu.VMEM((d, ff), F32), pltpu.VMEM((ff, d), F32),
                        pltpu.VMEM((d, ff), BF16), pltpu.VMEM((d, ff), BF16), pltpu.VMEM((ff, d), BF16),
                        pltpu.SemaphoreType.DMA((3,))],
    )
    return pl.pallas_call(
        _experts_body,
        grid_spec=grid_spec,
        out_shape=jax.ShapeDtypeStruct((nr, dp), jnp.uint32),
        compiler_params=_cparams("arbitrary"),
        name="experts",
    )(block_expert, next_expert, block_valid, n_used, xs, wg, wu, wd)


def _combine_body(d0_ref, d1_ref, info_ref, x1_ref, npost_ref, gt_ref, yb_ref, o_ref, buf, sem, *, tc, rc):
    dests = (d0_ref, d1_ref)

    def row_copy(r, k):
        return pltpu.make_async_copy(yb_ref.at[pl.ds(dests[k][r], 1), :], buf.at[k, pl.ds(r, 1), :], sem)

    def start(r, carry):
        for k in range(MOE_TOPK):
            row_copy(r, k).start()
        return carry

    def wait(r, carry):
        for k in range(MOE_TOPK):
            row_copy(r, k).wait()
        return carry

    lax.fori_loop(0, tc, start, 0, unroll=DMA_LOOP_UNROLL)
    lax.fori_loop(0, tc, wait, 0, unroll=DMA_LOOP_UNROLL)

    half = buf.shape[2]
    gain_lo, gain_hi = npost_ref[:, :half], npost_ref[:, half:]
    gate_lo, gate_hi = gt_ref[:, :half], gt_ref[:, half:]

    def chunk(i, carry):
        r0 = pl.multiple_of(i * rc, rc)
        info = info_ref[pl.ds(r0, rc), :]
        w0, w1 = info[:, 4:5], info[:, 5:6]
        a_lo, a_hi = _unpack_bf16_pair(buf[0, pl.ds(r0, rc), :])
        b_lo, b_hi = _unpack_bf16_pair(buf[1, pl.ds(r0, rc), :])
        y_lo = w0 * a_lo + w1 * b_lo
        y_hi = w0 * a_hi + w1 * b_hi
        ms = (jnp.sum(y_lo * y_lo, axis=-1, keepdims=True)
              + jnp.sum(y_hi * y_hi, axis=-1, keepdims=True)) * (1.0 / (2 * half))
        inv = lax.rsqrt(ms + NORM_EPS)
        o_ref[pl.ds(r0, rc), :half] = x1_ref[pl.ds(r0, rc), :half] + gate_lo * (y_lo * inv * gain_lo)
        o_ref[pl.ds(r0, rc), half:] = x1_ref[pl.ds(r0, rc), half:] + gate_hi * (y_hi * inv * gain_hi)
        return carry

    lax.fori_loop(0, tc // rc, chunk, 0)


def _combine(dest, info, x1, npost, mod3, yb, seq):
    t, d = x1.shape
    tc = min(1024, seq)
    rc = min(128, tc)
    per_b = seq // tc
    nt = t // tc
    return pl.pallas_call(
        functools.partial(_combine_body, tc=tc, rc=rc),
        grid=(nt,),
        in_specs=[pl.BlockSpec((tc,), lambda i: (i,), memory_space=pltpu.SMEM),
                  pl.BlockSpec((tc,), lambda i: (nt + i,), memory_space=pltpu.SMEM),
                  pl.BlockSpec((tc, LANES), lambda i: (i, 0)),
                  pl.BlockSpec((tc, d), lambda i: (i, 0)),
                  pl.BlockSpec((1, d), lambda i: (0, 0)),
                  pl.BlockSpec((None, 1, d), lambda i: (i // per_b, 0, 5)),
                  pl.BlockSpec(memory_space=pl.ANY)],
        out_specs=pl.BlockSpec((tc, d), lambda i: (i, 0)),
        out_shape=jax.ShapeDtypeStruct((t, d), F32),
        scratch_shapes=[pltpu.VMEM((MOE_TOPK, tc, d // 2), jnp.uint32), pltpu.SemaphoreType.DMA(())],
        compiler_params=_cparams("arbitrary"),
        name="combine",
    )(dest, dest, info, x1, npost, mod3, yb)


def _rope_tables(positions):
    half = ROPE_DIM // 2
    inv_freq = jnp.power(ROPE_THETA, -jnp.arange(half, dtype=F32) * (2.0 / ROPE_DIM))
    ang = positions.astype(F32)[..., None] * inv_freq
    cos, sin = jnp.cos(ang), jnp.sin(ang)
    rest = HEAD_DIM - ROPE_DIM
    cos_t = jnp.concatenate([cos, cos, jnp.ones(cos.shape[:-1] + (rest,), F32)], axis=-1)
    sin_t = jnp.concatenate([-sin, sin, jnp.zeros(sin.shape[:-1] + (rest,), F32)], axis=-1)
    return cos_t, sin_t


def _pad_lanes(v, n=LANES):
    return jnp.pad(v, [(0, 0)] * (v.ndim - 1) + [(0, n - v.shape[-1])])


def _mixer_and_router(x, mod3, positions, norm_mix_pre, norm_mix_post, norm_ffn_pre, w_in, conv_w,
                      dn_a_log, dn_dt_bias, dn_out_norm, w_branch_moba, w_branch_delta, w_out,
                      router_group_w, router_group_b, router_expert_w, router_expert_b):
    bsz, seq, d = x.shape
    t = bsz * seq
    mw = MOBA_HEADS * HEAD_DIM
    dw = DN_HEADS * HEAD_DIM
    o_qa, o_ka, o_va = 0, mw, 2 * mw
    o_dn = 3 * mw
    o_z = o_dn + 3 * dw
    o_ba = o_z + dw
    o_ga = o_ba + 2 * DN_HEADS
    o_gb = o_ga + d
    w_small = _pad_lanes(w_in[:, o_ba:o_ga]).astype(BF16)
    x2 = x.reshape(t, d)
    u, ba2 = _prenorm(x2, norm_mix_pre[None, :], mod3, w_small, seq, sc_chunk=1, sh_chunk=0)
    assert o_ba % 1024 == 0
    proj2 = _inproj(u, w_in, o_ba // 1024, w_in[:, o_ga:], out_first_b=True)
    proj = proj2.reshape(bsz, seq, -1)
    c0 = 2 * d // LANES
    nh = MOBA_HEADS

    cos_t, sin_t = _rope_tables(positions)
    ya = _moba(proj, cos_t, sin_t, q_blk0=c0, k_blk0=c0 + nh, v_blk0=c0 + 2 * nh)

    par = jnp.zeros((8, LANES), F32)
    par = par.at[0, DN_HEADS:2 * DN_HEADS].set(dn_a_log.astype(F32))
    par = par.at[1, DN_HEADS:2 * DN_HEADS].set(dn_dt_bias.astype(F32))
    beta, gcum, glast = _gates(ba2.reshape(bsz, seq, LANES), par)
    ngrp = seq // DN_GROUP
    grow = jnp.transpose(gcum[:, :, DN_HEADS:2 * DN_HEADS], (0, 2, 1)).reshape(bsz * DN_HEADS, ngrp, 1, DN_GROUP)
    d0 = c0 + 3 * nh
    yb = _deltanet(proj, conv_w, beta, gcum, glast, grow, dn_out_norm[None, :].astype(F32),
                   q_blk0=d0, k_blk0=d0 + DN_HEADS, v_blk0=d0 + 2 * DN_HEADS, z_blk0=d0 + 3 * DN_HEADS)

    wr = _pad_lanes(jnp.concatenate([router_group_w, router_expert_w], axis=1)).T
    wr_hi = wr.astype(BF16)
    wr = jnp.concatenate([wr_hi, (wr - wr_hi.astype(F32)).astype(BF16)], axis=0)
    br = _pad_lanes(jnp.concatenate([router_group_b, router_expert_b])[None, :]).T
    return _merge(ya.reshape(t, mw), yb.reshape(t, dw), proj2, x2,
                  w_branch_moba.astype(BF16), w_branch_delta.astype(BF16), w_out.astype(BF16), wr, br,
                  norm_mix_post[None, :], norm_ffn_pre[None, :], mod3, seq, ga_blk=0, gb_blk=1)


def _moe(x1, u2, logits, mod3, norm_ffn_post, w_gate, w_up, w_down, seq):
    t, d = x1.shape
    info_t, info, counts = _route(logits)
    counts = counts[MOE_GROUPS:MOE_GROUPS + MOE_EXPERTS, 0].astype(jnp.int32)
    rb = MOE_ROWS
    padded = (counts + rb - 1) // rb * rb
    pad_end = jnp.cumsum(padded)
    pad_start = pad_end - padded
    eid = info_t[0:MOE_TOPK].astype(jnp.int32)
    rank = info_t[MOE_TOPK:2 * MOE_TOPK].astype(jnp.int32)
    experts = jnp.arange(MOE_EXPERTS, dtype=jnp.int32)[:, None, None]
    start = jnp.sum(jnp.where(eid[None] == experts, pad_start[:, None, None], 0), axis=0)
    dest_flat = (start + rank).reshape(-1)
    n_blocks = (t * MOE_TOPK + MOE_EXPERTS * (rb - 1)) // rb + 1
    n_used = (pad_end[-1] // rb).astype(jnp.int32)
    blk_row = jnp.minimum(jnp.arange(n_blocks, dtype=jnp.int32), n_used - 1) * rb
    block_expert = jnp.minimum(jnp.sum(pad_end[None, :] <= blk_row[:, None], axis=1),
                               MOE_EXPERTS - 1).astype(jnp.int32)
    block_valid = jnp.clip(counts[block_expert] - (blk_row - pad_start[block_expert]), 0, rb).astype(jnp.int32)
    ids = jnp.arange(MOE_EXPERTS, dtype=jnp.int32)
    later_used = (ids[None, :] > ids[:, None]) & (counts[None, :] > 0)
    next_used = jnp.min(jnp.where(later_used, ids[None, :], MOE_EXPERTS), axis=1)
    next_used = jnp.where(next_used < MOE_EXPERTS, next_used, -1).astype(jnp.int32)
    xs = _dispatch(dest_flat, u2, n_blocks * rb)
    ys = _experts(block_expert, next_used[block_expert], block_valid, n_used[None], xs, w_gate, w_up, w_down)
    return _combine(dest_flat, info, x1, norm_ffn_post[None, :], mod3, ys, seq)


def kernel(x, c, positions, w_ada, b_ada, norm_mix_pre, norm_mix_post, norm_ffn_pre, norm_ffn_post, w_in, conv_w, dn_a_log, dn_dt_bias, dn_out_norm, w_branch_moba, w_branch_delta, w_out, router_group_w, router_group_b, router_expert_w, router_expert_b, expert_w_gate, expert_w_up, expert_w_down):
    bsz, seq, d = x.shape
    depth = w_ada.shape[0]
    for layer in range(depth):
        mod = _ada(c, w_ada[layer], b_ada[layer][None, :])
        mod3 = mod.reshape(bsz, 1, -1)
        x1, u2, logits = _mixer_and_router(
            x, mod3, positions, norm_mix_pre[layer], norm_mix_post[layer], norm_ffn_pre[layer], w_in[layer],
            conv_w[layer], dn_a_log[layer], dn_dt_bias[layer], dn_out_norm[layer], w_branch_moba[layer],
            w_branch_delta[layer], w_out[layer], router_group_w[layer], router_group_b[layer],
            router_expert_w[layer], router_expert_b[layer])
        out = _moe(x1, u2, logits, mod3, norm_ffn_post[layer], expert_w_gate[layer], expert_w_up[layer],
                   expert_w_down[layer], seq)
        x = out.reshape(bsz, seq, d)
    return x
```

```python
import functools
import math

import jax
import jax.numpy as jnp
from jax import lax
from jax.experimental import pallas as pl
from jax.experimental.pallas import tpu as pltpu

F32 = jnp.float32
BF16 = jnp.bfloat16
HI = lax.Precision.HIGHEST

NORM_EPS = 1e-6
HEAD_DIM = 128
MOBA_HEADS = 8
MOBA_BLOCK = 256
MOBA_TOPK = 3
MOBA_HEADS_PER_STEP = 2
ROPE_THETA = 500000.0
ROPE_DIM = HEAD_DIM // 4
DN_HEADS = 8
DN_CONV_WIDTH = 4
DN_CHUNK = 64
DN_GROUP = 256
DN_HEADS_PER_STEP = 4
DN_CONV_PAD = 8
DN_CONV_ROWS = 128
MOE_GROUPS = 4
MOE_EXPERTS_PER_GROUP = 8
MOE_EXPERTS = MOE_GROUPS * MOE_EXPERTS_PER_GROUP
MOE_TOPK = 2
MOE_ROWS = 256
DMA_LOOP_UNROLL = 8
MERGE_SUB_ROWS = 128
LANES = 128
NEG = -1e30

VMEM_LIMIT = 60 * 1024 * 1024


def _cparams(*sem):
    return pltpu.CompilerParams(dimension_semantics=sem, vmem_limit_bytes=VMEM_LIMIT)


def _mm(a, b, precision=None):
    return jnp.dot(a, b, precision=precision, preferred_element_type=F32)


def _nt(a, b, precision=None):
    return lax.dot_general(a, b, (((1,), (1,)), ((), ())), precision=precision,
                           preferred_element_type=F32)


def _eye(n, dtype):
    r = lax.broadcasted_iota(jnp.int32, (n, n), 0)
    c = lax.broadcasted_iota(jnp.int32, (n, n), 1)
    return jnp.where(r == c, 1.0, 0.0).astype(dtype)


def _silu(x):
    return x * jax.nn.sigmoid(x)


def _softplus(x):
    return jnp.maximum(x, 0.0) + jnp.log1p(jnp.exp(-jnp.abs(x)))


def _pack_bf16_pair(lo, hi):
    def rne(x):
        b = pltpu.bitcast(x, jnp.uint32)
        return b + jnp.uint32(0x7FFF) + ((b >> 16) & jnp.uint32(1))
    return (rne(hi) & jnp.uint32(0xFFFF0000)) | (rne(lo) >> 16)


def _unpack_bf16_pair(p):
    return pltpu.bitcast(p << 16, F32), pltpu.bitcast(p & jnp.uint32(0xFFFF0000), F32)


def _ada_body(c_ref, w_ref, b_ref, o_ref):
    o_ref[...] = _mm(_silu(c_ref[...]), w_ref[...], HI) + b_ref[...]


def _ada(c, w, b):
    bsz, d = c.shape
    n = w.shape[1]
    tn = 1024
    return pl.pallas_call(
        _ada_body,
        grid=(n // tn,),
        in_specs=[pl.BlockSpec((bsz, d), lambda j: (0, 0)),
                  pl.BlockSpec((d, tn), lambda j: (0, j)),
                  pl.BlockSpec((1, tn), lambda j: (0, j))],
        out_specs=pl.BlockSpec((bsz, tn), lambda j: (0, j)),
        out_shape=jax.ShapeDtypeStruct((bsz, n), F32),
        compiler_params=_cparams("parallel"),
        name="ada",
    )(c, w, b)


def _prenorm_body(x_ref, g_ref, sc_ref, sh_ref, ws_ref, u_ref, os_ref):
    x = x_ref[...]
    y = x * lax.rsqrt(jnp.mean(x * x, axis=-1, keepdims=True) + NORM_EPS) * g_ref[...]
    u = (y * (1.0 + sc_ref[...]) + sh_ref[...]).astype(BF16)
    u_ref[...] = u
    os_ref[...] = _mm(u, ws_ref[...])


def _prenorm(x2, gain, mod3, w_small, seq, *, sc_chunk, sh_chunk):
    t, d = x2.shape
    tm = min(256, seq)
    per_b = seq // tm
    return pl.pallas_call(
        _prenorm_body,
        grid=(t // tm,),
        in_specs=[pl.BlockSpec((tm, d), lambda i: (i, 0)),
                  pl.BlockSpec((1, d), lambda i: (0, 0)),
                  pl.BlockSpec((None, 1, d), lambda i: (i // per_b, 0, sc_chunk)),
                  pl.BlockSpec((None, 1, d), lambda i: (i // per_b, 0, sh_chunk)),
                  pl.BlockSpec((d, LANES), lambda i: (0, 0))],
        out_specs=[pl.BlockSpec((tm, d), lambda i: (i, 0)),
                   pl.BlockSpec((tm, LANES), lambda i: (i, 0))],
        out_shape=[jax.ShapeDtypeStruct((t, d), BF16),
                   jax.ShapeDtypeStruct((t, LANES), F32)],
        compiler_params=_cparams("parallel"),
        name="prenorm",
    )(x2, gain, mod3, mod3, w_small)


def _inproj_body(u_ref, wt_ref, o_ref, stage, w_s, sem, *, tn, n_a, b_row0, n_tiles):
    j = pl.program_id(0)

    def fetch(tile):
        row0 = jnp.where(tile < n_a, tile * tn, b_row0 + (tile - n_a) * tn)
        return pltpu.make_async_copy(wt_ref.at[pl.ds(pl.multiple_of(row0, 8), tn), :], stage, sem)

    @pl.when(pl.program_id(1) == 0)
    def _():
        @pl.when(j == 0)
        def _():
            fetch(j).start()

        fetch(j).wait()
        w_s[...] = stage[...].astype(BF16)

        @pl.when(j + 1 < n_tiles)
        def _():
            fetch(j + 1).start()

    o_ref[...] = _nt(u_ref[...], w_s[...]).astype(o_ref.dtype)


def _inproj(u, w_t, n_a, b_row0, n_b, *, out_first_b):
    t, d = u.shape
    tm = min(1024, t)
    tn = 1024
    assert b_row0 % 8 == 0 and b_row0 + n_b * tn <= w_t.shape[0]

    def out_col(j):
        if not out_first_b:
            return j
        return jnp.where(j < n_a, j + n_b, j - n_a)

    return pl.pallas_call(
        functools.partial(_inproj_body, tn=tn, n_a=n_a, b_row0=b_row0, n_tiles=n_a + n_b),
        grid=(n_a + n_b, t // tm),
        in_specs=[pl.BlockSpec((tm, d), lambda j, i: (i, 0)),
                  pl.BlockSpec(memory_space=pl.ANY)],
        out_specs=pl.BlockSpec((tm, tn), lambda j, i: (i, out_col(j))),
        out_shape=jax.ShapeDtypeStruct((t, (n_a + n_b) * tn), BF16),
        scratch_shapes=[pltpu.VMEM((tn, d), F32), pltpu.VMEM((tn, d), BF16), pltpu.SemaphoreType.DMA(())],
        compiler_params=_cparams("arbitrary", "arbitrary"),
        name="inproj",
    )(u, w_t)


def _moba_body(q_ref, k_ref, v_ref, cos_ref, sin_ref, o_ref, kr_s, vt_s, km_s, *, nblk, hp):
    blk = MOBA_BLOCK
    half = ROPE_DIM // 2
    lane = lax.broadcasted_iota(jnp.int32, (blk, HEAD_DIM), 1)
    eye_d = _eye(HEAD_DIM, BF16)
    eye_b = _eye(blk, BF16)
    scale = HEAD_DIM ** -0.5

    def rope(xf, rows):
        partner = jnp.where(lane < half, pltpu.roll(xf, HEAD_DIM - half, 1), pltpu.roll(xf, half, 1))
        return xf * cos_ref[rows, :] + partner * sin_ref[rows, :]

    def cols(h):
        return slice(h * HEAD_DIM, (h + 1) * HEAD_DIM)

    km_s[...] = jnp.zeros_like(km_s)
    for j in range(nblk):
        rows = slice(j * blk, (j + 1) * blk)
        for h in range(hp):
            kc = rope(k_ref[rows, cols(h)].astype(F32), rows)
            kr_s[rows, cols(h)] = kc.astype(BF16)
            km_s[j:j + 1, cols(h)] = jnp.mean(kc, axis=0, keepdims=True)
            vt_s[cols(h), rows] = _nt(eye_d, v_ref[rows, cols(h)]).astype(BF16)

    key_i = lax.broadcasted_iota(jnp.int32, (blk, blk), 0)
    qry_i = lax.broadcasted_iota(jnp.int32, (blk, blk), 1)
    blk_i = lax.broadcasted_iota(jnp.int32, (8, blk), 0)
    causal_bias = jnp.where(key_i <= qry_i, 0.0, NEG)

    def rows(j):
        return slice(j * blk, (j + 1) * blk)

    def begin(i, h):
        qc = rope(q_ref[rows(i), cols(h)].astype(F32), rows(i))
        st = dict(i=i, h=h, qs=(qc * scale).astype(BF16), scores=[], m=None, bias=None)
        if i > 0:
            g_t = _nt(km_s[:, cols(h)], qc, HI)
            rank = jnp.zeros((8, blk), F32)
            for jp in range(i):
                row = g_t[jp:jp + 1, :]
                beats = (row > g_t) | ((row == g_t) & (jp < blk_i))
                rank = rank + jnp.where(beats, 1.0, 0.0)
            sel = (rank < float(MOBA_TOPK)) & (blk_i < i)
            st["bias"] = jnp.where(sel, 0.0, NEG)
        return st

    def score(st, j):
        i = st["i"]
        s = _nt(kr_s[rows(j), cols(st["h"])], st["qs"])
        if j == i:
            s = s + causal_bias
        st["scores"].append(s)
        mj = jnp.max(s, axis=0, keepdims=True)
        if j < i:
            mj = mj + st["bias"][j:j + 1, :]
        st["m"] = mj if st["m"] is None else jnp.maximum(st["m"], mj)

    def accumulate(st, j):
        i = st["i"]
        if j == 0:
            st["den"] = jnp.zeros((1, blk), F32)
            st["acc"] = jnp.zeros((HEAD_DIM, blk), F32)
        shift = (st["bias"][j:j + 1, :] - st["m"]) if j < i else -st["m"]
        p = jnp.exp(st["scores"][j] + shift)
        st["den"] = st["den"] + jnp.sum(p, axis=0, keepdims=True)
        st["acc"] = st["acc"] + _mm(vt_s[cols(st["h"]), rows(j)], p.astype(BF16))
        if j == i:
            o_t = (st["acc"] / st["den"]).astype(BF16)
            o_ref[rows(i), cols(st["h"])] = _nt(eye_b, o_t).astype(o_ref.dtype)

    prev = None
    for i in range(nblk):
        cur = [begin(i, h) for h in range(hp)]
        for j in range(i + 1):
            for h in range(hp):
                score(cur[h], j)
                if prev is not None and j <= i - 1:
                    accumulate(prev[h], j)
        prev = cur
    for j in range(nblk):
        for h in range(hp):
            accumulate(prev[h], j)


def _moba(proj, cos_t, sin_t, *, q_blk0, k_blk0, v_blk0):
    bsz, seq, _ = proj.shape
    nblk = seq // MOBA_BLOCK
    assert seq % MOBA_BLOCK == 0 and 1 <= nblk <= 8
    hd = HEAD_DIM
    hp = MOBA_HEADS_PER_STEP
    wide = hp * hd
    assert MOBA_HEADS % hp == 0 and all(b0 % hp == 0 for b0 in (q_blk0, k_blk0, v_blk0))
    return pl.pallas_call(
        functools.partial(_moba_body, nblk=nblk, hp=hp),
        grid=(bsz, MOBA_HEADS // hp),
        in_specs=[pl.BlockSpec((None, seq, wide), lambda b, h: (b, 0, q_blk0 // hp + h)),
                  pl.BlockSpec((None, seq, wide), lambda b, h: (b, 0, k_blk0 // hp + h)),
                  pl.BlockSpec((None, seq, wide), lambda b, h: (b, 0, v_blk0 // hp + h)),
                  pl.BlockSpec((None, seq, hd), lambda b, h: (b, 0, 0)),
                  pl.BlockSpec((None, seq, hd), lambda b, h: (b, 0, 0))],
        out_specs=pl.BlockSpec((None, seq, wide), lambda b, h: (b, 0, h)),
        out_shape=jax.ShapeDtypeStruct((bsz, seq, MOBA_HEADS * hd), BF16),
        scratch_shapes=[pltpu.VMEM((seq, wide), BF16),
                        pltpu.VMEM((wide, seq), BF16),
                        pltpu.VMEM((8, wide), F32)],
        compiler_params=_cparams("parallel", "parallel"),
        name="moba",
    )(proj, proj, proj, cos_t, sin_t)


def _chunk_masks(n, chunk):
    r = lax.broadcasted_iota(jnp.int32, (n, n), 0)
    c = lax.broadcasted_iota(jnp.int32, (n, n), 1)
    shift = int(math.log2(chunk))
    same = jnp.right_shift(r, shift) == jnp.right_shift(c, shift)
    return r, c, same


def _gates_body(ba_ref, par_ref, beta_ref, g_ref, gl_ref, *, seq):
    grp = DN_GROUP
    r, c, same = _chunk_masks(grp, DN_CHUNK)
    low = jnp.where(same & (c <= r), 1.0, 0.0)
    ones = jnp.where(same, 1.0, 0.0)
    neg_a = -jnp.exp(par_ref[0:1, :])
    dt_b = par_ref[1:2, :]
    for i in range(seq // grp):
        rows = slice(i * grp, (i + 1) * grp)
        x = ba_ref[rows, :].astype(F32)
        beta_ref[rows, :] = jax.nn.sigmoid(x)
        g = neg_a * _softplus(x + dt_b)
        g_ref[rows, :] = _mm(low, g, HI)
        gl_ref[rows, :] = _mm(ones, g, HI)


def _gates(ba, par):
    bsz, seq, _ = ba.shape
    spec = pl.BlockSpec((None, seq, LANES), lambda b: (b, 0, 0))
    return pl.pallas_call(
        functools.partial(_gates_body, seq=seq),
        grid=(bsz,),
        in_specs=[spec, pl.BlockSpec((8, LANES), lambda b: (0, 0))],
        out_specs=[spec, spec, spec],
        out_shape=[jax.ShapeDtypeStruct((bsz, seq, LANES), F32)] * 3,
        compiler_params=_cparams("parallel"),
        name="dn_gates",
    )(ba, par)


def _dn_body(q_ref, k_ref, v_ref, z_ref, cwq_ref, cwk_ref, cwv_ref, beta_ref, g_ref, gl_ref, grow_ref,
             gain_ref, o_ref, stage_s, qkv_a, qkv_b, state_s, vnew_s, *bufs, seq, hp):
    bufs_a, bufs_b = bufs[:5], bufs[5:]
    hg = pl.program_id(1)
    grp = DN_GROUP
    chunk = DN_CHUNK
    ngrp = seq // grp
    hd = HEAD_DIM
    pad = DN_CONV_PAD

    crows = DN_CONV_ROWS
    r, c, same = _chunk_masks(grp, chunk)
    low_incl = same & (c <= r)
    low_strict = same & (c < r)
    eye_g = jnp.where(r == c, 1.0, 0.0)
    eye_d = _eye(hd, BF16)
    lane = lax.broadcasted_iota(jnp.int32, (grp, LANES), 1)
    col_chunk = jnp.right_shift(lax.broadcasted_iota(jnp.int32, (hd, grp), 1), int(math.log2(chunk)))
    gain = gain_ref[...]
    n_double = int(math.log2(chunk)) - 1

    def pick(ref, r0, lane_id):
        return jnp.sum(jnp.where(lane == lane_id, ref[pl.ds(r0, grp), :], 0.0), axis=-1, keepdims=True)

    state_s[...] = jnp.zeros_like(state_s)
    heads = range(hp)

    cols = [slice(hh * hd, (hh + 1) * hd) for hh in heads]
    head = [hg * hp + hh for hh in heads]

    tensors = ((q_ref, cwq_ref, True, hd ** -0.5), (k_ref, cwk_ref, True, None), (v_ref, cwv_ref, False, None))
    nsub = grp // crows

    def conv(gi, qkv, first=False):
        r0 = pl.multiple_of(gi * grp, grp)
        for ti, (src, cw_ref, l2, post) in enumerate(tensors):
            cw = cw_ref[...]
            for sb in range(nsub):
                stage_s[ti, sb, pad:pad + crows, :] = src[pl.ds(r0 + sb * crows, crows), :].astype(F32)
            if first:
                stage_s[ti, 0, 0:pad, :] = jnp.zeros((pad, hp * hd), F32)
            else:
                before = src[pl.ds(pl.multiple_of(r0 - 2 * pad, 2 * pad), 2 * pad), :].astype(F32)
                stage_s[ti, 0, 0:pad, :] = before[pad:, :]
            for sb in range(1, nsub):
                stage_s[ti, sb, 0:pad, :] = stage_s[ti, sb - 1, crows:crows + pad, :]
            for sb in range(nsub):
                for hh in heads:
                    acc = None
                    for j in range(DN_CONV_WIDTH):
                        off = pad - (DN_CONV_WIDTH - 1) + j
                        term = stage_s[ti, sb, off:off + crows, cols[hh]] * cw[j:j + 1, cols[hh]]
                        acc = term if acc is None else acc + term
                    yh = _silu(acc)
                    if l2:
                        yh = yh * lax.rsqrt(jnp.sum(yh * yh, axis=-1, keepdims=True) + NORM_EPS)
                    if post is not None:
                        yh = yh * post
                    qkv[ti, sb * crows:(sb + 1) * crows, cols[hh]] = yh
                yield

    def prepare(gi, qkv, bufs):
        u_r, w_r, qk_r, qd_r, kt_r = bufs
        r0 = pl.multiple_of(gi * grp, grp)
        q = [qkv[0, :, cols[hh]] for hh in heads]
        k = [qkv[1, :, cols[hh]] for hh in heads]
        v = [qkv[2, :, cols[hh]] for hh in heads]
        beta = [pick(beta_ref, r0, head[hh]) for hh in heads]
        g_col = [pick(g_ref, r0, DN_HEADS + head[hh]) for hh in heads]
        gl_col = [pick(gl_ref, r0, DN_HEADS + head[hh]) for hh in heads]
        decay = [jnp.exp(jnp.minimum(g_col[hh] - grow_ref[hh, gi], 0.0)) for hh in heads]
        e_g = [jnp.exp(g_col[hh]) for hh in heads]
        kb = [k[hh] * beta[hh] for hh in heads]
        vb = [(v[hh] * beta[hh]).astype(BF16) for hh in heads]
        k16 = [k[hh].astype(BF16) for hh in heads]
        n_mat = [jnp.where(low_strict, _nt(kb[hh].astype(BF16), k16[hh]) * decay[hh], 0.0) for hh in heads]
        p = [(-n_mat[hh]).astype(BF16) for hh in heads]
        x = [eye_g - n_mat[hh] for hh in heads]
        qk = [jnp.where(low_incl, _nt(q[hh].astype(BF16), k16[hh]) * decay[hh], 0.0).astype(BF16) for hh in heads]
        for hh in heads:
            qk_r[hh] = qk[hh]
            qd_r[hh] = (q[hh] * e_g[hh]).astype(BF16)
            kt_r[hh] = _nt(eye_d, (k[hh] * jnp.exp(gl_col[hh] - g_col[hh])).astype(BF16)).astype(BF16)
        yield
        for _ in range(n_double):
            p2 = [_mm(p[hh], p[hh]).astype(BF16) for hh in heads]
            x = [x[hh] + _mm(x[hh].astype(BF16), p2[hh]) for hh in heads]
            p = p2
            yield
        x16 = [x[hh].astype(BF16) for hh in heads]
        for hh in heads:
            u_r[hh] = _mm(x16[hh], vb[hh])
            w_r[hh] = _mm(x16[hh], (kb[hh] * e_g[hh]).astype(BF16)).astype(BF16)
        yield

    def scan(gi, bufs):
        u_r, w_r, qk_r, qd_r, kt_r = bufs
        r0 = pl.multiple_of(gi * grp, grp)
        vnew_s[...] = jnp.zeros_like(vnew_s)
        for ci in range(grp // chunk):
            rows = slice(ci * chunk, (ci + 1) * chunk)
            gl_row = jnp.exp(gl_ref[pl.ds(r0 + ci * chunk, 1), :])
            for hh in heads:
                state = state_s[hh]
                s16 = state.astype(BF16)
                v_new = u_r[hh, rows, :] - _mm(w_r[hh, rows, :], s16)
                vnew_s[hh, rows, :] = v_new.astype(BF16)
                o = _mm(qd_r[hh, rows, :], s16) + _mm(qk_r[hh, rows, :], vnew_s[hh])
                chunk_decay = jnp.sum(jnp.where(lane[0:1, :] == DN_HEADS + head[hh], gl_row, 0.0),
                                      axis=-1, keepdims=True)
                k_tail_t = kt_r[hh]
                kt = jnp.where(col_chunk == ci, k_tail_t, jnp.zeros_like(k_tail_t))
                state_s[hh] = state * chunk_decay + _mm(kt, vnew_s[hh])
                on = o * lax.rsqrt(jnp.mean(o * o, axis=-1, keepdims=True) + NORM_EPS) * gain
                zz = z_ref[pl.ds(r0 + ci * chunk, chunk), cols[hh]].astype(F32)
                o_ref[pl.ds(r0 + ci * chunk, chunk), cols[hh]] = (on * _silu(zz)).astype(o_ref.dtype)
            yield

    def interleave(first, *others):
        for _ in first:
            for steps in others:
                next(steps, None)
                next(steps, None)
        for steps in others:
            for _ in steps:
                pass

    interleave(conv(0, qkv_a, first=True))
    interleave(prepare(0, qkv_a, bufs_a), conv(1, qkv_b))

    def pair(pi, carry):
        g0 = 2 * pi
        interleave(scan(g0, bufs_a), prepare(g0 + 1, qkv_b, bufs_b), conv(g0 + 2, qkv_a))
        interleave(scan(g0 + 1, bufs_b), prepare(g0 + 2, qkv_a, bufs_a), conv(g0 + 3, qkv_b))
        return carry

    lax.fori_loop(0, ngrp // 2 - 1, pair, 0)
    interleave(scan(ngrp - 2, bufs_a), prepare(ngrp - 1, qkv_b, bufs_b))
    interleave(scan(ngrp - 1, bufs_b))


def _deltanet(proj, conv_w, beta, gcum, glast, grow, gain, *, q_blk0, k_blk0, v_blk0, z_blk0):
    bsz, seq, _ = proj.shape
    hd = HEAD_DIM
    nh = DN_HEADS
    assert seq % DN_GROUP == 0
    ngrp = seq // DN_GROUP

    hp = DN_HEADS_PER_STEP
    wide = hp * hd
    assert nh % hp == 0 and all(b0 % hp == 0 for b0 in (q_blk0, k_blk0, v_blk0, z_blk0))

    def col(blk0):
        return pl.BlockSpec((None, seq, wide), lambda b, h: (b, 0, blk0 // hp + h))

    def cw(blk0):
        return pl.BlockSpec((DN_CONV_WIDTH, wide), lambda b, h: (0, blk0 // hp + h))

    full = pl.BlockSpec((None, seq, LANES), lambda b, h: (b, 0, 0))
    return pl.pallas_call(
        functools.partial(_dn_body, seq=seq, hp=hp),
        grid=(bsz, nh // hp),
        in_specs=[col(q_blk0), col(k_blk0), col(v_blk0), col(z_blk0),
                  cw(0), cw(nh), cw(2 * nh),
                  full, full, full,
                  pl.BlockSpec((hp, ngrp, 1, DN_GROUP), lambda b, h: (b * (nh // hp) + h, 0, 0, 0)),
                  pl.BlockSpec((1, hd), lambda b, h: (0, 0))],
        out_specs=pl.BlockSpec((None, seq, wide), lambda b, h: (b, 0, h)),
        out_shape=jax.ShapeDtypeStruct((bsz, seq, nh * hd), BF16),
        scratch_shapes=[pltpu.VMEM((3, DN_GROUP // DN_CONV_ROWS, DN_CONV_ROWS + DN_CONV_PAD, wide), F32)]
        + [pltpu.VMEM((3, DN_GROUP, wide), F32)] * 2
        + [pltpu.VMEM((hp, hd, hd), F32), pltpu.VMEM((hp, DN_GROUP, hd), BF16)]
        + [pltpu.VMEM((hp, DN_GROUP, hd), F32), pltpu.VMEM((hp, DN_GROUP, hd), BF16),
           pltpu.VMEM((hp, DN_GROUP, DN_GROUP), BF16), pltpu.VMEM((hp, DN_GROUP, hd), BF16),
           pltpu.VMEM((hp, hd, DN_GROUP), BF16)] * 2,
        compiler_params=_cparams("parallel", "parallel"),
        name="deltanet",
    )(proj, proj, proj, proj, conv_w, conv_w, conv_w, beta, gcum, glast, grow, gain)


def _merge_body(ya_ref, yb_ref, ga_ref, gb_ref, x_ref, wm_ref, wd_ref, wo_ref, wr_ref, br_ref,
                npost_ref, npre_ref, gt_ref, sc_ref, sh_ref, x1_ref, u2_ref, lg_ref):
    sub = MERGE_SUB_ROWS
    for s in range(x_ref.shape[0] // sub):
        rows = slice(s * sub, (s + 1) * sub)
        ma = _mm(ya_ref[rows, :], wm_ref[...])
        mb = _mm(yb_ref[rows, :], wd_ref[...])
        merged = (jax.nn.sigmoid(ga_ref[rows, :].astype(F32)) * ma
                  + jax.nn.sigmoid(gb_ref[rows, :].astype(F32)) * mb)
        y = _mm(merged.astype(BF16), wo_ref[...])
        yn = y * lax.rsqrt(jnp.mean(y * y, axis=-1, keepdims=True) + NORM_EPS) * npost_ref[...]
        x1 = x_ref[rows, :] + gt_ref[...] * yn
        x1_ref[rows, :] = x1
        un = x1 * lax.rsqrt(jnp.mean(x1 * x1, axis=-1, keepdims=True) + NORM_EPS) * npre_ref[...]
        u2 = un * (1.0 + sc_ref[...]) + sh_ref[...]
        half = u2.shape[1] // 2
        u2_ref[rows, :] = _pack_bf16_pair(u2[:, :half], u2[:, half:])
        u_hi = u2.astype(BF16)
        u_lo = (u2 - u_hi.astype(F32)).astype(BF16)
        lg2 = _nt(wr_ref[...], u_hi)
        lg_ref[:, rows] = lg2[:LANES, :] + lg2[LANES:, :] + _nt(wr_ref[:LANES, :], u_lo) + br_ref[...]


def _merge(ya, yb, proj2, x2, wm, wd, wo, wr, br, npost, npre, mod3, seq, *, ga_blk, gb_blk):
    t, d = x2.shape
    wa = ya.shape[1]
    tm = min(256, seq)
    per_b = seq // tm
    const = lambda i: (0, 0)
    once = dict(pipeline_mode=pl.Buffered(1))

    def modspec(chunk):
        return pl.BlockSpec((None, 1, d), lambda i: (i // per_b, 0, chunk))

    return pl.pallas_call(
        _merge_body,
        grid=(t // tm,),
        in_specs=[pl.BlockSpec((tm, wa), lambda i: (i, 0)),
                  pl.BlockSpec((tm, wa), lambda i: (i, 0)),
                  pl.BlockSpec((tm, d), lambda i: (i, ga_blk)),
                  pl.BlockSpec((tm, d), lambda i: (i, gb_blk)),
                  pl.BlockSpec((tm, d), lambda i: (i, 0)),
                  pl.BlockSpec((wa, d), const, **once),
                  pl.BlockSpec((wa, d), const, **once),
                  pl.BlockSpec((d, d), const, **once),
                  pl.BlockSpec((2 * LANES, d), const, **once),
                  pl.BlockSpec((LANES, 1), const),
                  pl.BlockSpec((1, d), const),
                  pl.BlockSpec((1, d), const),
                  modspec(2), modspec(4), modspec(3)],
        out_specs=[pl.BlockSpec((tm, d), lambda i: (i, 0)),
                   pl.BlockSpec((tm, d // 2), lambda i: (i, 0)),
                   pl.BlockSpec((LANES, tm), lambda i: (0, i))],
        out_shape=[jax.ShapeDtypeStruct((t, d), F32),
                   jax.ShapeDtypeStruct((t, d // 2), jnp.uint32),
                   jax.ShapeDtypeStruct((LANES, t), F32)],
        compiler_params=_cparams("parallel"),
        name="merge",
    )(ya, yb, proj2, proj2, x2, wm, wd, wo, wr, br, npost, npre, mod3, mod3, mod3)


def _route_body(lg_ref, info_ref, col_ref, cnt_ref, run_s, *, tr):
    @pl.when(pl.program_id(0) == 0)
    def _():
        run_s[...] = jnp.zeros_like(run_s)

    lg = lg_ref[...]
    row = lax.broadcasted_iota(jnp.int32, (LANES, tr), 0)
    row_f = row.astype(F32)
    big = float(LANES)

    def first_max(vals, mask):
        mx = jnp.max(jnp.where(mask, vals, NEG), axis=0, keepdims=True)
        idx = jnp.min(jnp.where(mask & (vals == mx), row_f, big), axis=0, keepdims=True)
        return mx, idx

    gmask = row < MOE_GROUPS
    gmax, gidx = first_max(lg, gmask)
    p_group = 1.0 / jnp.sum(jnp.where(gmask, jnp.exp(lg - gmax), 0.0), axis=0, keepdims=True)
    lo = float(MOE_GROUPS) + gidx * float(MOE_EXPERTS_PER_GROUP)
    emask = (row_f >= lo) & (row_f < lo + float(MOE_EXPERTS_PER_GROUP))
    m1, i1 = first_max(lg, emask)
    m2, i2 = first_max(lg, emask & (row_f != i1))
    e2 = jnp.exp(m2 - m1)
    w1 = p_group / (1.0 + e2)
    w2 = p_group * e2 / (1.0 + e2)
    oh1 = row_f == i1
    oh2 = row_f == i2
    oh = jnp.where(oh1 | oh2, 1.0, 0.0).astype(BF16)
    r = lax.broadcasted_iota(jnp.int32, (tr, tr), 0)
    c = lax.broadcasted_iota(jnp.int32, (tr, tr), 1)
    before = jnp.where(r < c, 1.0, 0.0).astype(BF16)
    prefix = _mm(oh, before) + run_s[:, 0:1]
    rank1 = jnp.sum(jnp.where(oh1, prefix, 0.0), axis=0, keepdims=True)
    rank2 = jnp.sum(jnp.where(oh2, prefix, 0.0), axis=0, keepdims=True)
    run_s[...] = run_s[...] + jnp.sum(oh.astype(F32), axis=1, keepdims=True)
    goff = float(MOE_GROUPS)
    info = jnp.where(row == 0, i1 - goff, 0.0)
    info = jnp.where(row == 1, i2 - goff, info)
    info = jnp.where(row == 2, rank1, info)
    info = jnp.where(row == 3, rank2, info)
    info = jnp.where(row == 4, w1, info)
    info = jnp.where(row == 5, w2, info)
    info_ref[...] = info[0:8, :]
    col_ref[...] = _nt(_eye(tr, F32), info, HI)
    cnt_ref[...] = run_s[...]


def _route(logits_t):
    t = logits_t.shape[1]
    tr = min(256, t)
    return pl.pallas_call(
        functools.partial(_route_body, tr=tr),
        grid=(t // tr,),
        in_specs=[pl.BlockSpec((LANES, tr), lambda i: (0, i))],
        out_specs=[pl.BlockSpec((8, tr), lambda i: (0, i)),
                   pl.BlockSpec((tr, LANES), lambda i: (i, 0)),
                   pl.BlockSpec((LANES, LANES), lambda i: (0, 0))],
        out_shape=[jax.ShapeDtypeStruct((8, t), F32),
                   jax.ShapeDtypeStruct((t, LANES), F32),
                   jax.ShapeDtypeStruct((LANES, LANES), F32)],
        scratch_shapes=[pltpu.VMEM((LANES, LANES), F32)],
        compiler_params=_cparams("arbitrary"),
        name="route",
    )(logits_t)


def _dispatch_body(d0_ref, d1_ref, u_ref, xs_ref, sem, *, td):
    dests = (d0_ref, d1_ref)

    def row_copy(r, k):
        return pltpu.make_async_copy(u_ref.at[pl.ds(r, 1), :], xs_ref.at[pl.ds(dests[k][r], 1), :], sem)

    def start(r, carry):
        for k in range(MOE_TOPK):
            row_copy(r, k).start()
        return carry

    def wait(r, carry):
        for k in range(MOE_TOPK):
            row_copy(r, k).wait()
        return carry

    lax.fori_loop(0, td, start, 0, unroll=DMA_LOOP_UNROLL)
    lax.fori_loop(0, td, wait, 0, unroll=DMA_LOOP_UNROLL)


def _dispatch(dest, u2, n_rows):
    t, d = u2.shape
    td = min(1024, t)
    nt = t // td
    return pl.pallas_call(
        functools.partial(_dispatch_body, td=td),
        grid=(nt,),
        in_specs=[pl.BlockSpec((td,), lambda i: (i,), memory_space=pltpu.SMEM),
                  pl.BlockSpec((td,), lambda i: (nt + i,), memory_space=pltpu.SMEM),
                  pl.BlockSpec((td, d), lambda i: (i, 0))],
        out_specs=pl.BlockSpec(memory_space=pl.ANY),
        out_shape=jax.ShapeDtypeStruct((n_rows, d), u2.dtype),
        scratch_shapes=[pltpu.SemaphoreType.DMA(())],
        compiler_params=_cparams("arbitrary"),
        name="dispatch",
    )(dest, dest, u2)


def _experts_body(be_ref, nx_ref, nv_ref, nu_ref, x_ref, wg_ref, wu_ref, wd_ref, o_ref,
                  stage_g, stage_u, stage_d, wg_s, wu_s, wd_s, sems):
    i = pl.program_id(0)

    def fetch(e):
        return (pltpu.make_async_copy(wg_ref.at[e], stage_g, sems.at[0]),
                pltpu.make_async_copy(wu_ref.at[e], stage_u, sems.at[1]),
                pltpu.make_async_copy(wd_ref.at[e], stage_d, sems.at[2]))

    @pl.when(i == 0)
    def _():
        for cp in fetch(be_ref[0]):
            cp.start()

    @pl.when(jnp.logical_or(i == 0, be_ref[i] != be_ref[jnp.maximum(i - 1, 0)]))
    def _():
        for cp in fetch(be_ref[i]):
            cp.wait()
        wg_s[...] = stage_g[...].astype(BF16)
        wu_s[...] = stage_u[...].astype(BF16)
        wd_s[...] = stage_d[...].astype(BF16)

        @pl.when(nx_ref[i] >= 0)
        def _():
            for cp in fetch(nx_ref[i]):
                cp.start()

    @pl.when(i < nu_ref[0])
    def _():
        row = lax.broadcasted_iota(jnp.int32, (x_ref.shape[0], 1), 0)
        packed = jnp.where(row < nv_ref[i], x_ref[...], jnp.uint32(0))
        x_lo, x_hi = (v.astype(BF16) for v in _unpack_bf16_pair(packed))
        half = packed.shape[1]
        gate = _mm(x_lo, wg_s[0:half, :]) + _mm(x_hi, wg_s[half:, :])
        up = _mm(x_lo, wu_s[0:half, :]) + _mm(x_hi, wu_s[half:, :])
        y = _mm((_silu(gate) * up).astype(BF16), wd_s[...])
        o_ref[...] = _pack_bf16_pair(y[:, :half], y[:, half:])

    @pl.when(i >= nu_ref[0])
    def _():
        o_ref[...] = jnp.zeros_like(o_ref)


def _experts(block_expert, next_expert, block_valid, n_used, xs, wg, wu, wd):
    nr, dp = xs.shape
    d, ff = wg.shape[1], wg.shape[2]
    assert d == 2 * dp
    rb = MOE_ROWS
    row_map = lambda i, be, nx, nv, nu: (jnp.minimum(i, nu[0] - 1), 0)
    hbm = pl.BlockSpec(memory_space=pl.ANY)
    grid_spec = pltpu.PrefetchScalarGridSpec(
        num_scalar_prefetch=4,
        grid=(nr // rb,),
        in_specs=[pl.BlockSpec((rb, dp), row_map), hbm, hbm, hbm],
        out_specs=pl.BlockSpec((rb, dp), lambda i, be, nx, nv, nu: (i, 0)),
        scratch_shapes=[pltpu.VMEM((d, ff), F32), pltpu.VMEM((d, ff), F32), pltpu.VMEM((ff, d), F32),
                        pltpu.VMEM((d, ff), BF16), pltpu.VMEM((d, ff), BF16), pltpu.VMEM((ff, d), BF16),
                        pltpu.SemaphoreType.DMA((3,))],
    )
    return pl.pallas_call(
        _experts_body,
        grid_spec=grid_spec,
        out_shape=jax.ShapeDtypeStruct((nr, dp), jnp.uint32),
        compiler_params=_cparams("arbitrary"),
        name="experts",
    )(block_expert, next_expert, block_valid, n_used, xs, wg, wu, wd)


def _combine_body(d0_ref, d1_ref, info_ref, x1_ref, npost_ref, gt_ref, yb_ref, o_ref, buf, sem, *, tc, rc):
    dests = (d0_ref, d1_ref)

    def row_copy(r, k):
        return pltpu.make_async_copy(yb_ref.at[pl.ds(dests[k][r], 1), :], buf.at[k, pl.ds(r, 1), :], sem)

    def start(r, carry):
        for k in range(MOE_TOPK):
            row_copy(r, k).start()
        return carry

    def wait(r, carry):
        for k in range(MOE_TOPK):
            row_copy(r, k).wait()
        return carry

    lax.fori_loop(0, tc, start, 0, unroll=DMA_LOOP_UNROLL)
    lax.fori_loop(0, tc, wait, 0, unroll=DMA_LOOP_UNROLL)

    half = buf.shape[2]
    gain_lo, gain_hi = npost_ref[:, :half], npost_ref[:, half:]
    gate_lo, gate_hi = gt_ref[:, :half], gt_ref[:, half:]

    def chunk(i, carry):
        r0 = pl.multiple_of(i * rc, rc)
        info = info_ref[pl.ds(r0, rc), :]
        w0, w1 = info[:, 4:5], info[:, 5:6]
        a_lo, a_hi = _unpack_bf16_pair(buf[0, pl.ds(r0, rc), :])
        b_lo, b_hi = _unpack_bf16_pair(buf[1, pl.ds(r0, rc), :])
        y_lo = w0 * a_lo + w1 * b_lo
        y_hi = w0 * a_hi + w1 * b_hi
        ms = (jnp.sum(y_lo * y_lo, axis=-1, keepdims=True)
              + jnp.sum(y_hi * y_hi, axis=-1, keepdims=True)) * (1.0 / (2 * half))
        inv = lax.rsqrt(ms + NORM_EPS)
        o_ref[pl.ds(r0, rc), :half] = x1_ref[pl.ds(r0, rc), :half] + gate_lo * (y_lo * inv * gain_lo)
        o_ref[pl.ds(r0, rc), half:] = x1_ref[pl.ds(r0, rc), half:] + gate_hi * (y_hi * inv * gain_hi)
        return carry

    lax.fori_loop(0, tc // rc, chunk, 0)


def _combine(dest, info, x1, npost, mod3, yb, seq):
    t, d = x1.shape
    tc = min(1024, seq)
    rc = min(128, tc)
    per_b = seq // tc
    nt = t // tc
    return pl.pallas_call(
        functools.partial(_combine_body, tc=tc, rc=rc),
        grid=(nt,),
        in_specs=[pl.BlockSpec((tc,), lambda i: (i,), memory_space=pltpu.SMEM),
                  pl.BlockSpec((tc,), lambda i: (nt + i,), memory_space=pltpu.SMEM),
                  pl.BlockSpec((tc, LANES), lambda i: (i, 0)),
                  pl.BlockSpec((tc, d), lambda i: (i, 0)),
                  pl.BlockSpec((1, d), lambda i: (0, 0)),
                  pl.BlockSpec((None, 1, d), lambda i: (i // per_b, 0, 5)),
                  pl.BlockSpec(memory_space=pl.ANY)],
        out_specs=pl.BlockSpec((tc, d), lambda i: (i, 0)),
        out_shape=jax.ShapeDtypeStruct((t, d), F32),
        scratch_shapes=[pltpu.VMEM((MOE_TOPK, tc, d // 2), jnp.uint32), pltpu.SemaphoreType.DMA(())],
        compiler_params=_cparams("arbitrary"),
        name="combine",
    )(dest, dest, info, x1, npost, mod3, yb)


def _rope_tables(positions):
    half = ROPE_DIM // 2
    inv_freq = jnp.power(ROPE_THETA, -jnp.arange(half, dtype=F32) * (2.0 / ROPE_DIM))
    ang = positions.astype(F32)[..., None] * inv_freq
    cos, sin = jnp.cos(ang), jnp.sin(ang)
    rest = HEAD_DIM - ROPE_DIM
    cos_t = jnp.concatenate([cos, cos, jnp.ones(cos.shape[:-1] + (rest,), F32)], axis=-1)
    sin_t = jnp.concatenate([-sin, sin, jnp.zeros(sin.shape[:-1] + (rest,), F32)], axis=-1)
    return cos_t, sin_t


def _pad_lanes(v, n=LANES):
    return jnp.pad(v, [(0, 0)] * (v.ndim - 1) + [(0, n - v.shape[-1])])


def _mixer_and_router(x, mod3, positions, norm_mix_pre, norm_mix_post, norm_ffn_pre, w_in, conv_w,
                      dn_a_log, dn_dt_bias, dn_out_norm, w_branch_moba, w_branch_delta, w_out,
                      router_group_w, router_group_b, router_expert_w, router_expert_b):
    bsz, seq, d = x.shape
    t = bsz * seq
    mw = MOBA_HEADS * HEAD_DIM
    dw = DN_HEADS * HEAD_DIM
    o_qa, o_ka, o_va = 0, mw, 2 * mw
    o_dn = 3 * mw
    o_z = o_dn + 3 * dw
    o_ba = o_z + dw
    o_ga = o_ba + 2 * DN_HEADS
    o_gb = o_ga + d
    w_small = _pad_lanes(w_in[:, o_ba:o_ga]).astype(BF16)
    x2 = x.reshape(t, d)
    u, ba2 = _prenorm(x2, norm_mix_pre[None, :], mod3, w_small, seq, sc_chunk=1, sh_chunk=0)
    assert o_ba % 1024 == 0 and (2 * d) % 1024 == 0
    proj2 = _inproj(u, w_in.T, o_ba // 1024, o_ga, 2 * d // 1024, out_first_b=True)
    proj = proj2.reshape(bsz, seq, -1)
    c0 = 2 * d // LANES
    nh = MOBA_HEADS

    cos_t, sin_t = _rope_tables(positions)
    ya = _moba(proj, cos_t, sin_t, q_blk0=c0, k_blk0=c0 + nh, v_blk0=c0 + 2 * nh)

    par = jnp.zeros((8, LANES), F32)
    par = par.at[0, DN_HEADS:2 * DN_HEADS].set(dn_a_log.astype(F32))
    par = par.at[1, DN_HEADS:2 * DN_HEADS].set(dn_dt_bias.astype(F32))
    beta, gcum, glast = _gates(ba2.reshape(bsz, seq, LANES), par)
    ngrp = seq // DN_GROUP
    grow = jnp.transpose(gcum[:, :, DN_HEADS:2 * DN_HEADS], (0, 2, 1)).reshape(bsz * DN_HEADS, ngrp, 1, DN_GROUP)
    d0 = c0 + 3 * nh
    yb = _deltanet(proj, conv_w, beta, gcum, glast, grow, dn_out_norm[None, :].astype(F32),
                   q_blk0=d0, k_blk0=d0 + DN_HEADS, v_blk0=d0 + 2 * DN_HEADS, z_blk0=d0 + 3 * DN_HEADS)

    wr = _pad_lanes(jnp.concatenate([router_group_w, router_expert_w], axis=1)).T
    wr_hi = wr.astype(BF16)
    wr = jnp.concatenate([wr_hi, (wr - wr_hi.astype(F32)).astype(BF16)], axis=0)
    br = _pad_lanes(jnp.concatenate([router_group_b, router_expert_b])[None, :]).T
    return _merge(ya.reshape(t, mw), yb.reshape(t, dw), proj2, x2,
                  w_branch_moba.astype(BF16), w_branch_delta.astype(BF16), w_out.astype(BF16), wr, br,
                  norm_mix_post[None, :], norm_ffn_pre[None, :], mod3, seq, ga_blk=0, gb_blk=1)


def _moe(x1, u2, logits, mod3, norm_ffn_post, w_gate, w_up, w_down, seq):
    t, d = x1.shape
    info_t, info, counts = _route(logits)
    counts = counts[MOE_GROUPS:MOE_GROUPS + MOE_EXPERTS, 0].astype(jnp.int32)
    rb = MOE_ROWS
    padded = (counts + rb - 1) // rb * rb
    pad_end = jnp.cumsum(padded)
    pad_start = pad_end - padded
    eid = info_t[0:MOE_TOPK].astype(jnp.int32)
    rank = info_t[MOE_TOPK:2 * MOE_TOPK].astype(jnp.int32)
    experts = jnp.arange(MOE_EXPERTS, dtype=jnp.int32)[:, None, None]
    start = jnp.sum(jnp.where(eid[None] == experts, pad_start[:, None, None], 0), axis=0)
    dest_flat = (start + rank).reshape(-1)
    n_blocks = (t * MOE_TOPK + MOE_EXPERTS * (rb - 1)) // rb + 1
    n_used = (pad_end[-1] // rb).astype(jnp.int32)
    blk_row = jnp.minimum(jnp.arange(n_blocks, dtype=jnp.int32), n_used - 1) * rb
    block_expert = jnp.minimum(jnp.sum(pad_end[None, :] <= blk_row[:, None], axis=1),
                               MOE_EXPERTS - 1).astype(jnp.int32)
    ids = jnp.arange(MOE_EXPERTS, dtype=jnp.int32)
    later_used = (ids[None, :] > ids[:, None]) & (counts[None, :] > 0)
    next_used = jnp.min(jnp.where(later_used, ids[None, :], MOE_EXPERTS), axis=1)
    next_used = jnp.where(next_used < MOE_EXPERTS, next_used, -1).astype(jnp.int32)
    own = block_expert[:, None] == ids[None, :]
    look = lambda table: jnp.sum(jnp.where(own, table[None, :], 0), axis=1)
    block_valid = jnp.clip(look(counts) - (blk_row - look(pad_start)), 0, rb).astype(jnp.int32)
    xs = _dispatch(dest_flat, u2, n_blocks * rb)
    ys = _experts(block_expert, look(next_used), block_valid, n_used[None], xs, w_gate, w_up, w_down)
    return _combine(dest_flat, info, x1, norm_ffn_post[None, :], mod3, ys, seq)


def kernel(x, c, positions, w_ada, b_ada, norm_mix_pre, norm_mix_post, norm_ffn_pre, norm_ffn_post, w_in, conv_w, dn_a_log, dn_dt_bias, dn_out_norm, w_branch_moba, w_branch_delta, w_out, router_group_w, router_group_b, router_expert_w, router_expert_b, expert_w_gate, expert_w_up, expert_w_down):
    bsz, seq, d = x.shape
    depth = w_ada.shape[0]
    for layer in range(depth):
        mod = _ada(c, w_ada[layer], b_ada[layer][None, :])
        mod3 = mod.reshape(bsz, 1, -1)
        x1, u2, logits = _mixer_and_router(
            x, mod3, positions, norm_mix_pre[layer], norm_mix_post[layer], norm_ffn_pre[layer], w_in[layer],
            conv_w[layer], dn_a_log[layer], dn_dt_bias[layer], dn_out_norm[layer], w_branch_moba[layer],
            w_branch_delta[layer], w_out[layer], router_group_w[layer], router_group_b[layer],
            router_expert_w[layer], router_expert_b[layer])
        out = _moe(x1, u2, logits, mod3, norm_ffn_post[layer], expert_w_gate[layer], expert_w_up[layer],
                   expert_w_down[layer], seq)
        x = out.reshape(bsz, seq, d)
    return x
```

```python
import functools
import math

import jax
import jax.numpy as jnp
from jax import lax
from jax.experimental import pallas as pl
from jax.experimental.pallas import tpu as pltpu

F32 = jnp.float32
BF16 = jnp.bfloat16
HI = lax.Precision.HIGHEST

NORM_EPS = 1e-6
HEAD_DIM = 128
MOBA_HEADS = 8
MOBA_BLOCK = 256
MOBA_TOPK = 3
MOBA_HEADS_PER_STEP = 2
ROPE_THETA = 500000.0
ROPE_DIM = HEAD_DIM // 4
DN_HEADS = 8
DN_CONV_WIDTH = 4
DN_CHUNK = 64
DN_GROUP = 256
DN_HEADS_PER_STEP = 4
DN_CONV_PAD = 8
DN_CONV_ROWS = 128
MOE_GROUPS = 4
MOE_EXPERTS_PER_GROUP = 8
MOE_EXPERTS = MOE_GROUPS * MOE_EXPERTS_PER_GROUP
MOE_TOPK = 2
MOE_ROWS = 256
DMA_LOOP_UNROLL = 8
MERGE_SUB_ROWS = 128
LANES = 128
NEG = -1e30

VMEM_LIMIT = 60 * 1024 * 1024


def _cparams(*sem):
    return pltpu.CompilerParams(dimension_semantics=sem, vmem_limit_bytes=VMEM_LIMIT)


def _mm(a, b, precision=None):
    return jnp.dot(a, b, precision=precision, preferred_element_type=F32)


def _nt(a, b, precision=None):
    return lax.dot_general(a, b, (((1,), (1,)), ((), ())), precision=precision,
                           preferred_element_type=F32)


def _eye(n, dtype):
    r = lax.broadcasted_iota(jnp.int32, (n, n), 0)
    c = lax.broadcasted_iota(jnp.int32, (n, n), 1)
    return jnp.where(r == c, 1.0, 0.0).astype(dtype)


def _silu(x):
    return x * jax.nn.sigmoid(x)


def _softplus(x):
    return jnp.maximum(x, 0.0) + jnp.log1p(jnp.exp(-jnp.abs(x)))


def _pack_bf16_pair(lo, hi):
    def rne(x):
        b = pltpu.bitcast(x, jnp.uint32)
        return b + jnp.uint32(0x7FFF) + ((b >> 16) & jnp.uint32(1))
    return (rne(hi) & jnp.uint32(0xFFFF0000)) | (rne(lo) >> 16)


def _unpack_bf16_pair(p):
    return pltpu.bitcast(p << 16, F32), pltpu.bitcast(p & jnp.uint32(0xFFFF0000), F32)


def _ada_body(c_ref, w_ref, b_ref, o_ref):
    cond = _silu(c_ref[...])
    w = w_ref[...]
    c_hi = cond.astype(BF16)
    c_lo = (cond - c_hi.astype(F32)).astype(BF16)
    w_hi = w.astype(BF16)
    w_lo = (w - w_hi.astype(F32)).astype(BF16)
    o_ref[...] = _mm(c_hi, w_hi) + _mm(c_lo, w_hi) + _mm(c_hi, w_lo) + b_ref[...]


def _ada(c, w, b):
    bsz, d = c.shape
    n = w.shape[1]
    tn = 1024
    return pl.pallas_call(
        _ada_body,
        grid=(n // tn,),
        in_specs=[pl.BlockSpec((bsz, d), lambda j: (0, 0)),
                  pl.BlockSpec((d, tn), lambda j: (0, j)),
                  pl.BlockSpec((1, tn), lambda j: (0, j))],
        out_specs=pl.BlockSpec((bsz, tn), lambda j: (0, j)),
        out_shape=jax.ShapeDtypeStruct((bsz, n), F32),
        compiler_params=_cparams("parallel"),
        name="ada",
    )(c, w, b)


def _prenorm_body(x_ref, g_ref, sc_ref, sh_ref, ws_ref, u_ref, os_ref):
    x = x_ref[...]
    y = x * lax.rsqrt(jnp.mean(x * x, axis=-1, keepdims=True) + NORM_EPS) * g_ref[...]
    u = (y * (1.0 + sc_ref[...]) + sh_ref[...]).astype(BF16)
    u_ref[...] = u
    os_ref[...] = _mm(u, ws_ref[...])


def _prenorm(x2, gain, mod3, w_small, seq, *, sc_chunk, sh_chunk):
    t, d = x2.shape
    tm = min(256, seq)
    per_b = seq // tm
    return pl.pallas_call(
        _prenorm_body,
        grid=(t // tm,),
        in_specs=[pl.BlockSpec((tm, d), lambda i: (i, 0)),
                  pl.BlockSpec((1, d), lambda i: (0, 0)),
                  pl.BlockSpec((None, 1, d), lambda i: (i // per_b, 0, sc_chunk)),
                  pl.BlockSpec((None, 1, d), lambda i: (i // per_b, 0, sh_chunk)),
                  pl.BlockSpec((d, LANES), lambda i: (0, 0))],
        out_specs=[pl.BlockSpec((tm, d), lambda i: (i, 0)),
                   pl.BlockSpec((tm, LANES), lambda i: (i, 0))],
        out_shape=[jax.ShapeDtypeStruct((t, d), BF16),
                   jax.ShapeDtypeStruct((t, LANES), F32)],
        compiler_params=_cparams("parallel"),
        name="prenorm",
    )(x2, gain, mod3, mod3, w_small)


def _inproj_body(u_ref, wt_ref, o_ref, stage, w_s, sem, *, tn, n_a, b_row0, n_tiles):
    j = pl.program_id(0)

    def fetch(tile):
        row0 = jnp.where(tile < n_a, tile * tn, b_row0 + (tile - n_a) * tn)
        return pltpu.make_async_copy(wt_ref.at[pl.ds(pl.multiple_of(row0, 8), tn), :], stage, sem)

    @pl.when(pl.program_id(1) == 0)
    def _():
        @pl.when(j == 0)
        def _():
            fetch(j).start()

        fetch(j).wait()
        w_s[...] = stage[...].astype(BF16)

        @pl.when(j + 1 < n_tiles)
        def _():
            fetch(j + 1).start()

    o_ref[...] = _nt(u_ref[...], w_s[...]).astype(o_ref.dtype)


def _inproj(u, w_t, n_a, b_row0, n_b, *, out_first_b):
    t, d = u.shape
    tm = min(1024, t)
    tn = 1024
    assert b_row0 % 8 == 0 and b_row0 + n_b * tn <= w_t.shape[0]

    def out_col(j):
        if not out_first_b:
            return j
        return jnp.where(j < n_a, j + n_b, j - n_a)

    return pl.pallas_call(
        functools.partial(_inproj_body, tn=tn, n_a=n_a, b_row0=b_row0, n_tiles=n_a + n_b),
        grid=(n_a + n_b, t // tm),
        in_specs=[pl.BlockSpec((tm, d), lambda j, i: (i, 0)),
                  pl.BlockSpec(memory_space=pl.ANY)],
        out_specs=pl.BlockSpec((tm, tn), lambda j, i: (i, out_col(j))),
        out_shape=jax.ShapeDtypeStruct((t, (n_a + n_b) * tn), BF16),
        scratch_shapes=[pltpu.VMEM((tn, d), F32), pltpu.VMEM((tn, d), BF16), pltpu.SemaphoreType.DMA(())],
        compiler_params=_cparams("arbitrary", "arbitrary"),
        name="inproj",
    )(u, w_t)


def _moba_body(q_ref, k_ref, v_ref, cos_ref, sin_ref, o_ref, kr_s, vt_s, km_s, *, nblk, hp):
    blk = MOBA_BLOCK
    half = ROPE_DIM // 2
    lane = lax.broadcasted_iota(jnp.int32, (blk, HEAD_DIM), 1)
    eye_d = _eye(HEAD_DIM, BF16)
    eye_b = _eye(blk, BF16)
    scale = HEAD_DIM ** -0.5

    def rope(xf, rows):
        partner = jnp.where(lane < half, pltpu.roll(xf, HEAD_DIM - half, 1), pltpu.roll(xf, half, 1))
        return xf * cos_ref[rows, :] + partner * sin_ref[rows, :]

    def cols(h):
        return slice(h * HEAD_DIM, (h + 1) * HEAD_DIM)

    km_s[...] = jnp.zeros_like(km_s)
    for j in range(nblk):
        rows = slice(j * blk, (j + 1) * blk)
        for h in range(hp):
            kc = rope(k_ref[rows, cols(h)].astype(F32), rows)
            kr_s[rows, cols(h)] = kc.astype(BF16)
            km_s[j:j + 1, cols(h)] = jnp.mean(kc, axis=0, keepdims=True)
            vt_s[cols(h), rows] = _nt(eye_d, v_ref[rows, cols(h)]).astype(BF16)

    key_i = lax.broadcasted_iota(jnp.int32, (blk, blk), 0)
    qry_i = lax.broadcasted_iota(jnp.int32, (blk, blk), 1)
    blk_i = lax.broadcasted_iota(jnp.int32, (8, blk), 0)
    causal_bias = jnp.where(key_i <= qry_i, 0.0, NEG)

    def rows(j):
        return slice(j * blk, (j + 1) * blk)

    def begin(i, h):
        qc = rope(q_ref[rows(i), cols(h)].astype(F32), rows(i))
        st = dict(i=i, h=h, qs=(qc * scale).astype(BF16), scores=[], m=None, bias=None)
        if i > 0:
            g_t = _nt(km_s[:, cols(h)], qc, HI)
            rank = jnp.zeros((8, blk), F32)
            for jp in range(i):
                row = g_t[jp:jp + 1, :]
                beats = (row > g_t) | ((row == g_t) & (jp < blk_i))
                rank = rank + jnp.where(beats, 1.0, 0.0)
            sel = (rank < float(MOBA_TOPK)) & (blk_i < i)
            st["bias"] = jnp.where(sel, 0.0, NEG)
        return st

    def score(st, j):
        i = st["i"]
        s = _nt(kr_s[rows(j), cols(st["h"])], st["qs"])
        if j == i:
            s = s + causal_bias
        st["scores"].append(s)
        mj = jnp.max(s, axis=0, keepdims=True)
        if j < i:
            mj = mj + st["bias"][j:j + 1, :]
        st["m"] = mj if st["m"] is None else jnp.maximum(st["m"], mj)

    def accumulate(st, j):
        i = st["i"]
        if j == 0:
            st["den"] = jnp.zeros((1, blk), F32)
            st["acc"] = jnp.zeros((HEAD_DIM, blk), F32)
        shift = (st["bias"][j:j + 1, :] - st["m"]) if j < i else -st["m"]
        p = jnp.exp(st["scores"][j] + shift)
        st["den"] = st["den"] + jnp.sum(p, axis=0, keepdims=True)
        st["acc"] = st["acc"] + _mm(vt_s[cols(st["h"]), rows(j)], p.astype(BF16))
        if j == i:
            o_t = (st["acc"] / st["den"]).astype(BF16)
            o_ref[rows(i), cols(st["h"])] = _nt(eye_b, o_t).astype(o_ref.dtype)

    prev = None
    for i in range(nblk):
        cur = [begin(i, h) for h in range(hp)]
        for j in range(i + 1):
            for h in range(hp):
                score(cur[h], j)
                if prev is not None and j <= i - 1:
                    accumulate(prev[h], j)
        prev = cur
    for j in range(nblk):
        for h in range(hp):
            accumulate(prev[h], j)


def _moba(proj, cos_t, sin_t, *, q_blk0, k_blk0, v_blk0):
    bsz, seq, _ = proj.shape
    nblk = seq // MOBA_BLOCK
    assert seq % MOBA_BLOCK == 0 and 1 <= nblk <= 8
    hd = HEAD_DIM
    hp = MOBA_HEADS_PER_STEP
    wide = hp * hd
    assert MOBA_HEADS % hp == 0 and all(b0 % hp == 0 for b0 in (q_blk0, k_blk0, v_blk0))
    return pl.pallas_call(
        functools.partial(_moba_body, nblk=nblk, hp=hp),
        grid=(bsz, MOBA_HEADS // hp),
        in_specs=[pl.BlockSpec((None, seq, wide), lambda b, h: (b, 0, q_blk0 // hp + h)),
                  pl.BlockSpec((None, seq, wide), lambda b, h: (b, 0, k_blk0 // hp + h)),
                  pl.BlockSpec((None, seq, wide), lambda b, h: (b, 0, v_blk0 // hp + h)),
                  pl.BlockSpec((None, seq, hd), lambda b, h: (b, 0, 0)),
                  pl.BlockSpec((None, seq, hd), lambda b, h: (b, 0, 0))],
        out_specs=pl.BlockSpec((None, seq, wide), lambda b, h: (b, 0, h)),
        out_shape=jax.ShapeDtypeStruct((bsz, seq, MOBA_HEADS * hd), BF16),
        scratch_shapes=[pltpu.VMEM((seq, wide), BF16),
                        pltpu.VMEM((wide, seq), BF16),
                        pltpu.VMEM((8, wide), F32)],
        compiler_params=_cparams("parallel", "parallel"),
        name="moba",
    )(proj, proj, proj, cos_t, sin_t)


def _chunk_masks(n, chunk):
    r = lax.broadcasted_iota(jnp.int32, (n, n), 0)
    c = lax.broadcasted_iota(jnp.int32, (n, n), 1)
    shift = int(math.log2(chunk))
    same = jnp.right_shift(r, shift) == jnp.right_shift(c, shift)
    return r, c, same


def _gates_body(ba_ref, par_ref, beta_ref, g_ref, gl_ref, *, seq):
    grp = DN_GROUP
    r, c, same = _chunk_masks(grp, DN_CHUNK)
    low = jnp.where(same & (c <= r), 1.0, 0.0)
    ones = jnp.where(same, 1.0, 0.0)
    neg_a = -jnp.exp(par_ref[0:1, :])
    dt_b = par_ref[1:2, :]
    for i in range(seq // grp):
        rows = slice(i * grp, (i + 1) * grp)
        x = ba_ref[rows, :].astype(F32)
        beta_ref[rows, :] = jax.nn.sigmoid(x)
        g = neg_a * _softplus(x + dt_b)
        g_ref[rows, :] = _mm(low, g, HI)
        gl_ref[rows, :] = _mm(ones, g, HI)


def _gates(ba, par):
    bsz, seq, _ = ba.shape
    spec = pl.BlockSpec((None, seq, LANES), lambda b: (b, 0, 0))
    return pl.pallas_call(
        functools.partial(_gates_body, seq=seq),
        grid=(bsz,),
        in_specs=[spec, pl.BlockSpec((8, LANES), lambda b: (0, 0))],
        out_specs=[spec, spec, spec],
        out_shape=[jax.ShapeDtypeStruct((bsz, seq, LANES), F32)] * 3,
        compiler_params=_cparams("parallel"),
        name="dn_gates",
    )(ba, par)


def _dn_body(q_ref, k_ref, v_ref, z_ref, cwq_ref, cwk_ref, cwv_ref, beta_ref, g_ref, gl_ref, grow_ref,
             gain_ref, o_ref, stage_s, qkv_a, qkv_b, state_s, vnew_s, *bufs, seq, hp):
    bufs_a, bufs_b = bufs[:5], bufs[5:]
    hg = pl.program_id(1)
    grp = DN_GROUP
    chunk = DN_CHUNK
    ngrp = seq // grp
    hd = HEAD_DIM
    pad = DN_CONV_PAD

    crows = DN_CONV_ROWS
    r, c, same = _chunk_masks(grp, chunk)
    low_incl = same & (c <= r)
    low_strict = same & (c < r)
    eye_g = jnp.where(r == c, 1.0, 0.0)
    eye_d = _eye(hd, BF16)
    lane = lax.broadcasted_iota(jnp.int32, (grp, LANES), 1)
    col_chunk = jnp.right_shift(lax.broadcasted_iota(jnp.int32, (hd, grp), 1), int(math.log2(chunk)))
    gain = gain_ref[...]
    n_double = int(math.log2(chunk)) - 1

    def pick(ref, r0, lane_id):
        return jnp.sum(jnp.where(lane == lane_id, ref[pl.ds(r0, grp), :], 0.0), axis=-1, keepdims=True)

    state_s[...] = jnp.zeros_like(state_s)
    heads = range(hp)

    cols = [slice(hh * hd, (hh + 1) * hd) for hh in heads]
    head = [hg * hp + hh for hh in heads]

    tensors = ((q_ref, cwq_ref, True, hd ** -0.5), (k_ref, cwk_ref, True, None), (v_ref, cwv_ref, False, None))
    nsub = grp // crows

    def conv(gi, qkv, first=False):
        r0 = pl.multiple_of(gi * grp, grp)
        for ti, (src, cw_ref, l2, post) in enumerate(tensors):
            cw = cw_ref[...]
            for sb in range(nsub):
                stage_s[ti, sb, pad:pad + crows, :] = src[pl.ds(r0 + sb * crows, crows), :].astype(F32)
            if first:
                stage_s[ti, 0, 0:pad, :] = jnp.zeros((pad, hp * hd), F32)
            else:
                before = src[pl.ds(pl.multiple_of(r0 - 2 * pad, 2 * pad), 2 * pad), :].astype(F32)
                stage_s[ti, 0, 0:pad, :] = before[pad:, :]
            for sb in range(1, nsub):
                stage_s[ti, sb, 0:pad, :] = stage_s[ti, sb - 1, crows:crows + pad, :]
            for sb in range(nsub):
                for hh in heads:
                    acc = None
                    for j in range(DN_CONV_WIDTH):
                        off = pad - (DN_CONV_WIDTH - 1) + j
                        term = stage_s[ti, sb, off:off + crows, cols[hh]] * cw[j:j + 1, cols[hh]]
                        acc = term if acc is None else acc + term
                    yh = _silu(acc)
                    if l2:
                        yh = yh * lax.rsqrt(jnp.sum(yh * yh, axis=-1, keepdims=True) + NORM_EPS)
                    if post is not None:
                        yh = yh * post
                    qkv[ti, sb * crows:(sb + 1) * crows, cols[hh]] = yh
                yield

    def prepare(gi, qkv, bufs):
        u_r, w_r, qk_r, qd_r, kt_r = bufs
        r0 = pl.multiple_of(gi * grp, grp)
        q = [qkv[0, :, cols[hh]] for hh in heads]
        k = [qkv[1, :, cols[hh]] for hh in heads]
        v = [qkv[2, :, cols[hh]] for hh in heads]
        beta = [pick(beta_ref, r0, head[hh]) for hh in heads]
        g_col = [pick(g_ref, r0, DN_HEADS + head[hh]) for hh in heads]
        gl_col = [pick(gl_ref, r0, DN_HEADS + head[hh]) for hh in heads]
        decay = [jnp.exp(jnp.minimum(g_col[hh] - grow_ref[hh, gi], 0.0)) for hh in heads]
        e_g = [jnp.exp(g_col[hh]) for hh in heads]
        kb = [k[hh] * beta[hh] for hh in heads]
        vb = [(v[hh] * beta[hh]).astype(BF16) for hh in heads]
        k16 = [k[hh].astype(BF16) for hh in heads]
        n_mat = [jnp.where(low_strict, _nt(kb[hh].astype(BF16), k16[hh]) * decay[hh], 0.0) for hh in heads]
        p = [(-n_mat[hh]).astype(BF16) for hh in heads]
        x = [eye_g - n_mat[hh] for hh in heads]
        qk = [jnp.where(low_incl, _nt(q[hh].astype(BF16), k16[hh]) * decay[hh], 0.0).astype(BF16) for hh in heads]
        for hh in heads:
            qk_r[hh] = qk[hh]
            qd_r[hh] = (q[hh] * e_g[hh]).astype(BF16)
            kt_r[hh] = _nt(eye_d, (k[hh] * jnp.exp(gl_col[hh] - g_col[hh])).astype(BF16)).astype(BF16)
        yield
        for _ in range(n_double):
            p2 = [_mm(p[hh], p[hh]).astype(BF16) for hh in heads]
            x = [x[hh] + _mm(x[hh].astype(BF16), p2[hh]) for hh in heads]
            p = p2
            yield
        x16 = [x[hh].astype(BF16) for hh in heads]
        for hh in heads:
            u_r[hh] = _mm(x16[hh], vb[hh])
            w_r[hh] = _mm(x16[hh], (kb[hh] * e_g[hh]).astype(BF16)).astype(BF16)
        yield

    def scan(gi, bufs):
        u_r, w_r, qk_r, qd_r, kt_r = bufs
        r0 = pl.multiple_of(gi * grp, grp)
        vnew_s[...] = jnp.zeros_like(vnew_s)
        for ci in range(grp // chunk):
            rows = slice(ci * chunk, (ci + 1) * chunk)
            gl_row = jnp.exp(gl_ref[pl.ds(r0 + ci * chunk, 1), :])
            for hh in heads:
                state = state_s[hh]
                s16 = state.astype(BF16)
                v_new = u_r[hh, rows, :] - _mm(w_r[hh, rows, :], s16)
                vnew_s[hh, rows, :] = v_new.astype(BF16)
                o = _mm(qd_r[hh, rows, :], s16) + _mm(qk_r[hh, rows, :], vnew_s[hh])
                chunk_decay = jnp.sum(jnp.where(lane[0:1, :] == DN_HEADS + head[hh], gl_row, 0.0),
                                      axis=-1, keepdims=True)
                k_tail_t = kt_r[hh]
                kt = jnp.where(col_chunk == ci, k_tail_t, jnp.zeros_like(k_tail_t))
                state_s[hh] = state * chunk_decay + _mm(kt, vnew_s[hh])
                on = o * lax.rsqrt(jnp.mean(o * o, axis=-1, keepdims=True) + NORM_EPS) * gain
                zz = z_ref[pl.ds(r0 + ci * chunk, chunk), cols[hh]].astype(F32)
                o_ref[pl.ds(r0 + ci * chunk, chunk), cols[hh]] = (on * _silu(zz)).astype(o_ref.dtype)
            yield

    def interleave(first, *others):
        for _ in first:
            for steps in others:
                next(steps, None)
                next(steps, None)
        for steps in others:
            for _ in steps:
                pass

    interleave(conv(0, qkv_a, first=True))
    interleave(prepare(0, qkv_a, bufs_a), conv(1, qkv_b))

    def pair(pi, carry):
        g0 = 2 * pi
        interleave(scan(g0, bufs_a), prepare(g0 + 1, qkv_b, bufs_b), conv(g0 + 2, qkv_a))
        interleave(scan(g0 + 1, bufs_b), prepare(g0 + 2, qkv_a, bufs_a), conv(g0 + 3, qkv_b))
        return carry

    lax.fori_loop(0, ngrp // 2 - 1, pair, 0)
    interleave(scan(ngrp - 2, bufs_a), prepare(ngrp - 1, qkv_b, bufs_b))
    interleave(scan(ngrp - 1, bufs_b))


def _deltanet(proj, conv_w, beta, gcum, glast, grow, gain, *, q_blk0, k_blk0, v_blk0, z_blk0):
    bsz, seq, _ = proj.shape
    hd = HEAD_DIM
    nh = DN_HEADS
    assert seq % DN_GROUP == 0
    ngrp = seq // DN_GROUP

    hp = DN_HEADS_PER_STEP
    wide = hp * hd
    assert nh % hp == 0 and all(b0 % hp == 0 for b0 in (q_blk0, k_blk0, v_blk0, z_blk0))

    def col(blk0):
        return pl.BlockSpec((None, seq, wide), lambda b, h: (b, 0, blk0 // hp + h))

    def cw(blk0):
        return pl.BlockSpec((DN_CONV_WIDTH, wide), lambda b, h: (0, blk0 // hp + h))

    full = pl.BlockSpec((None, seq, LANES), lambda b, h: (b, 0, 0))
    return pl.pallas_call(
        functools.partial(_dn_body, seq=seq, hp=hp),
        grid=(bsz, nh // hp),
        in_specs=[col(q_blk0), col(k_blk0), col(v_blk0), col(z_blk0),
                  cw(0), cw(nh), cw(2 * nh),
                  full, full, full,
                  pl.BlockSpec((hp, ngrp, 1, DN_GROUP), lambda b, h: (b * (nh // hp) + h, 0, 0, 0)),
                  pl.BlockSpec((1, hd), lambda b, h: (0, 0))],
        out_specs=pl.BlockSpec((None, seq, wide), lambda b, h: (b, 0, h)),
        out_shape=jax.ShapeDtypeStruct((bsz, seq, nh * hd), BF16),
        scratch_shapes=[pltpu.VMEM((3, DN_GROUP // DN_CONV_ROWS, DN_CONV_ROWS + DN_CONV_PAD, wide), F32)]
        + [pltpu.VMEM((3, DN_GROUP, wide), F32)] * 2
        + [pltpu.VMEM((hp, hd, hd), F32), pltpu.VMEM((hp, DN_GROUP, hd), BF16)]
        + [pltpu.VMEM((hp, DN_GROUP, hd), F32), pltpu.VMEM((hp, DN_GROUP, hd), BF16),
           pltpu.VMEM((hp, DN_GROUP, DN_GROUP), BF16), pltpu.VMEM((hp, DN_GROUP, hd), BF16),
           pltpu.VMEM((hp, hd, DN_GROUP), BF16)] * 2,
        compiler_params=_cparams("parallel", "parallel"),
        name="deltanet",
    )(proj, proj, proj, proj, conv_w, conv_w, conv_w, beta, gcum, glast, grow, gain)


def _merge_body(ya_ref, yb_ref, ga_ref, gb_ref, x_ref, wm_ref, wd_ref, wo_ref, wr_ref, br_ref,
                npost_ref, npre_ref, gt_ref, sc_ref, sh_ref, x1_ref, u2_ref, lg_ref):
    sub = MERGE_SUB_ROWS
    for s in range(x_ref.shape[0] // sub):
        rows = slice(s * sub, (s + 1) * sub)
        ma = _mm(ya_ref[rows, :], wm_ref[...])
        mb = _mm(yb_ref[rows, :], wd_ref[...])
        merged = (jax.nn.sigmoid(ga_ref[rows, :].astype(F32)) * ma
                  + jax.nn.sigmoid(gb_ref[rows, :].astype(F32)) * mb)
        y = _mm(merged.astype(BF16), wo_ref[...])
        yn = y * lax.rsqrt(jnp.mean(y * y, axis=-1, keepdims=True) + NORM_EPS) * npost_ref[...]
        x1 = x_ref[rows, :] + gt_ref[...] * yn
        x1_ref[rows, :] = x1
        un = x1 * lax.rsqrt(jnp.mean(x1 * x1, axis=-1, keepdims=True) + NORM_EPS) * npre_ref[...]
        u2 = un * (1.0 + sc_ref[...]) + sh_ref[...]
        half = u2.shape[1] // 2
        u2_ref[rows, :] = _pack_bf16_pair(u2[:, :half], u2[:, half:])
        u_hi = u2.astype(BF16)
        u_lo = (u2 - u_hi.astype(F32)).astype(BF16)
        lg2 = _nt(wr_ref[...], u_hi)
        lg_ref[:, rows] = lg2[:LANES, :] + lg2[LANES:, :] + _nt(wr_ref[:LANES, :], u_lo) + br_ref[...]


def _merge(ya, yb, proj2, x2, wm, wd, wo, wr, br, npost, npre, mod3, seq, *, ga_blk, gb_blk):
    t, d = x2.shape
    wa = ya.shape[1]
    tm = min(256, seq)
    per_b = seq // tm
    const = lambda i: (0, 0)
    once = dict(pipeline_mode=pl.Buffered(1))

    def modspec(chunk):
        return pl.BlockSpec((None, 1, d), lambda i: (i // per_b, 0, chunk))

    return pl.pallas_call(
        _merge_body,
        grid=(t // tm,),
        in_specs=[pl.BlockSpec((tm, wa), lambda i: (i, 0)),
                  pl.BlockSpec((tm, wa), lambda i: (i, 0)),
                  pl.BlockSpec((tm, d), lambda i: (i, ga_blk)),
                  pl.BlockSpec((tm, d), lambda i: (i, gb_blk)),
                  pl.BlockSpec((tm, d), lambda i: (i, 0)),
                  pl.BlockSpec((wa, d), const, **once),
                  pl.BlockSpec((wa, d), const, **once),
                  pl.BlockSpec((d, d), const, **once),
                  pl.BlockSpec((2 * LANES, d), const, **once),
                  pl.BlockSpec((LANES, 1), const),
                  pl.BlockSpec((1, d), const),
                  pl.BlockSpec((1, d), const),
                  modspec(2), modspec(4), modspec(3)],
        out_specs=[pl.BlockSpec((tm, d), lambda i: (i, 0)),
                   pl.BlockSpec((tm, d // 2), lambda i: (i, 0)),
                   pl.BlockSpec((LANES, tm), lambda i: (0, i))],
        out_shape=[jax.ShapeDtypeStruct((t, d), F32),
                   jax.ShapeDtypeStruct((t, d // 2), jnp.uint32),
                   jax.ShapeDtypeStruct((LANES, t), F32)],
        compiler_params=_cparams("parallel"),
        name="merge",
    )(ya, yb, proj2, proj2, x2, wm, wd, wo, wr, br, npost, npre, mod3, mod3, mod3)


def _route_body(lg_ref, info_ref, col_ref, cnt_ref, run_s, *, tr):
    @pl.when(pl.program_id(0) == 0)
    def _():
        run_s[...] = jnp.zeros_like(run_s)

    lg = lg_ref[...]
    row = lax.broadcasted_iota(jnp.int32, (LANES, tr), 0)
    row_f = row.astype(F32)
    big = float(LANES)

    def first_max(vals, mask):
        mx = jnp.max(jnp.where(mask, vals, NEG), axis=0, keepdims=True)
        idx = jnp.min(jnp.where(mask & (vals == mx), row_f, big), axis=0, keepdims=True)
        return mx, idx

    gmask = row < MOE_GROUPS
    gmax, gidx = first_max(lg, gmask)
    p_group = 1.0 / jnp.sum(jnp.where(gmask, jnp.exp(lg - gmax), 0.0), axis=0, keepdims=True)
    lo = float(MOE_GROUPS) + gidx * float(MOE_EXPERTS_PER_GROUP)
    emask = (row_f >= lo) & (row_f < lo + float(MOE_EXPERTS_PER_GROUP))
    m1, i1 = first_max(lg, emask)
    m2, i2 = first_max(lg, emask & (row_f != i1))
    e2 = jnp.exp(m2 - m1)
    w1 = p_group / (1.0 + e2)
    w2 = p_group * e2 / (1.0 + e2)
    oh1 = row_f == i1
    oh2 = row_f == i2
    oh = jnp.where(oh1 | oh2, 1.0, 0.0).astype(BF16)
    r = lax.broadcasted_iota(jnp.int32, (tr, tr), 0)
    c = lax.broadcasted_iota(jnp.int32, (tr, tr), 1)
    before = jnp.where(r < c, 1.0, 0.0).astype(BF16)
    prefix = _mm(oh, before) + run_s[:, 0:1]
    rank1 = jnp.sum(jnp.where(oh1, prefix, 0.0), axis=0, keepdims=True)
    rank2 = jnp.sum(jnp.where(oh2, prefix, 0.0), axis=0, keepdims=True)
    run_s[...] = run_s[...] + jnp.sum(oh.astype(F32), axis=1, keepdims=True)
    goff = float(MOE_GROUPS)
    info = jnp.where(row == 0, i1 - goff, 0.0)
    info = jnp.where(row == 1, i2 - goff, info)
    info = jnp.where(row == 2, rank1, info)
    info = jnp.where(row == 3, rank2, info)
    info = jnp.where(row == 4, w1, info)
    info = jnp.where(row == 5, w2, info)
    info_ref[...] = info[0:8, :]
    col_ref[...] = _nt(_eye(tr, F32), info, HI)
    cnt_ref[...] = run_s[...]


def _route(logits_t):
    t = logits_t.shape[1]
    tr = min(256, t)
    return pl.pallas_call(
        functools.partial(_route_body, tr=tr),
        grid=(t // tr,),
        in_specs=[pl.BlockSpec((LANES, tr), lambda i: (0, i))],
        out_specs=[pl.BlockSpec((8, tr), lambda i: (0, i)),
                   pl.BlockSpec((tr, LANES), lambda i: (i, 0)),
                   pl.BlockSpec((LANES, LANES), lambda i: (0, 0))],
        out_shape=[jax.ShapeDtypeStruct((8, t), F32),
                   jax.ShapeDtypeStruct((t, LANES), F32),
                   jax.ShapeDtypeStruct((LANES, LANES), F32)],
        scratch_shapes=[pltpu.VMEM((LANES, LANES), F32)],
        compiler_params=_cparams("arbitrary"),
        name="route",
    )(logits_t)


def _dispatch_body(d0_ref, d1_ref, u_ref, xs_ref, sem, *, td):
    dests = (d0_ref, d1_ref)

    def row_copy(r, k):
        return pltpu.make_async_copy(u_ref.at[pl.ds(r, 1), :], xs_ref.at[pl.ds(dests[k][r], 1), :], sem)

    def start(r, carry):
        for k in range(MOE_TOPK):
            row_copy(r, k).start()
        return carry

    def wait(r, carry):
        for k in range(MOE_TOPK):
            row_copy(r, k).wait()
        return carry

    lax.fori_loop(0, td, start, 0, unroll=DMA_LOOP_UNROLL)
    lax.fori_loop(0, td, wait, 0, unroll=DMA_LOOP_UNROLL)


def _dispatch(dest, u2, n_rows):
    t, d = u2.shape
    td = min(1024, t)
    nt = t // td
    return pl.pallas_call(
        functools.partial(_dispatch_body, td=td),
        grid=(nt,),
        in_specs=[pl.BlockSpec((td,), lambda i: (i,), memory_space=pltpu.SMEM),
                  pl.BlockSpec((td,), lambda i: (nt + i,), memory_space=pltpu.SMEM),
                  pl.BlockSpec((td, d), lambda i: (i, 0))],
        out_specs=pl.BlockSpec(memory_space=pl.ANY),
        out_shape=jax.ShapeDtypeStruct((n_rows, d), u2.dtype),
        scratch_shapes=[pltpu.SemaphoreType.DMA(())],
        compiler_params=_cparams("arbitrary"),
        name="dispatch",
    )(dest, dest, u2)


def _experts_body(be_ref, nx_ref, nv_ref, nu_ref, x_ref, wg_ref, wu_ref, wd_ref, o_ref,
                  stage_g, stage_u, stage_d, wg_s, wu_s, wd_s, sems):
    i = pl.program_id(0)

    def fetch(e):
        return (pltpu.make_async_copy(wg_ref.at[e], stage_g, sems.at[0]),
                pltpu.make_async_copy(wu_ref.at[e], stage_u, sems.at[1]),
                pltpu.make_async_copy(wd_ref.at[e], stage_d, sems.at[2]))

    @pl.when(i == 0)
    def _():
        for cp in fetch(be_ref[0]):
            cp.start()

    @pl.when(jnp.logical_or(i == 0, be_ref[i] != be_ref[jnp.maximum(i - 1, 0)]))
    def _():
        for cp in fetch(be_ref[i]):
            cp.wait()
        wg_s[...] = stage_g[...].astype(BF16)
        wu_s[...] = stage_u[...].astype(BF16)
        wd_s[...] = stage_d[...].astype(BF16)

        @pl.when(nx_ref[i] >= 0)
        def _():
            for cp in fetch(nx_ref[i]):
                cp.start()

    @pl.when(i < nu_ref[0])
    def _():
        row = lax.broadcasted_iota(jnp.int32, (x_ref.shape[0], 1), 0)
        packed = jnp.where(row < nv_ref[i], x_ref[...], jnp.uint32(0))
        x_lo, x_hi = (v.astype(BF16) for v in _unpack_bf16_pair(packed))
        half = packed.shape[1]
        gate = _mm(x_lo, wg_s[0:half, :]) + _mm(x_hi, wg_s[half:, :])
        up = _mm(x_lo, wu_s[0:half, :]) + _mm(x_hi, wu_s[half:, :])
        y = _mm((_silu(gate) * up).astype(BF16), wd_s[...])
        o_ref[...] = _pack_bf16_pair(y[:, :half], y[:, half:])

    @pl.when(i >= nu_ref[0])
    def _():
        o_ref[...] = jnp.zeros_like(o_ref)


def _experts(block_expert, next_expert, block_valid, n_used, xs, wg, wu, wd):
    nr, dp = xs.shape
    d, ff = wg.shape[1], wg.shape[2]
    assert d == 2 * dp
    rb = MOE_ROWS
    row_map = lambda i, be, nx, nv, nu: (jnp.minimum(i, nu[0] - 1), 0)
    hbm = pl.BlockSpec(memory_space=pl.ANY)
    grid_spec = pltpu.PrefetchScalarGridSpec(
        num_scalar_prefetch=4,
        grid=(nr // rb,),
        in_specs=[pl.BlockSpec((rb, dp), row_map), hbm, hbm, hbm],
        out_specs=pl.BlockSpec((rb, dp), lambda i, be, nx, nv, nu: (i, 0)),
        scratch_shapes=[pltpu.VMEM((d, ff), F32), pltpu.VMEM((d, ff), F32), pltpu.VMEM((ff, d), F32),
                        pltpu.VMEM((d, ff), BF16), pltpu.VMEM((d, ff), BF16), pltpu.VMEM((ff, d), BF16),
                        pltpu.SemaphoreType.DMA((3,))],
    )
    return pl.pallas_call(
        _experts_body,
        grid_spec=grid_spec,
        out_shape=jax.ShapeDtypeStruct((nr, dp), jnp.uint32),
        compiler_params=_cparams("arbitrary"),
        name="experts",
    )(block_expert, next_expert, block_valid, n_used, xs, wg, wu, wd)


def _combine_body(d0_ref, d1_ref, n0_ref, n1_ref, info_ref, x1_ref, npost_ref, gt_ref, yb_ref, o_ref,
                  buf, sems, *, tc, rc, nt):
    i = pl.program_id(0)
    slot = lax.rem(i, 2)
    own = (d0_ref, d1_ref)
    ahead = (n0_ref, n1_ref)

    def row_copy(dests, sl, r, k):
        return pltpu.make_async_copy(yb_ref.at[pl.ds(dests[k][r], 1), :], buf.at[sl, k, pl.ds(r, 1), :],
                                     sems.at[sl])

    @pl.when(i == 0)
    def _():
        def start(r, carry):
            for k in range(MOE_TOPK):
                row_copy(own, slot, r, k).start()
            return carry

        lax.fori_loop(0, tc, start, 0, unroll=DMA_LOOP_UNROLL)

    def wait(r, carry):
        for k in range(MOE_TOPK):
            row_copy(own, slot, r, k).wait()
        return carry

    lax.fori_loop(0, tc, wait, 0, unroll=DMA_LOOP_UNROLL)

    half = buf.shape[3]
    gain_lo, gain_hi = npost_ref[:, :half], npost_ref[:, half:]
    gate_lo, gate_hi = gt_ref[:, :half], gt_ref[:, half:]

    def chunk(ci, carry, request_next):
        r0 = pl.multiple_of(ci * rc, rc)
        info = info_ref[pl.ds(r0, rc), :]
        w0, w1 = info[:, 4:5], info[:, 5:6]
        a_lo, a_hi = _unpack_bf16_pair(buf[slot, 0, pl.ds(r0, rc), :])
        b_lo, b_hi = _unpack_bf16_pair(buf[slot, 1, pl.ds(r0, rc), :])
        y_lo = w0 * a_lo + w1 * b_lo
        y_hi = w0 * a_hi + w1 * b_hi
        ms = (jnp.sum(y_lo * y_lo, axis=-1, keepdims=True)
              + jnp.sum(y_hi * y_hi, axis=-1, keepdims=True)) * (1.0 / (2 * half))
        inv = lax.rsqrt(ms + NORM_EPS)
        o_ref[pl.ds(r0, rc), :half] = x1_ref[pl.ds(r0, rc), :half] + gate_lo * (y_lo * inv * gain_lo)
        o_ref[pl.ds(r0, rc), half:] = x1_ref[pl.ds(r0, rc), half:] + gate_hi * (y_hi * inv * gain_hi)
        if request_next:
            for rr in range(rc):
                for k in range(MOE_TOPK):
                    row_copy(ahead, 1 - slot, r0 + rr, k).start()
        return carry

    @pl.when(i + 1 < nt)
    def _():
        lax.fori_loop(0, tc // rc, functools.partial(chunk, request_next=True), 0)

    @pl.when(i + 1 == nt)
    def _():
        lax.fori_loop(0, tc // rc, functools.partial(chunk, request_next=False), 0)


def _combine(dest, info, x1, npost, mod3, yb, seq):
    t, d = x1.shape
    tc = min(1024, seq)
    rc = min(128, tc)
    per_b = seq // tc
    nt = t // tc
    nxt = lambda i: jnp.minimum(i + 1, nt - 1)
    return pl.pallas_call(
        functools.partial(_combine_body, tc=tc, rc=rc, nt=nt),
        grid=(nt,),
        in_specs=[pl.BlockSpec((tc,), lambda i: (i,), memory_space=pltpu.SMEM),
                  pl.BlockSpec((tc,), lambda i: (nt + i,), memory_space=pltpu.SMEM),
                  pl.BlockSpec((tc,), lambda i: (nxt(i),), memory_space=pltpu.SMEM),
                  pl.BlockSpec((tc,), lambda i: (nt + nxt(i),), memory_space=pltpu.SMEM),
                  pl.BlockSpec((tc, LANES), lambda i: (i, 0)),
                  pl.BlockSpec((tc, d), lambda i: (i, 0)),
                  pl.BlockSpec((1, d), lambda i: (0, 0)),
                  pl.BlockSpec((None, 1, d), lambda i: (i // per_b, 0, 5)),
                  pl.BlockSpec(memory_space=pl.ANY)],
        out_specs=pl.BlockSpec((tc, d), lambda i: (i, 0)),
        out_shape=jax.ShapeDtypeStruct((t, d), F32),
        scratch_shapes=[pltpu.VMEM((2, MOE_TOPK, tc, d // 2), jnp.uint32), pltpu.SemaphoreType.DMA((2,))],
        compiler_params=_cparams("arbitrary"),
        name="combine",
    )(dest, dest, dest, dest, info, x1, npost, mod3, yb)


def _rope_tables(positions):
    half = ROPE_DIM // 2
    inv_freq = jnp.power(ROPE_THETA, -jnp.arange(half, dtype=F32) * (2.0 / ROPE_DIM))
    ang = inv_freq[None, :, None] * positions.astype(F32)[:, None, :]
    cos, sin = jnp.cos(ang), jnp.sin(ang)
    rest = (positions.shape[0], HEAD_DIM - ROPE_DIM, positions.shape[1])
    cos_t = jnp.concatenate([cos, cos, jnp.ones(rest, F32)], axis=1)
    sin_t = jnp.concatenate([-sin, sin, jnp.zeros(rest, F32)], axis=1)
    return jnp.transpose(cos_t, (0, 2, 1)), jnp.transpose(sin_t, (0, 2, 1))


def _pad_lanes(v, n=LANES):
    return jnp.pad(v, [(0, 0)] * (v.ndim - 1) + [(0, n - v.shape[-1])])


def _mixer_and_router(x, mod3, positions, norm_mix_pre, norm_mix_post, norm_ffn_pre, w_in, conv_w,
                      dn_a_log, dn_dt_bias, dn_out_norm, w_branch_moba, w_branch_delta, w_out,
                      router_group_w, router_group_b, router_expert_w, router_expert_b):
    bsz, seq, d = x.shape
    t = bsz * seq
    mw = MOBA_HEADS * HEAD_DIM
    dw = DN_HEADS * HEAD_DIM
    o_qa, o_ka, o_va = 0, mw, 2 * mw
    o_dn = 3 * mw
    o_z = o_dn + 3 * dw
    o_ba = o_z + dw
    o_ga = o_ba + 2 * DN_HEADS
    o_gb = o_ga + d
    w_small = _pad_lanes(w_in[:, o_ba:o_ga]).astype(BF16)
    x2 = x.reshape(t, d)
    u, ba2 = _prenorm(x2, norm_mix_pre[None, :], mod3, w_small, seq, sc_chunk=1, sh_chunk=0)
    assert o_ba % 1024 == 0 and (2 * d) % 1024 == 0
    proj2 = _inproj(u, w_in.T, o_ba // 1024, o_ga, 2 * d // 1024, out_first_b=True)
    proj = proj2.reshape(bsz, seq, -1)
    c0 = 2 * d // LANES
    nh = MOBA_HEADS

    cos_t, sin_t = _rope_tables(positions)
    ya = _moba(proj, cos_t, sin_t, q_blk0=c0, k_blk0=c0 + nh, v_blk0=c0 + 2 * nh)

    par = jnp.zeros((8, LANES), F32)
    par = par.at[0, DN_HEADS:2 * DN_HEADS].set(dn_a_log.astype(F32))
    par = par.at[1, DN_HEADS:2 * DN_HEADS].set(dn_dt_bias.astype(F32))
    beta, gcum, glast = _gates(ba2.reshape(bsz, seq, LANES), par)
    ngrp = seq // DN_GROUP
    grow = jnp.transpose(gcum[:, :, DN_HEADS:2 * DN_HEADS], (0, 2, 1)).reshape(bsz * DN_HEADS, ngrp, 1, DN_GROUP)
    d0 = c0 + 3 * nh
    yb = _deltanet(proj, conv_w, beta, gcum, glast, grow, dn_out_norm[None, :].astype(F32),
                   q_blk0=d0, k_blk0=d0 + DN_HEADS, v_blk0=d0 + 2 * DN_HEADS, z_blk0=d0 + 3 * DN_HEADS)

    wr = _pad_lanes(jnp.concatenate([router_group_w, router_expert_w], axis=1)).T
    wr_hi = wr.astype(BF16)
    wr = jnp.concatenate([wr_hi, (wr - wr_hi.astype(F32)).astype(BF16)], axis=0)
    br = _pad_lanes(jnp.concatenate([router_group_b, router_expert_b])[None, :]).T
    return _merge(ya.reshape(t, mw), yb.reshape(t, dw), proj2, x2,
                  w_branch_moba.astype(BF16), w_branch_delta.astype(BF16), w_out.astype(BF16), wr, br,
                  norm_mix_post[None, :], norm_ffn_pre[None, :], mod3, seq, ga_blk=0, gb_blk=1)


def _moe(x1, u2, logits, mod3, norm_ffn_post, w_gate, w_up, w_down, seq):
    t, d = x1.shape
    info_t, info, counts = _route(logits)
    counts = counts[MOE_GROUPS:MOE_GROUPS + MOE_EXPERTS, 0].astype(jnp.int32)
    rb = MOE_ROWS
    padded = (counts + rb - 1) // rb * rb
    pad_end = jnp.cumsum(padded)
    pad_start = pad_end - padded
    eid = info_t[0:MOE_TOPK].astype(jnp.int32)
    rank = info_t[MOE_TOPK:2 * MOE_TOPK].astype(jnp.int32)
    experts = jnp.arange(MOE_EXPERTS, dtype=jnp.int32)[:, None, None]
    start = jnp.sum(jnp.where(eid[None] == experts, pad_start[:, None, None], 0), axis=0)
    dest_flat = (start + rank).reshape(-1)
    n_blocks = (t * MOE_TOPK + MOE_EXPERTS * (rb - 1)) // rb + 1
    n_used = (pad_end[-1] // rb).astype(jnp.int32)
    blk_row = jnp.minimum(jnp.arange(n_blocks, dtype=jnp.int32), n_used - 1) * rb
    block_expert = jnp.minimum(jnp.sum(pad_end[None, :] <= blk_row[:, None], axis=1),
                               MOE_EXPERTS - 1).astype(jnp.int32)
    ids = jnp.arange(MOE_EXPERTS, dtype=jnp.int32)
    later_used = (ids[None, :] > ids[:, None]) & (counts[None, :] > 0)
    next_used = jnp.min(jnp.where(later_used, ids[None, :], MOE_EXPERTS), axis=1)
    next_used = jnp.where(next_used < MOE_EXPERTS, next_used, -1).astype(jnp.int32)
    own = block_expert[:, None] == ids[None, :]
    look = lambda table: jnp.sum(jnp.where(own, table[None, :], 0), axis=1)
    block_valid = jnp.clip(look(counts) - (blk_row - look(pad_start)), 0, rb).astype(jnp.int32)
    xs = _dispatch(dest_flat, u2, n_blocks * rb)
    ys = _experts(block_expert, look(next_used), block_valid, n_used[None], xs, w_gate, w_up, w_down)
    return _combine(dest_flat, info, x1, norm_ffn_post[None, :], mod3, ys, seq)


def kernel(x, c, positions, w_ada, b_ada, norm_mix_pre, norm_mix_post, norm_ffn_pre, norm_ffn_post, w_in, conv_w, dn_a_log, dn_dt_bias, dn_out_norm, w_branch_moba, w_branch_delta, w_out, router_group_w, router_group_b, router_expert_w, router_expert_b, expert_w_gate, expert_w_up, expert_w_down):
    bsz, seq, d = x.shape
    depth = w_ada.shape[0]
    for layer in range(depth):
        mod = _ada(c, w_ada[layer], b_ada[layer][None, :])
        mod3 = mod.reshape(bsz, 1, -1)
        x1, u2, logits = _mixer_and_router(
            x, mod3, positions, norm_mix_pre[layer], norm_mix_post[layer], norm_ffn_pre[layer], w_in[layer],
            conv_w[layer], dn_a_log[layer], dn_dt_bias[layer], dn_out_norm[layer], w_branch_moba[layer],
            w_branch_delta[layer], w_out[layer], router_group_w[layer], router_group_b[layer],
            router_expert_w[layer], router_expert_b[layer])
        out = _moe(x1, u2, logits, mod3, norm_ffn_post[layer], expert_w_gate[layer], expert_w_up[layer],
                   expert_w_down[layer], seq)
        x = out.reshape(bsz, seq, d)
    return x
```

```python
import functools
import math

import jax
import jax.numpy as jnp
from jax import lax
from jax.experimental import pallas as pl
from jax.experimental.pallas import tpu as pltpu

F32 = jnp.float32
BF16 = jnp.bfloat16
HI = lax.Precision.HIGHEST

NORM_EPS = 1e-6
HEAD_DIM = 128
MOBA_HEADS = 8
MOBA_BLOCK = 256
MOBA_TOPK = 3
MOBA_HEADS_PER_STEP = 2
ROPE_THETA = 500000.0
ROPE_DIM = HEAD_DIM // 4
DN_HEADS = 8
DN_CONV_WIDTH = 4
DN_CHUNK = 64
DN_GROUP = 256
DN_HEADS_PER_STEP = 4
DN_CONV_PAD = 8
DN_CONV_ROWS = 128
MOE_GROUPS = 4
MOE_EXPERTS_PER_GROUP = 8
MOE_EXPERTS = MOE_GROUPS * MOE_EXPERTS_PER_GROUP
MOE_TOPK = 2
MOE_ROWS = 256
DMA_LOOP_UNROLL = 8
MERGE_SUB_ROWS = 128
LANES = 128
NEG = -1e30

VMEM_LIMIT = 60 * 1024 * 1024


def _cparams(*sem):
    return pltpu.CompilerParams(dimension_semantics=sem, vmem_limit_bytes=VMEM_LIMIT)


def _mm(a, b, precision=None):
    return jnp.dot(a, b, precision=precision, preferred_element_type=F32)


def _nt(a, b, precision=None):
    return lax.dot_general(a, b, (((1,), (1,)), ((), ())), precision=precision,
                           preferred_element_type=F32)


def _eye(n, dtype):
    r = lax.broadcasted_iota(jnp.int32, (n, n), 0)
    c = lax.broadcasted_iota(jnp.int32, (n, n), 1)
    return jnp.where(r == c, 1.0, 0.0).astype(dtype)


def _silu(x):
    return x * jax.nn.sigmoid(x)


def _softplus(x):
    return jnp.maximum(x, 0.0) + jnp.log1p(jnp.exp(-jnp.abs(x)))


def _pack_bf16_pair(lo, hi):
    def rne(x):
        b = pltpu.bitcast(x, jnp.uint32)
        return b + jnp.uint32(0x7FFF) + ((b >> 16) & jnp.uint32(1))
    return (rne(hi) & jnp.uint32(0xFFFF0000)) | (rne(lo) >> 16)


def _unpack_bf16_pair(p):
    return pltpu.bitcast(p << 16, F32), pltpu.bitcast(p & jnp.uint32(0xFFFF0000), F32)


def _ada_body(c_ref, w_ref, b_ref, o_ref):
    cond = _silu(c_ref[...])
    w = w_ref[...]
    c_hi = cond.astype(BF16)
    c_lo = (cond - c_hi.astype(F32)).astype(BF16)
    w_hi = w.astype(BF16)
    w_lo = (w - w_hi.astype(F32)).astype(BF16)
    o_ref[...] = _mm(c_hi, w_hi) + _mm(c_lo, w_hi) + _mm(c_hi, w_lo) + b_ref[...]


def _ada(c, w, b):
    bsz, d = c.shape
    n = w.shape[1]
    tn = 1024
    return pl.pallas_call(
        _ada_body,
        grid=(n // tn,),
        in_specs=[pl.BlockSpec((bsz, d), lambda j: (0, 0)),
                  pl.BlockSpec((d, tn), lambda j: (0, j)),
                  pl.BlockSpec((1, tn), lambda j: (0, j))],
        out_specs=pl.BlockSpec((bsz, tn), lambda j: (0, j)),
        out_shape=jax.ShapeDtypeStruct((bsz, n), F32),
        compiler_params=_cparams("parallel"),
        name="ada",
    )(c, w, b)


def _prenorm_body(x_ref, g_ref, sc_ref, sh_ref, ws_ref, u_ref, os_ref, *, rc):
    gain = g_ref[...]
    scale = 1.0 + sc_ref[...]
    shift = sh_ref[...]

    def chunk(i, carry):
        r0 = pl.multiple_of(i * rc, rc)
        x = x_ref[pl.ds(r0, rc), :]
        y = x * lax.rsqrt(jnp.mean(x * x, axis=-1, keepdims=True) + NORM_EPS) * gain
        u = (y * scale + shift).astype(BF16)
        u_ref[pl.ds(r0, rc), :] = u
        os_ref[pl.ds(r0, rc), :] = _mm(u, ws_ref[...])
        return carry

    lax.fori_loop(0, x_ref.shape[0] // rc, chunk, 0)


def _prenorm(x2, gain, mod3, w_small, seq, *, sc_chunk, sh_chunk):
    t, d = x2.shape
    tm = min(1024, seq)
    per_b = seq // tm
    return pl.pallas_call(
        functools.partial(_prenorm_body, rc=min(256, tm)),
        grid=(t // tm,),
        in_specs=[pl.BlockSpec((tm, d), lambda i: (i, 0)),
                  pl.BlockSpec((1, d), lambda i: (0, 0)),
                  pl.BlockSpec((None, 1, d), lambda i: (i // per_b, 0, sc_chunk)),
                  pl.BlockSpec((None, 1, d), lambda i: (i // per_b, 0, sh_chunk)),
                  pl.BlockSpec((d, LANES), lambda i: (0, 0))],
        out_specs=[pl.BlockSpec((tm, d), lambda i: (i, 0)),
                   pl.BlockSpec((tm, LANES), lambda i: (i, 0))],
        out_shape=[jax.ShapeDtypeStruct((t, d), BF16),
                   jax.ShapeDtypeStruct((t, LANES), F32)],
        compiler_params=_cparams("parallel"),
        name="prenorm",
    )(x2, gain, mod3, mod3, w_small)


def _inproj_body(u_ref, wt_ref, o_ref, stage, w_s, sem, *, tn, n_a, b_row0, n_tiles):
    j = pl.program_id(0)

    def fetch(tile):
        row0 = jnp.where(tile < n_a, tile * tn, b_row0 + (tile - n_a) * tn)
        return pltpu.make_async_copy(wt_ref.at[pl.ds(pl.multiple_of(row0, 8), tn), :], stage, sem)

    @pl.when(pl.program_id(1) == 0)
    def _():
        @pl.when(j == 0)
        def _():
            fetch(j).start()

        fetch(j).wait()
        w_s[...] = stage[...].astype(BF16)

        @pl.when(j + 1 < n_tiles)
        def _():
            fetch(j + 1).start()

    o_ref[...] = _nt(u_ref[...], w_s[...]).astype(o_ref.dtype)


def _inproj(u, w_t, n_a, b_row0, n_b, *, out_first_b):
    t, d = u.shape
    tm = min(2048, t)
    tn = 1024
    assert b_row0 % 8 == 0 and b_row0 + n_b * tn <= w_t.shape[0]

    def out_col(j):
        if not out_first_b:
            return j
        return jnp.where(j < n_a, j + n_b, j - n_a)

    return pl.pallas_call(
        functools.partial(_inproj_body, tn=tn, n_a=n_a, b_row0=b_row0, n_tiles=n_a + n_b),
        grid=(n_a + n_b, t // tm),
        in_specs=[pl.BlockSpec((tm, d), lambda j, i: (i, 0)),
                  pl.BlockSpec(memory_space=pl.ANY)],
        out_specs=pl.BlockSpec((tm, tn), lambda j, i: (i, out_col(j))),
        out_shape=jax.ShapeDtypeStruct((t, (n_a + n_b) * tn), BF16),
        scratch_shapes=[pltpu.VMEM((tn, d), F32), pltpu.VMEM((tn, d), BF16), pltpu.SemaphoreType.DMA(())],
        compiler_params=_cparams("arbitrary", "arbitrary"),
        name="inproj",
    )(u, w_t)


def _moba_body(q_ref, k_ref, v_ref, cos_ref, sin_ref, o_ref, kr_s, vt_s, km_s, *, nblk, hp):
    blk = MOBA_BLOCK
    half = ROPE_DIM // 2
    lane = lax.broadcasted_iota(jnp.int32, (blk, HEAD_DIM), 1)
    eye_d = _eye(HEAD_DIM, BF16)
    eye_b = _eye(blk, BF16)
    scale = HEAD_DIM ** -0.5

    def rope(xf, rows):
        partner = jnp.where(lane < half, pltpu.roll(xf, HEAD_DIM - half, 1), pltpu.roll(xf, half, 1))
        return xf * cos_ref[rows, :] + partner * sin_ref[rows, :]

    def cols(h):
        return slice(h * HEAD_DIM, (h + 1) * HEAD_DIM)

    km_s[...] = jnp.zeros_like(km_s)
    for j in range(nblk):
        rows = slice(j * blk, (j + 1) * blk)
        for h in range(hp):
            kc = rope(k_ref[rows, cols(h)].astype(F32), rows)
            kr_s[rows, cols(h)] = kc.astype(BF16)
            km_s[j:j + 1, cols(h)] = jnp.mean(kc, axis=0, keepdims=True)
            vt_s[cols(h), rows] = _nt(eye_d, v_ref[rows, cols(h)]).astype(BF16)

    key_i = lax.broadcasted_iota(jnp.int32, (blk, blk), 0)
    qry_i = lax.broadcasted_iota(jnp.int32, (blk, blk), 1)
    blk_i = lax.broadcasted_iota(jnp.int32, (8, blk), 0)
    causal_bias = jnp.where(key_i <= qry_i, 0.0, NEG)

    def rows(j):
        return slice(j * blk, (j + 1) * blk)

    def begin(i, h):
        qc = rope(q_ref[rows(i), cols(h)].astype(F32), rows(i))
        st = dict(i=i, h=h, qs=(qc * scale).astype(BF16), scores=[], m=None, bias=None)
        if i > 0:
            g_t = _nt(km_s[:, cols(h)], qc, HI)
            rank = jnp.zeros((8, blk), F32)
            for jp in range(i):
                row = g_t[jp:jp + 1, :]
                beats = (row > g_t) | ((row == g_t) & (jp < blk_i))
                rank = rank + jnp.where(beats, 1.0, 0.0)
            sel = (rank < float(MOBA_TOPK)) & (blk_i < i)
            st["bias"] = jnp.where(sel, 0.0, NEG)
        return st

    def score(st, j):
        i = st["i"]
        s = _nt(kr_s[rows(j), cols(st["h"])], st["qs"])
        if j == i:
            s = s + causal_bias
        st["scores"].append(s)
        mj = jnp.max(s, axis=0, keepdims=True)
        if j < i:
            mj = mj + st["bias"][j:j + 1, :]
        st["m"] = mj if st["m"] is None else jnp.maximum(st["m"], mj)

    def accumulate(st, j):
        i = st["i"]
        if j == 0:
            st["den"] = jnp.zeros((1, blk), F32)
            st["acc"] = jnp.zeros((HEAD_DIM, blk), F32)
        shift = (st["bias"][j:j + 1, :] - st["m"]) if j < i else -st["m"]
        p = jnp.exp(st["scores"][j] + shift)
        st["den"] = st["den"] + jnp.sum(p, axis=0, keepdims=True)
        st["acc"] = st["acc"] + _mm(vt_s[cols(st["h"]), rows(j)], p.astype(BF16))
        if j == i:
            o_t = (st["acc"] / st["den"]).astype(BF16)
            o_ref[rows(i), cols(st["h"])] = _nt(eye_b, o_t).astype(o_ref.dtype)

    prev = None
    for i in range(nblk):
        cur = [begin(i, h) for h in range(hp)]
        for j in range(i + 1):
            for h in range(hp):
                score(cur[h], j)
                if prev is not None and j <= i - 1:
                    accumulate(prev[h], j)
        prev = cur
    for j in range(nblk):
        for h in range(hp):
            accumulate(prev[h], j)


def _moba(proj, cos_t, sin_t, *, q_blk0, k_blk0, v_blk0):
    bsz, seq, _ = proj.shape
    nblk = seq // MOBA_BLOCK
    assert seq % MOBA_BLOCK == 0 and 1 <= nblk <= 8
    hd = HEAD_DIM
    hp = MOBA_HEADS_PER_STEP
    wide = hp * hd
    assert MOBA_HEADS % hp == 0 and all(b0 % hp == 0 for b0 in (q_blk0, k_blk0, v_blk0))
    return pl.pallas_call(
        functools.partial(_moba_body, nblk=nblk, hp=hp),
        grid=(bsz, MOBA_HEADS // hp),
        in_specs=[pl.BlockSpec((None, seq, wide), lambda b, h: (b, 0, q_blk0 // hp + h)),
                  pl.BlockSpec((None, seq, wide), lambda b, h: (b, 0, k_blk0 // hp + h)),
                  pl.BlockSpec((None, seq, wide), lambda b, h: (b, 0, v_blk0 // hp + h)),
                  pl.BlockSpec((None, seq, hd), lambda b, h: (b, 0, 0)),
                  pl.BlockSpec((None, seq, hd), lambda b, h: (b, 0, 0))],
        out_specs=pl.BlockSpec((None, seq, wide), lambda b, h: (b, 0, h)),
        out_shape=jax.ShapeDtypeStruct((bsz, seq, MOBA_HEADS * hd), BF16),
        scratch_shapes=[pltpu.VMEM((seq, wide), BF16),
                        pltpu.VMEM((wide, seq), BF16),
                        pltpu.VMEM((8, wide), F32)],
        compiler_params=_cparams("parallel", "parallel"),
        name="moba",
    )(proj, proj, proj, cos_t, sin_t)


def _chunk_masks(n, chunk):
    r = lax.broadcasted_iota(jnp.int32, (n, n), 0)
    c = lax.broadcasted_iota(jnp.int32, (n, n), 1)
    shift = int(math.log2(chunk))
    same = jnp.right_shift(r, shift) == jnp.right_shift(c, shift)
    return r, c, same


def _gates_body(ba_ref, par_ref, beta_ref, g_ref, gl_ref, *, seq):
    grp = DN_GROUP
    r, c, same = _chunk_masks(grp, DN_CHUNK)
    low = jnp.where(same & (c <= r), 1.0, 0.0)
    ones = jnp.where(same, 1.0, 0.0)
    neg_a = -jnp.exp(par_ref[0:1, :])
    dt_b = par_ref[1:2, :]
    for i in range(seq // grp):
        rows = slice(i * grp, (i + 1) * grp)
        x = ba_ref[rows, :].astype(F32)
        beta_ref[rows, :] = jax.nn.sigmoid(x)
        g = neg_a * _softplus(x + dt_b)
        g_ref[rows, :] = _mm(low, g, HI)
        gl_ref[rows, :] = _mm(ones, g, HI)


def _gates(ba, par):
    bsz, seq, _ = ba.shape
    spec = pl.BlockSpec((None, seq, LANES), lambda b: (b, 0, 0))
    return pl.pallas_call(
        functools.partial(_gates_body, seq=seq),
        grid=(bsz,),
        in_specs=[spec, pl.BlockSpec((8, LANES), lambda b: (0, 0))],
        out_specs=[spec, spec, spec],
        out_shape=[jax.ShapeDtypeStruct((bsz, seq, LANES), F32)] * 3,
        compiler_params=_cparams("parallel"),
        name="dn_gates",
    )(ba, par)


def _dn_body(q_ref, k_ref, v_ref, z_ref, cwq_ref, cwk_ref, cwv_ref, beta_ref, g_ref, gl_ref, grow_ref,
             gain_ref, o_ref, stage_s, qkv_a, qkv_b, state_s, vnew_s, *bufs, seq, hp):
    bufs_a, bufs_b = bufs[:5], bufs[5:]
    hg = pl.program_id(1)
    grp = DN_GROUP
    chunk = DN_CHUNK
    ngrp = seq // grp
    hd = HEAD_DIM
    pad = DN_CONV_PAD

    crows = DN_CONV_ROWS
    r, c, same = _chunk_masks(grp, chunk)
    low_incl = same & (c <= r)
    low_strict = same & (c < r)
    eye_g = jnp.where(r == c, 1.0, 0.0)
    eye_d = _eye(hd, BF16)
    lane = lax.broadcasted_iota(jnp.int32, (grp, LANES), 1)
    col_chunk = jnp.right_shift(lax.broadcasted_iota(jnp.int32, (hd, grp), 1), int(math.log2(chunk)))
    gain = gain_ref[...]
    n_double = int(math.log2(chunk)) - 1

    def pick(ref, r0, lane_id):
        return jnp.sum(jnp.where(lane == lane_id, ref[pl.ds(r0, grp), :], 0.0), axis=-1, keepdims=True)

    state_s[...] = jnp.zeros_like(state_s)
    heads = range(hp)

    cols = [slice(hh * hd, (hh + 1) * hd) for hh in heads]
    head = [hg * hp + hh for hh in heads]

    tensors = ((q_ref, cwq_ref, True, hd ** -0.5), (k_ref, cwk_ref, True, None), (v_ref, cwv_ref, False, None))
    nsub = grp // crows

    def conv(gi, qkv, first=False):
        r0 = pl.multiple_of(gi * grp, grp)
        for ti, (src, cw_ref, l2, post) in enumerate(tensors):
            cw = cw_ref[...]
            for sb in range(nsub):
                stage_s[ti, sb, pad:pad + crows, :] = src[pl.ds(r0 + sb * crows, crows), :].astype(F32)
            if first:
                stage_s[ti, 0, 0:pad, :] = jnp.zeros((pad, hp * hd), F32)
            else:
                before = src[pl.ds(pl.multiple_of(r0 - 2 * pad, 2 * pad), 2 * pad), :].astype(F32)
                stage_s[ti, 0, 0:pad, :] = before[pad:, :]
            for sb in range(1, nsub):
                stage_s[ti, sb, 0:pad, :] = stage_s[ti, sb - 1, crows:crows + pad, :]
            for sb in range(nsub):
                for hh in heads:
                    acc = None
                    for j in range(DN_CONV_WIDTH):
                        off = pad - (DN_CONV_WIDTH - 1) + j
                        term = stage_s[ti, sb, off:off + crows, cols[hh]] * cw[j:j + 1, cols[hh]]
                        acc = term if acc is None else acc + term
                    yh = _silu(acc)
                    if l2:
                        yh = yh * lax.rsqrt(jnp.sum(yh * yh, axis=-1, keepdims=True) + NORM_EPS)
                    if post is not None:
                        yh = yh * post
                    qkv[ti, sb * crows:(sb + 1) * crows, cols[hh]] = yh
                yield

    def prepare(gi, qkv, bufs):
        u_r, w_r, qk_r, qd_r, kt_r = bufs
        r0 = pl.multiple_of(gi * grp, grp)
        q = [qkv[0, :, cols[hh]] for hh in heads]
        k = [qkv[1, :, cols[hh]] for hh in heads]
        v = [qkv[2, :, cols[hh]] for hh in heads]
        beta = [pick(beta_ref, r0, head[hh]) for hh in heads]
        g_col = [pick(g_ref, r0, DN_HEADS + head[hh]) for hh in heads]
        gl_col = [pick(gl_ref, r0, DN_HEADS + head[hh]) for hh in heads]
        decay = [jnp.exp(jnp.minimum(g_col[hh] - grow_ref[hh, gi], 0.0)) for hh in heads]
        e_g = [jnp.exp(g_col[hh]) for hh in heads]
        kb = [k[hh] * beta[hh] for hh in heads]
        vb = [(v[hh] * beta[hh]).astype(BF16) for hh in heads]
        k16 = [k[hh].astype(BF16) for hh in heads]
        n_mat = [jnp.where(low_strict, _nt(kb[hh].astype(BF16), k16[hh]) * decay[hh], 0.0) for hh in heads]
        p = [(-n_mat[hh]).astype(BF16) for hh in heads]
        x = [eye_g - n_mat[hh] for hh in heads]
        qk = [jnp.where(low_incl, _nt(q[hh].astype(BF16), k16[hh]) * decay[hh], 0.0).astype(BF16) for hh in heads]
        for hh in heads:
            qk_r[hh] = qk[hh]
            qd_r[hh] = (q[hh] * e_g[hh]).astype(BF16)
            kt_r[hh] = _nt(eye_d, (k[hh] * jnp.exp(gl_col[hh] - g_col[hh])).astype(BF16)).astype(BF16)
        yield
        for _ in range(n_double):
            p2 = [_mm(p[hh], p[hh]).astype(BF16) for hh in heads]
            x = [x[hh] + _mm(x[hh].astype(BF16), p2[hh]) for hh in heads]
            p = p2
            yield
        x16 = [x[hh].astype(BF16) for hh in heads]
        for hh in heads:
            u_r[hh] = _mm(x16[hh], vb[hh])
            w_r[hh] = _mm(x16[hh], (kb[hh] * e_g[hh]).astype(BF16)).astype(BF16)
        yield

    def scan(gi, bufs):
        u_r, w_r, qk_r, qd_r, kt_r = bufs
        r0 = pl.multiple_of(gi * grp, grp)
        vnew_s[...] = jnp.zeros_like(vnew_s)
        for ci in range(grp // chunk):
            rows = slice(ci * chunk, (ci + 1) * chunk)
            gl_row = jnp.exp(gl_ref[pl.ds(r0 + ci * chunk, 1), :])
            for hh in heads:
                state = state_s[hh]
                s16 = state.astype(BF16)
                v_new = u_r[hh, rows, :] - _mm(w_r[hh, rows, :], s16)
                vnew_s[hh, rows, :] = v_new.astype(BF16)
                o = _mm(qd_r[hh, rows, :], s16) + _mm(qk_r[hh, rows, :], vnew_s[hh])
                chunk_decay = jnp.sum(jnp.where(lane[0:1, :] == DN_HEADS + head[hh], gl_row, 0.0),
                                      axis=-1, keepdims=True)
                k_tail_t = kt_r[hh]
                kt = jnp.where(col_chunk == ci, k_tail_t, jnp.zeros_like(k_tail_t))
                state_s[hh] = state * chunk_decay + _mm(kt, vnew_s[hh])
                on = o * lax.rsqrt(jnp.mean(o * o, axis=-1, keepdims=True) + NORM_EPS) * gain
                zz = z_ref[pl.ds(r0 + ci * chunk, chunk), cols[hh]].astype(F32)
                o_ref[pl.ds(r0 + ci * chunk, chunk), cols[hh]] = (on * _silu(zz)).astype(o_ref.dtype)
            yield

    def interleave(first, *others):
        for _ in first:
            for _ in range(2):
                for steps in others:
                    next(steps, None)
        for steps in others:
            for _ in steps:
                pass

    interleave(conv(0, qkv_a, first=True))
    interleave(prepare(0, qkv_a, bufs_a), conv(1, qkv_b))

    def pair(pi, carry):
        g0 = 2 * pi
        interleave(scan(g0, bufs_a), prepare(g0 + 1, qkv_b, bufs_b), conv(g0 + 2, qkv_a))
        interleave(scan(g0 + 1, bufs_b), prepare(g0 + 2, qkv_a, bufs_a), conv(g0 + 3, qkv_b))
        return carry

    lax.fori_loop(0, ngrp // 2 - 1, pair, 0)
    interleave(scan(ngrp - 2, bufs_a), prepare(ngrp - 1, qkv_b, bufs_b))
    interleave(scan(ngrp - 1, bufs_b))


def _deltanet(proj, conv_w, beta, gcum, glast, grow, gain, *, q_blk0, k_blk0, v_blk0, z_blk0):
    bsz, seq, _ = proj.shape
    hd = HEAD_DIM
    nh = DN_HEADS
    assert seq % DN_GROUP == 0
    ngrp = seq // DN_GROUP

    hp = DN_HEADS_PER_STEP
    wide = hp * hd
    assert nh % hp == 0 and all(b0 % hp == 0 for b0 in (q_blk0, k_blk0, v_blk0, z_blk0))

    def col(blk0):
        return pl.BlockSpec((None, seq, wide), lambda b, h: (b, 0, blk0 // hp + h))

    def cw(blk0):
        return pl.BlockSpec((DN_CONV_WIDTH, wide), lambda b, h: (0, blk0 // hp + h))

    full = pl.BlockSpec((None, seq, LANES), lambda b, h: (b, 0, 0))
    return pl.pallas_call(
        functools.partial(_dn_body, seq=seq, hp=hp),
        grid=(bsz, nh // hp),
        in_specs=[col(q_blk0), col(k_blk0), col(v_blk0), col(z_blk0),
                  cw(0), cw(nh), cw(2 * nh),
                  full, full, full,
                  pl.BlockSpec((hp, ngrp, 1, DN_GROUP), lambda b, h: (b * (nh // hp) + h, 0, 0, 0)),
                  pl.BlockSpec((1, hd), lambda b, h: (0, 0))],
        out_specs=pl.BlockSpec((None, seq, wide), lambda b, h: (b, 0, h)),
        out_shape=jax.ShapeDtypeStruct((bsz, seq, nh * hd), BF16),
        scratch_shapes=[pltpu.VMEM((3, DN_GROUP // DN_CONV_ROWS, DN_CONV_ROWS + DN_CONV_PAD, wide), F32)]
        + [pltpu.VMEM((3, DN_GROUP, wide), F32)] * 2
        + [pltpu.VMEM((hp, hd, hd), F32), pltpu.VMEM((hp, DN_GROUP, hd), BF16)]
        + [pltpu.VMEM((hp, DN_GROUP, hd), F32), pltpu.VMEM((hp, DN_GROUP, hd), BF16),
           pltpu.VMEM((hp, DN_GROUP, DN_GROUP), BF16), pltpu.VMEM((hp, DN_GROUP, hd), BF16),
           pltpu.VMEM((hp, hd, DN_GROUP), BF16)] * 2,
        compiler_params=_cparams("parallel", "parallel"),
        name="deltanet",
    )(proj, proj, proj, proj, conv_w, conv_w, conv_w, beta, gcum, glast, grow, gain)


def _merge_body(ya_ref, yb_ref, ga_ref, gb_ref, x_ref, wm_ref, wd_ref, wo_ref, wr_ref, br_ref,
                npost_ref, npre_ref, gt_ref, sc_ref, sh_ref, x1_ref, u2_ref, lg_ref):
    sub = MERGE_SUB_ROWS
    for s in range(x_ref.shape[0] // sub):
        rows = slice(s * sub, (s + 1) * sub)
        ma = _mm(ya_ref[rows, :], wm_ref[...])
        mb = _mm(yb_ref[rows, :], wd_ref[...])
        merged = (jax.nn.sigmoid(ga_ref[rows, :].astype(F32)) * ma
                  + jax.nn.sigmoid(gb_ref[rows, :].astype(F32)) * mb)
        y = _mm(merged.astype(BF16), wo_ref[...])
        yn = y * lax.rsqrt(jnp.mean(y * y, axis=-1, keepdims=True) + NORM_EPS) * npost_ref[...]
        x1 = x_ref[rows, :] + gt_ref[...] * yn
        x1_ref[rows, :] = x1
        un = x1 * lax.rsqrt(jnp.mean(x1 * x1, axis=-1, keepdims=True) + NORM_EPS) * npre_ref[...]
        u2 = un * (1.0 + sc_ref[...]) + sh_ref[...]
        half = u2.shape[1] // 2
        u2_ref[rows, :] = _pack_bf16_pair(u2[:, :half], u2[:, half:])
        u_hi = u2.astype(BF16)
        u_lo = (u2 - u_hi.astype(F32)).astype(BF16)
        lg2 = _nt(wr_ref[...], u_hi)
        lg_ref[:, rows] = lg2[:LANES, :] + lg2[LANES:, :] + _nt(wr_ref[:LANES, :], u_lo) + br_ref[...]


def _merge(ya, yb, proj2, x2, wm, wd, wo, wr, br, npost, npre, mod3, seq, *, ga_blk, gb_blk):
    t, d = x2.shape
    wa = ya.shape[1]
    tm = min(256, seq)
    per_b = seq // tm
    const = lambda i: (0, 0)
    once = dict(pipeline_mode=pl.Buffered(1))

    def modspec(chunk):
        return pl.BlockSpec((None, 1, d), lambda i: (i // per_b, 0, chunk))

    return pl.pallas_call(
        _merge_body,
        grid=(t // tm,),
        in_specs=[pl.BlockSpec((tm, wa), lambda i: (i, 0)),
                  pl.BlockSpec((tm, wa), lambda i: (i, 0)),
                  pl.BlockSpec((tm, d), lambda i: (i, ga_blk)),
                  pl.BlockSpec((tm, d), lambda i: (i, gb_blk)),
                  pl.BlockSpec((tm, d), lambda i: (i, 0)),
                  pl.BlockSpec((wa, d), const, **once),
                  pl.BlockSpec((wa, d), const, **once),
                  pl.BlockSpec((d, d), const, **once),
                  pl.BlockSpec((2 * LANES, d), const, **once),
                  pl.BlockSpec((LANES, 1), const),
                  pl.BlockSpec((1, d), const),
                  pl.BlockSpec((1, d), const),
                  modspec(2), modspec(4), modspec(3)],
        out_specs=[pl.BlockSpec((tm, d), lambda i: (i, 0)),
                   pl.BlockSpec((tm, d // 2), lambda i: (i, 0)),
                   pl.BlockSpec((LANES, tm), lambda i: (0, i))],
        out_shape=[jax.ShapeDtypeStruct((t, d), F32),
                   jax.ShapeDtypeStruct((t, d // 2), jnp.uint32),
                   jax.ShapeDtypeStruct((LANES, t), F32)],
        compiler_params=_cparams("parallel"),
        name="merge",
    )(ya, yb, proj2, proj2, x2, wm, wd, wo, wr, br, npost, npre, mod3, mod3, mod3)


def _route_body(lg_ref, info_ref, col_ref, cnt_ref, run_s, *, tr):
    @pl.when(pl.program_id(0) == 0)
    def _():
        run_s[...] = jnp.zeros_like(run_s)

    lg = lg_ref[...]
    row = lax.broadcasted_iota(jnp.int32, (LANES, tr), 0)
    row_f = row.astype(F32)
    big = float(LANES)

    def first_max(vals, mask):
        mx = jnp.max(jnp.where(mask, vals, NEG), axis=0, keepdims=True)
        idx = jnp.min(jnp.where(mask & (vals == mx), row_f, big), axis=0, keepdims=True)
        return mx, idx

    gmask = row < MOE_GROUPS
    gmax, gidx = first_max(lg, gmask)
    p_group = 1.0 / jnp.sum(jnp.where(gmask, jnp.exp(lg - gmax), 0.0), axis=0, keepdims=True)
    lo = float(MOE_GROUPS) + gidx * float(MOE_EXPERTS_PER_GROUP)
    emask = (row_f >= lo) & (row_f < lo + float(MOE_EXPERTS_PER_GROUP))
    m1, i1 = first_max(lg, emask)
    m2, i2 = first_max(lg, emask & (row_f != i1))
    e2 = jnp.exp(m2 - m1)
    w1 = p_group / (1.0 + e2)
    w2 = p_group * e2 / (1.0 + e2)
    oh1 = row_f == i1
    oh2 = row_f == i2
    oh = jnp.where(oh1 | oh2, 1.0, 0.0).astype(BF16)
    r = lax.broadcasted_iota(jnp.int32, (tr, tr), 0)
    c = lax.broadcasted_iota(jnp.int32, (tr, tr), 1)
    before = jnp.where(r < c, 1.0, 0.0).astype(BF16)
    prefix = _mm(oh, before) + run_s[:, 0:1]
    rank1 = jnp.sum(jnp.where(oh1, prefix, 0.0), axis=0, keepdims=True)
    rank2 = jnp.sum(jnp.where(oh2, prefix, 0.0), axis=0, keepdims=True)
    run_s[...] = run_s[...] + jnp.sum(oh.astype(F32), axis=1, keepdims=True)
    goff = float(MOE_GROUPS)
    info = jnp.where(row == 0, i1 - goff, 0.0)
    info = jnp.where(row == 1, i2 - goff, info)
    info = jnp.where(row == 2, rank1, info)
    info = jnp.where(row == 3, rank2, info)
    info = jnp.where(row == 4, w1, info)
    info = jnp.where(row == 5, w2, info)
    info_ref[...] = info[0:8, :]
    col_ref[...] = _nt(_eye(tr, F32), info, HI)
    cnt_ref[...] = run_s[...]


def _route(logits_t):
    t = logits_t.shape[1]
    tr = min(256, t)
    return pl.pallas_call(
        functools.partial(_route_body, tr=tr),
        grid=(t // tr,),
        in_specs=[pl.BlockSpec((LANES, tr), lambda i: (0, i))],
        out_specs=[pl.BlockSpec((8, tr), lambda i: (0, i)),
                   pl.BlockSpec((tr, LANES), lambda i: (i, 0)),
                   pl.BlockSpec((LANES, LANES), lambda i: (0, 0))],
        out_shape=[jax.ShapeDtypeStruct((8, t), F32),
                   jax.ShapeDtypeStruct((t, LANES), F32),
                   jax.ShapeDtypeStruct((LANES, LANES), F32)],
        scratch_shapes=[pltpu.VMEM((LANES, LANES), F32)],
        compiler_params=_cparams("arbitrary"),
        name="route",
    )(logits_t)


def _dispatch_body(d0_ref, d1_ref, u_ref, xs_ref, sem, *, td):
    dests = (d0_ref, d1_ref)

    def row_copy(r, k):
        return pltpu.make_async_copy(u_ref.at[pl.ds(r, 1), :], xs_ref.at[pl.ds(dests[k][r], 1), :], sem)

    def start(r, carry):
        for k in range(MOE_TOPK):
            row_copy(r, k).start()
        return carry

    def wait(r, carry):
        for k in range(MOE_TOPK):
            row_copy(r, k).wait()
        return carry

    lax.fori_loop(0, td, start, 0, unroll=DMA_LOOP_UNROLL)
    lax.fori_loop(0, td, wait, 0, unroll=DMA_LOOP_UNROLL)


def _dispatch(dest, u2, n_rows):
    t, d = u2.shape
    td = min(1024, t)
    nt = t // td
    return pl.pallas_call(
        functools.partial(_dispatch_body, td=td),
        grid=(nt,),
        in_specs=[pl.BlockSpec((td,), lambda i: (i,), memory_space=pltpu.SMEM),
                  pl.BlockSpec((td,), lambda i: (nt + i,), memory_space=pltpu.SMEM),
                  pl.BlockSpec((td, d), lambda i: (i, 0))],
        out_specs=pl.BlockSpec(memory_space=pl.ANY),
        out_shape=jax.ShapeDtypeStruct((n_rows, d), u2.dtype),
        scratch_shapes=[pltpu.SemaphoreType.DMA(())],
        compiler_params=_cparams("arbitrary"),
        name="dispatch",
    )(dest, dest, u2)


def _experts_body(be_ref, nx_ref, nv_ref, nu_ref, x_ref, wg_ref, wu_ref, wd_ref, o_ref,
                  stage_g, stage_u, stage_d, wg_s, wu_s, wd_s, sems):
    i = pl.program_id(0)

    def fetch(e):
        return (pltpu.make_async_copy(wg_ref.at[e], stage_g, sems.at[0]),
                pltpu.make_async_copy(wu_ref.at[e], stage_u, sems.at[1]),
                pltpu.make_async_copy(wd_ref.at[e], stage_d, sems.at[2]))

    @pl.when(i == 0)
    def _():
        for cp in fetch(be_ref[0]):
            cp.start()

    @pl.when(jnp.logical_or(i == 0, be_ref[i] != be_ref[jnp.maximum(i - 1, 0)]))
    def _():
        for cp in fetch(be_ref[i]):
            cp.wait()
        wg_s[...] = stage_g[...].astype(BF16)
        wu_s[...] = stage_u[...].astype(BF16)
        wd_s[...] = stage_d[...].astype(BF16)

        @pl.when(nx_ref[i] >= 0)
        def _():
            for cp in fetch(nx_ref[i]):
                cp.start()

    @pl.when(i < nu_ref[0])
    def _():
        row = lax.broadcasted_iota(jnp.int32, (x_ref.shape[0], 1), 0)
        packed = jnp.where(row < nv_ref[i], x_ref[...], jnp.uint32(0))
        x_lo, x_hi = (v.astype(BF16) for v in _unpack_bf16_pair(packed))
        half = packed.shape[1]
        gate = _mm(x_lo, wg_s[0:half, :]) + _mm(x_hi, wg_s[half:, :])
        up = _mm(x_lo, wu_s[0:half, :]) + _mm(x_hi, wu_s[half:, :])
        y = _mm((_silu(gate) * up).astype(BF16), wd_s[...])
        o_ref[...] = _pack_bf16_pair(y[:, :half], y[:, half:])

    @pl.when(i >= nu_ref[0])
    def _():
        o_ref[...] = jnp.zeros_like(o_ref)


def _experts(block_expert, next_expert, block_valid, n_used, xs, wg, wu, wd):
    nr, dp = xs.shape
    d, ff = wg.shape[1], wg.shape[2]
    assert d == 2 * dp
    rb = MOE_ROWS
    row_map = lambda i, be, nx, nv, nu: (jnp.minimum(i, nu[0] - 1), 0)
    hbm = pl.BlockSpec(memory_space=pl.ANY)
    grid_spec = pltpu.PrefetchScalarGridSpec(
        num_scalar_prefetch=4,
        grid=(nr // rb,),
        in_specs=[pl.BlockSpec((rb, dp), row_map), hbm, hbm, hbm],
        out_specs=pl.BlockSpec((rb, dp), lambda i, be, nx, nv, nu: (i, 0)),
        scratch_shapes=[pltpu.VMEM((d, ff), F32), pltpu.VMEM((d, ff), F32), pltpu.VMEM((ff, d), F32),
                        pltpu.VMEM((d, ff), BF16), pltpu.VMEM((d, ff), BF16), pltpu.VMEM((ff, d), BF16),
                        pltpu.SemaphoreType.DMA((3,))],
    )
    return pl.pallas_call(
        _experts_body,
        grid_spec=grid_spec,
        out_shape=jax.ShapeDtypeStruct((nr, dp), jnp.uint32),
        compiler_params=_cparams("arbitrary"),
        name="experts",
    )(block_expert, next_expert, block_valid, n_used, xs, wg, wu, wd)


def _combine_body(d0_ref, d1_ref, n0_ref, n1_ref, info_ref, x1_ref, npost_ref, gt_ref, yb_ref, o_ref,
                  buf, sems, *, tc, rc, nt):
    i = pl.program_id(0)
    slot = lax.rem(i, 2)
    own = (d0_ref, d1_ref)
    ahead = (n0_ref, n1_ref)

    def row_copy(dests, sl, r, k):
        return pltpu.make_async_copy(yb_ref.at[pl.ds(dests[k][r], 1), :], buf.at[sl, k, pl.ds(r, 1), :],
                                     sems.at[sl])

    @pl.when(i == 0)
    def _():
        def start(r, carry):
            for k in range(MOE_TOPK):
                row_copy(own, slot, r, k).start()
            return carry

        lax.fori_loop(0, tc, start, 0, unroll=DMA_LOOP_UNROLL)

    def wait(r, carry):
        for k in range(MOE_TOPK):
            row_copy(own, slot, r, k).wait()
        return carry

    lax.fori_loop(0, tc, wait, 0, unroll=DMA_LOOP_UNROLL)

    half = buf.shape[3]
    gain_lo, gain_hi = npost_ref[:, :half], npost_ref[:, half:]
    gate_lo, gate_hi = gt_ref[:, :half], gt_ref[:, half:]

    def chunk(ci, carry, request_next):
        r0 = pl.multiple_of(ci * rc, rc)
        info = info_ref[pl.ds(r0, rc), :]
        w0, w1 = info[:, 4:5], info[:, 5:6]
        a_lo, a_hi = _unpack_bf16_pair(buf[slot, 0, pl.ds(r0, rc), :])
        b_lo, b_hi = _unpack_bf16_pair(buf[slot, 1, pl.ds(r0, rc), :])
        y_lo = w0 * a_lo + w1 * b_lo
        y_hi = w0 * a_hi + w1 * b_hi
        ms = (jnp.sum(y_lo * y_lo, axis=-1, keepdims=True)
              + jnp.sum(y_hi * y_hi, axis=-1, keepdims=True)) * (1.0 / (2 * half))
        inv = lax.rsqrt(ms + NORM_EPS)
        o_ref[pl.ds(r0, rc), :half] = x1_ref[pl.ds(r0, rc), :half] + gate_lo * (y_lo * inv * gain_lo)
        o_ref[pl.ds(r0, rc), half:] = x1_ref[pl.ds(r0, rc), half:] + gate_hi * (y_hi * inv * gain_hi)
        if request_next:
            for rr in range(rc):
                for k in range(MOE_TOPK):
                    row_copy(ahead, 1 - slot, r0 + rr, k).start()
        return carry

    @pl.when(i + 1 < nt)
    def _():
        lax.fori_loop(0, tc // rc, functools.partial(chunk, request_next=True), 0)

    @pl.when(i + 1 == nt)
    def _():
        lax.fori_loop(0, tc // rc, functools.partial(chunk, request_next=False), 0)


def _combine(dest, info, x1, npost, mod3, yb, seq):
    t, d = x1.shape
    tc = min(1024, seq)
    rc = min(128, tc)
    per_b = seq // tc
    nt = t // tc
    nxt = lambda i: jnp.minimum(i + 1, nt - 1)
    return pl.pallas_call(
        functools.partial(_combine_body, tc=tc, rc=rc, nt=nt),
        grid=(nt,),
        in_specs=[pl.BlockSpec((tc,), lambda i: (i,), memory_space=pltpu.SMEM),
                  pl.BlockSpec((tc,), lambda i: (nt + i,), memory_space=pltpu.SMEM),
                  pl.BlockSpec((tc,), lambda i: (nxt(i),), memory_space=pltpu.SMEM),
                  pl.BlockSpec((tc,), lambda i: (nt + nxt(i),), memory_space=pltpu.SMEM),
                  pl.BlockSpec((tc, LANES), lambda i: (i, 0)),
                  pl.BlockSpec((tc, d), lambda i: (i, 0)),
                  pl.BlockSpec((1, d), lambda i: (0, 0)),
                  pl.BlockSpec((None, 1, d), lambda i: (i // per_b, 0, 5)),
                  pl.BlockSpec(memory_space=pl.ANY)],
        out_specs=pl.BlockSpec((tc, d), lambda i: (i, 0)),
        out_shape=jax.ShapeDtypeStruct((t, d), F32),
        scratch_shapes=[pltpu.VMEM((2, MOE_TOPK, tc, d // 2), jnp.uint32), pltpu.SemaphoreType.DMA((2,))],
        compiler_params=_cparams("arbitrary"),
        name="combine",
    )(dest, dest, dest, dest, info, x1, npost, mod3, yb)


def _rope_tables(positions):
    half = ROPE_DIM // 2
    inv_freq = jnp.power(ROPE_THETA, -jnp.arange(half, dtype=F32) * (2.0 / ROPE_DIM))
    ang = inv_freq[None, :, None] * positions.astype(F32)[:, None, :]
    cos, sin = jnp.cos(ang), jnp.sin(ang)
    rest = (positions.shape[0], HEAD_DIM - ROPE_DIM, positions.shape[1])
    cos_t = jnp.concatenate([cos, cos, jnp.ones(rest, F32)], axis=1)
    sin_t = jnp.concatenate([-sin, sin, jnp.zeros(rest, F32)], axis=1)
    return jnp.transpose(cos_t, (0, 2, 1)), jnp.transpose(sin_t, (0, 2, 1))


def _pad_lanes(v, n=LANES):
    return jnp.pad(v, [(0, 0)] * (v.ndim - 1) + [(0, n - v.shape[-1])])


def _mixer_and_router(x, mod3, positions, norm_mix_pre, norm_mix_post, norm_ffn_pre, w_in, conv_w,
                      dn_a_log, dn_dt_bias, dn_out_norm, w_branch_moba, w_branch_delta, w_out,
                      router_group_w, router_group_b, router_expert_w, router_expert_b):
    bsz, seq, d = x.shape
    t = bsz * seq
    mw = MOBA_HEADS * HEAD_DIM
    dw = DN_HEADS * HEAD_DIM
    o_qa, o_ka, o_va = 0, mw, 2 * mw
    o_dn = 3 * mw
    o_z = o_dn + 3 * dw
    o_ba = o_z + dw
    o_ga = o_ba + 2 * DN_HEADS
    o_gb = o_ga + d
    w_small = _pad_lanes(w_in[:, o_ba:o_ga]).astype(BF16)
    x2 = x.reshape(t, d)
    u, ba2 = _prenorm(x2, norm_mix_pre[None, :], mod3, w_small, seq, sc_chunk=1, sh_chunk=0)
    assert o_ba % 1024 == 0 and (2 * d) % 1024 == 0
    proj2 = _inproj(u, w_in.T, o_ba // 1024, o_ga, 2 * d // 1024, out_first_b=True)
    proj = proj2.reshape(bsz, seq, -1)
    c0 = 2 * d // LANES
    nh = MOBA_HEADS

    cos_t, sin_t = _rope_tables(positions)
    ya = _moba(proj, cos_t, sin_t, q_blk0=c0, k_blk0=c0 + nh, v_blk0=c0 + 2 * nh)

    par = jnp.zeros((8, LANES), F32)
    par = par.at[0, DN_HEADS:2 * DN_HEADS].set(dn_a_log.astype(F32))
    par = par.at[1, DN_HEADS:2 * DN_HEADS].set(dn_dt_bias.astype(F32))
    beta, gcum, glast = _gates(ba2.reshape(bsz, seq, LANES), par)
    ngrp = seq // DN_GROUP
    grow = jnp.transpose(gcum[:, :, DN_HEADS:2 * DN_HEADS], (0, 2, 1)).reshape(bsz * DN_HEADS, ngrp, 1, DN_GROUP)
    d0 = c0 + 3 * nh
    yb = _deltanet(proj, conv_w, beta, gcum, glast, grow, dn_out_norm[None, :].astype(F32),
                   q_blk0=d0, k_blk0=d0 + DN_HEADS, v_blk0=d0 + 2 * DN_HEADS, z_blk0=d0 + 3 * DN_HEADS)

    wr = _pad_lanes(jnp.concatenate([router_group_w, router_expert_w], axis=1)).T
    wr_hi = wr.astype(BF16)
    wr = jnp.concatenate([wr_hi, (wr - wr_hi.astype(F32)).astype(BF16)], axis=0)
    br = _pad_lanes(jnp.concatenate([router_group_b, router_expert_b])[None, :]).T
    return _merge(ya.reshape(t, mw), yb.reshape(t, dw), proj2, x2,
                  w_branch_moba.astype(BF16), w_branch_delta.astype(BF16), w_out.astype(BF16), wr, br,
                  norm_mix_post[None, :], norm_ffn_pre[None, :], mod3, seq, ga_blk=0, gb_blk=1)


def _moe(x1, u2, logits, mod3, norm_ffn_post, w_gate, w_up, w_down, seq):
    t, d = x1.shape
    info_t, info, counts = _route(logits)
    counts = counts[MOE_GROUPS:MOE_GROUPS + MOE_EXPERTS, 0].astype(jnp.int32)
    rb = MOE_ROWS
    padded = (counts + rb - 1) // rb * rb
    pad_end = jnp.cumsum(padded)
    pad_start = pad_end - padded
    eid = info_t[0:MOE_TOPK].astype(jnp.int32)
    rank = info_t[MOE_TOPK:2 * MOE_TOPK].astype(jnp.int32)
    experts = jnp.arange(MOE_EXPERTS, dtype=jnp.int32)[:, None, None]
    start = jnp.sum(jnp.where(eid[None] == experts, pad_start[:, None, None], 0), axis=0)
    dest_flat = (start + rank).reshape(-1)
    n_blocks = (t * MOE_TOPK + MOE_EXPERTS * (rb - 1)) // rb + 1
    n_used = (pad_end[-1] // rb).astype(jnp.int32)
    blk_row = jnp.minimum(jnp.arange(n_blocks, dtype=jnp.int32), n_used - 1) * rb
    block_expert = jnp.minimum(jnp.sum(pad_end[None, :] <= blk_row[:, None], axis=1),
                               MOE_EXPERTS - 1).astype(jnp.int32)
    ids = jnp.arange(MOE_EXPERTS, dtype=jnp.int32)
    later_used = (ids[None, :] > ids[:, None]) & (counts[None, :] > 0)
    next_used = jnp.min(jnp.where(later_used, ids[None, :], MOE_EXPERTS), axis=1)
    next_used = jnp.where(next_used < MOE_EXPERTS, next_used, -1).astype(jnp.int32)
    own = block_expert[:, None] == ids[None, :]
    look = lambda table: jnp.sum(jnp.where(own, table[None, :], 0), axis=1)
    block_valid = jnp.clip(look(counts) - (blk_row - look(pad_start)), 0, rb).astype(jnp.int32)
    xs = _dispatch(dest_flat, u2, n_blocks * rb)
    ys = _experts(block_expert, look(next_used), block_valid, n_used[None], xs, w_gate, w_up, w_down)
    return _combine(dest_flat, info, x1, norm_ffn_post[None, :], mod3, ys, seq)


def kernel(x, c, positions, w_ada, b_ada, norm_mix_pre, norm_mix_post, norm_ffn_pre, norm_ffn_post, w_in, conv_w, dn_a_log, dn_dt_bias, dn_out_norm, w_branch_moba, w_branch_delta, w_out, router_group_w, router_group_b, router_expert_w, router_expert_b, expert_w_gate, expert_w_up, expert_w_down):
    bsz, seq, d = x.shape
    depth = w_ada.shape[0]
    for layer in range(depth):
        mod = _ada(c, w_ada[layer], b_ada[layer][None, :])
        mod3 = mod.reshape(bsz, 1, -1)
        x1, u2, logits = _mixer_and_router(
            x, mod3, positions, norm_mix_pre[layer], norm_mix_post[layer], norm_ffn_pre[layer], w_in[layer],
            conv_w[layer], dn_a_log[layer], dn_dt_bias[layer], dn_out_norm[layer], w_branch_moba[layer],
            w_branch_delta[layer], w_out[layer], router_group_w[layer], router_group_b[layer],
            router_expert_w[layer], router_expert_b[layer])
        out = _moe(x1, u2, logits, mod3, norm_ffn_post[layer], expert_w_gate[layer], expert_w_up[layer],
                   expert_w_down[layer], seq)
        x = out.reshape(bsz, seq, d)
    return x
```

```python
import functools
import math

import jax
import jax.numpy as jnp
from jax import lax
from jax.experimental import pallas as pl
from jax.experimental.pallas import tpu as pltpu

F32 = jnp.float32
BF16 = jnp.bfloat16
HI = lax.Precision.HIGHEST

NORM_EPS = 1e-6
HEAD_DIM = 128
MOBA_HEADS = 8
MOBA_BLOCK = 256
MOBA_TOPK = 3
MOBA_HEADS_PER_STEP = 2
ROPE_THETA = 500000.0
ROPE_DIM = HEAD_DIM // 4
DN_HEADS = 8
DN_CONV_WIDTH = 4
DN_CHUNK = 64
DN_GROUP = 256
DN_HEADS_PER_STEP = 4
DN_CONV_PAD = 8
DN_CONV_ROWS = 128
MOE_GROUPS = 4
MOE_EXPERTS_PER_GROUP = 8
MOE_EXPERTS = MOE_GROUPS * MOE_EXPERTS_PER_GROUP
MOE_TOPK = 2
MOE_ROWS = 256
DMA_LOOP_UNROLL = 8
MERGE_SUB_ROWS = 128
LANES = 128
NEG = -1e30

VMEM_LIMIT = 60 * 1024 * 1024


def _cparams(*sem):
    return pltpu.CompilerParams(dimension_semantics=sem, vmem_limit_bytes=VMEM_LIMIT)


def _mm(a, b, precision=None):
    return jnp.dot(a, b, precision=precision, preferred_element_type=F32)


def _nt(a, b, precision=None):
    return lax.dot_general(a, b, (((1,), (1,)), ((), ())), precision=precision,
                           preferred_element_type=F32)


def _eye(n, dtype):
    r = lax.broadcasted_iota(jnp.int32, (n, n), 0)
    c = lax.broadcasted_iota(jnp.int32, (n, n), 1)
    return jnp.where(r == c, 1.0, 0.0).astype(dtype)


def _silu(x):
    return x * jax.nn.sigmoid(x)


def _softplus(x):
    return jnp.maximum(x, 0.0) + jnp.log1p(jnp.exp(-jnp.abs(x)))


def _pack_bf16_pair(lo, hi):
    def rne(x):
        b = pltpu.bitcast(x, jnp.uint32)
        return b + jnp.uint32(0x7FFF) + ((b >> 16) & jnp.uint32(1))
    return (rne(hi) & jnp.uint32(0xFFFF0000)) | (rne(lo) >> 16)


def _unpack_bf16_pair(p):
    return pltpu.bitcast(p << 16, F32), pltpu.bitcast(p & jnp.uint32(0xFFFF0000), F32)


def _ada_body(c_ref, w_ref, b_ref, o_ref):
    cond = _silu(c_ref[...])
    w = w_ref[...]
    c_hi = cond.astype(BF16)
    c_lo = (cond - c_hi.astype(F32)).astype(BF16)
    w_hi = w.astype(BF16)
    w_lo = (w - w_hi.astype(F32)).astype(BF16)
    o_ref[...] = _mm(c_hi, w_hi) + _mm(c_lo, w_hi) + _mm(c_hi, w_lo) + b_ref[...]


def _ada(c, w, b):
    bsz, d = c.shape
    n = w.shape[1]
    tn = 1024
    return pl.pallas_call(
        _ada_body,
        grid=(n // tn,),
        in_specs=[pl.BlockSpec((bsz, d), lambda j: (0, 0)),
                  pl.BlockSpec((d, tn), lambda j: (0, j)),
                  pl.BlockSpec((1, tn), lambda j: (0, j))],
        out_specs=pl.BlockSpec((bsz, tn), lambda j: (0, j)),
        out_shape=jax.ShapeDtypeStruct((bsz, n), F32),
        compiler_params=_cparams("parallel"),
        name="ada",
    )(c, w, b)


def _prenorm_body(x_ref, g_ref, sc_ref, sh_ref, ws_ref, u_ref, os_ref, *, rc):
    gain = g_ref[...]
    scale = 1.0 + sc_ref[...]
    shift = sh_ref[...]

    def chunk(i, carry):
        r0 = pl.multiple_of(i * rc, rc)
        x = x_ref[pl.ds(r0, rc), :]
        y = x * lax.rsqrt(jnp.mean(x * x, axis=-1, keepdims=True) + NORM_EPS) * gain
        u = (y * scale + shift).astype(BF16)
        u_ref[pl.ds(r0, rc), :] = u
        os_ref[pl.ds(r0, rc), :] = _mm(u, ws_ref[...])
        return carry

    lax.fori_loop(0, x_ref.shape[0] // rc, chunk, 0)


def _prenorm(x2, gain, mod3, w_small, seq, *, sc_chunk, sh_chunk):
    t, d = x2.shape
    tm = min(1024, seq)
    per_b = seq // tm
    return pl.pallas_call(
        functools.partial(_prenorm_body, rc=min(256, tm)),
        grid=(t // tm,),
        in_specs=[pl.BlockSpec((tm, d), lambda i: (i, 0)),
                  pl.BlockSpec((1, d), lambda i: (0, 0)),
                  pl.BlockSpec((None, 1, d), lambda i: (i // per_b, 0, sc_chunk)),
                  pl.BlockSpec((None, 1, d), lambda i: (i // per_b, 0, sh_chunk)),
                  pl.BlockSpec((d, LANES), lambda i: (0, 0))],
        out_specs=[pl.BlockSpec((tm, d), lambda i: (i, 0)),
                   pl.BlockSpec((tm, LANES), lambda i: (i, 0))],
        out_shape=[jax.ShapeDtypeStruct((t, d), BF16),
                   jax.ShapeDtypeStruct((t, LANES), F32)],
        compiler_params=_cparams("parallel"),
        name="prenorm",
    )(x2, gain, mod3, mod3, w_small)


def _inproj_body(u_ref, wt_ref, o_ref, stage, w_s, sem, *, tn, n_a, b_row0, n_tiles):
    j = pl.program_id(0)

    def fetch(tile):
        row0 = jnp.where(tile < n_a, tile * tn, b_row0 + (tile - n_a) * tn)
        return pltpu.make_async_copy(wt_ref.at[pl.ds(pl.multiple_of(row0, 8), tn), :], stage, sem)

    @pl.when(pl.program_id(1) == 0)
    def _():
        @pl.when(j == 0)
        def _():
            fetch(j).start()

        fetch(j).wait()
        w_s[...] = stage[...].astype(BF16)

        @pl.when(j + 1 < n_tiles)
        def _():
            fetch(j + 1).start()

    o_ref[...] = _nt(u_ref[...], w_s[...]).astype(o_ref.dtype)


def _inproj(u, w_t, n_a, b_row0, n_b, *, out_first_b):
    t, d = u.shape
    tm = min(2048, t)
    tn = 1024
    assert b_row0 % 8 == 0 and b_row0 + n_b * tn <= w_t.shape[0]

    def out_col(j):
        if not out_first_b:
            return j
        return jnp.where(j < n_a, j + n_b, j - n_a)

    return pl.pallas_call(
        functools.partial(_inproj_body, tn=tn, n_a=n_a, b_row0=b_row0, n_tiles=n_a + n_b),
        grid=(n_a + n_b, t // tm),
        in_specs=[pl.BlockSpec((tm, d), lambda j, i: (i, 0)),
                  pl.BlockSpec(memory_space=pl.ANY)],
        out_specs=pl.BlockSpec((tm, tn), lambda j, i: (i, out_col(j))),
        out_shape=jax.ShapeDtypeStruct((t, (n_a + n_b) * tn), BF16),
        scratch_shapes=[pltpu.VMEM((tn, d), F32), pltpu.VMEM((tn, d), BF16), pltpu.SemaphoreType.DMA(())],
        compiler_params=_cparams("arbitrary", "arbitrary"),
        name="inproj",
    )(u, w_t)


def _moba_body(q_ref, k_ref, v_ref, cos_ref, sin_ref, o_ref, kr_s, vt_s, km_s, *, nblk, hp):
    blk = MOBA_BLOCK
    half = ROPE_DIM // 2
    lane = lax.broadcasted_iota(jnp.int32, (blk, HEAD_DIM), 1)
    eye_d = _eye(HEAD_DIM, BF16)
    eye_b = _eye(blk, BF16)
    scale = HEAD_DIM ** -0.5

    def rope(xf, rows):
        partner = jnp.where(lane < half, pltpu.roll(xf, HEAD_DIM - half, 1), pltpu.roll(xf, half, 1))
        return xf * cos_ref[rows, :] + partner * sin_ref[rows, :]

    def cols(h):
        return slice(h * HEAD_DIM, (h + 1) * HEAD_DIM)

    km_s[...] = jnp.zeros_like(km_s)
    for j in range(nblk):
        rows = slice(j * blk, (j + 1) * blk)
        for h in range(hp):
            kc = rope(k_ref[rows, cols(h)].astype(F32), rows)
            kr_s[rows, cols(h)] = kc.astype(BF16)
            km_s[j:j + 1, cols(h)] = jnp.mean(kc, axis=0, keepdims=True)
            vt_s[cols(h), rows] = _nt(eye_d, v_ref[rows, cols(h)]).astype(BF16)

    key_i = lax.broadcasted_iota(jnp.int32, (blk, blk), 0)
    qry_i = lax.broadcasted_iota(jnp.int32, (blk, blk), 1)
    blk_i = lax.broadcasted_iota(jnp.int32, (8, blk), 0)
    causal_bias = jnp.where(key_i <= qry_i, 0.0, NEG)

    def rows(j):
        return slice(j * blk, (j + 1) * blk)

    def begin(i, h):
        qc = rope(q_ref[rows(i), cols(h)].astype(F32), rows(i))
        st = dict(i=i, h=h, qs=(qc * scale).astype(BF16), scores=[], m=None, bias=None)
        if i > 0:
            g_t = _nt(km_s[:, cols(h)], qc, HI)
            rank = jnp.zeros((8, blk), F32)
            for jp in range(i):
                row = g_t[jp:jp + 1, :]
                beats = (row > g_t) | ((row == g_t) & (jp < blk_i))
                rank = rank + jnp.where(beats, 1.0, 0.0)
            sel = (rank < float(MOBA_TOPK)) & (blk_i < i)
            st["bias"] = jnp.where(sel, 0.0, NEG)
        return st

    def score(st, j):
        i = st["i"]
        s = _nt(kr_s[rows(j), cols(st["h"])], st["qs"])
        if j == i:
            s = s + causal_bias
        st["scores"].append(s)
        mj = jnp.max(s, axis=0, keepdims=True)
        if j < i:
            mj = mj + st["bias"][j:j + 1, :]
        st["m"] = mj if st["m"] is None else jnp.maximum(st["m"], mj)

    def accumulate(st, j):
        i = st["i"]
        if j == 0:
            st["den"] = jnp.zeros((1, blk), F32)
            st["acc"] = jnp.zeros((HEAD_DIM, blk), F32)
        shift = (st["bias"][j:j + 1, :] - st["m"]) if j < i else -st["m"]
        p = jnp.exp(st["scores"][j] + shift)
        st["den"] = st["den"] + jnp.sum(p, axis=0, keepdims=True)
        st["acc"] = st["acc"] + _mm(vt_s[cols(st["h"]), rows(j)], p.astype(BF16))
        if j == i:
            o_t = (st["acc"] / st["den"]).astype(BF16)
            o_ref[rows(i), cols(st["h"])] = _nt(eye_b, o_t).astype(o_ref.dtype)

    prev = None
    for i in range(nblk):
        cur = [begin(i, h) for h in range(hp)]
        for j in range(i + 1):
            for h in range(hp):
                score(cur[h], j)
                if prev is not None and j <= i - 1:
                    accumulate(prev[h], j)
        prev = cur
    for j in range(nblk):
        for h in range(hp):
            accumulate(prev[h], j)


def _moba(proj, cos_t, sin_t, *, q_blk0, k_blk0, v_blk0):
    bsz, seq, _ = proj.shape
    nblk = seq // MOBA_BLOCK
    assert seq % MOBA_BLOCK == 0 and 1 <= nblk <= 8
    hd = HEAD_DIM
    hp = MOBA_HEADS_PER_STEP
    wide = hp * hd
    assert MOBA_HEADS % hp == 0 and all(b0 % hp == 0 for b0 in (q_blk0, k_blk0, v_blk0))
    return pl.pallas_call(
        functools.partial(_moba_body, nblk=nblk, hp=hp),
        grid=(bsz, MOBA_HEADS // hp),
        in_specs=[pl.BlockSpec((None, seq, wide), lambda b, h: (b, 0, q_blk0 // hp + h)),
                  pl.BlockSpec((None, seq, wide), lambda b, h: (b, 0, k_blk0 // hp + h)),
                  pl.BlockSpec((None, seq, wide), lambda b, h: (b, 0, v_blk0 // hp + h)),
                  pl.BlockSpec((None, seq, hd), lambda b, h: (b, 0, 0)),
                  pl.BlockSpec((None, seq, hd), lambda b, h: (b, 0, 0))],
        out_specs=pl.BlockSpec((None, seq, wide), lambda b, h: (b, 0, h)),
        out_shape=jax.ShapeDtypeStruct((bsz, seq, MOBA_HEADS * hd), BF16),
        scratch_shapes=[pltpu.VMEM((seq, wide), BF16),
                        pltpu.VMEM((wide, seq), BF16),
                        pltpu.VMEM((8, wide), F32)],
        compiler_params=_cparams("parallel", "parallel"),
        name="moba",
    )(proj, proj, proj, cos_t, sin_t)


def _chunk_masks(n, chunk):
    r = lax.broadcasted_iota(jnp.int32, (n, n), 0)
    c = lax.broadcasted_iota(jnp.int32, (n, n), 1)
    shift = int(math.log2(chunk))
    same = jnp.right_shift(r, shift) == jnp.right_shift(c, shift)
    return r, c, same


def _gates_body(ba_ref, par_ref, beta_ref, g_ref, gl_ref, *, seq):
    grp = DN_GROUP
    r, c, same = _chunk_masks(grp, DN_CHUNK)
    sums = jnp.concatenate([jnp.where(same & (c <= r), 1.0, 0.0), jnp.where(same, 1.0, 0.0)], axis=0).astype(BF16)
    neg_a = -jnp.exp(par_ref[0:1, :])
    dt_b = par_ref[1:2, :]
    for i in range(seq // grp):
        rows = slice(i * grp, (i + 1) * grp)
        x = ba_ref[rows, :].astype(F32)
        beta_ref[rows, :] = jax.nn.sigmoid(x)
        g = neg_a * _softplus(x + dt_b)
        g_hi = g.astype(BF16)
        rest = g - g_hi.astype(F32)
        g_mid = rest.astype(BF16)
        g_lo = (rest - g_mid.astype(F32)).astype(BF16)
        acc = _mm(sums, g_hi) + _mm(sums, g_mid) + _mm(sums, g_lo)
        g_ref[rows, :] = acc[:grp, :]
        gl_ref[rows, :] = acc[grp:, :]


def _gates(ba, par):
    bsz, seq, _ = ba.shape
    spec = pl.BlockSpec((None, seq, LANES), lambda b: (b, 0, 0))
    return pl.pallas_call(
        functools.partial(_gates_body, seq=seq),
        grid=(bsz,),
        in_specs=[spec, pl.BlockSpec((8, LANES), lambda b: (0, 0))],
        out_specs=[spec, spec, spec],
        out_shape=[jax.ShapeDtypeStruct((bsz, seq, LANES), F32)] * 3,
        compiler_params=_cparams("parallel"),
        name="dn_gates",
    )(ba, par)


def _dn_body(q_ref, k_ref, v_ref, z_ref, cwq_ref, cwk_ref, cwv_ref, beta_ref, g_ref, gl_ref, grow_ref,
             gain_ref, o_ref, stage_s, qkv_a, qkv_b, state_s, vnew_s, *bufs, seq, hp):
    bufs_a, bufs_b = bufs[:5], bufs[5:]
    hg = pl.program_id(1)
    grp = DN_GROUP
    chunk = DN_CHUNK
    ngrp = seq // grp
    hd = HEAD_DIM
    pad = DN_CONV_PAD

    crows = DN_CONV_ROWS
    r, c, same = _chunk_masks(grp, chunk)
    low_incl = same & (c <= r)
    low_strict = same & (c < r)
    eye_g = jnp.where(r == c, 1.0, 0.0)
    eye_d = _eye(hd, BF16)
    lane = lax.broadcasted_iota(jnp.int32, (grp, LANES), 1)
    col_chunk = jnp.right_shift(lax.broadcasted_iota(jnp.int32, (hd, grp), 1), int(math.log2(chunk)))
    gain = gain_ref[...]
    n_double = int(math.log2(chunk)) - 1

    def pick(ref, r0, lane_id):
        return jnp.sum(jnp.where(lane == lane_id, ref[pl.ds(r0, grp), :], 0.0), axis=-1, keepdims=True)

    state_s[...] = jnp.zeros_like(state_s)
    heads = range(hp)

    cols = [slice(hh * hd, (hh + 1) * hd) for hh in heads]
    head = [hg * hp + hh for hh in heads]

    tensors = ((q_ref, cwq_ref, True, hd ** -0.5), (k_ref, cwk_ref, True, None), (v_ref, cwv_ref, False, None))
    nsub = grp // crows

    def conv(gi, qkv, first=False):
        r0 = pl.multiple_of(gi * grp, grp)
        for ti, (src, cw_ref, l2, post) in enumerate(tensors):
            cw = cw_ref[...]
            for sb in range(nsub):
                stage_s[ti, sb, pad:pad + crows, :] = src[pl.ds(r0 + sb * crows, crows), :].astype(F32)
            if first:
                stage_s[ti, 0, 0:pad, :] = jnp.zeros((pad, hp * hd), F32)
            else:
                before = src[pl.ds(pl.multiple_of(r0 - 2 * pad, 2 * pad), 2 * pad), :].astype(F32)
                stage_s[ti, 0, 0:pad, :] = before[pad:, :]
            for sb in range(1, nsub):
                stage_s[ti, sb, 0:pad, :] = stage_s[ti, sb - 1, crows:crows + pad, :]
            for sb in range(nsub):
                for hh in heads:
                    acc = None
                    for j in range(DN_CONV_WIDTH):
                        off = pad - (DN_CONV_WIDTH - 1) + j
                        term = stage_s[ti, sb, off:off + crows, cols[hh]] * cw[j:j + 1, cols[hh]]
                        acc = term if acc is None else acc + term
                    yh = _silu(acc)
                    if l2:
                        yh = yh * lax.rsqrt(jnp.sum(yh * yh, axis=-1, keepdims=True) + NORM_EPS)
                    if post is not None:
                        yh = yh * post
                    qkv[ti, sb * crows:(sb + 1) * crows, cols[hh]] = yh
                yield

    def prepare(gi, qkv, bufs):
        u_r, w_r, qk_r, qd_r, kt_r = bufs
        r0 = pl.multiple_of(gi * grp, grp)
        q = [qkv[0, :, cols[hh]] for hh in heads]
        k = [qkv[1, :, cols[hh]] for hh in heads]
        v = [qkv[2, :, cols[hh]] for hh in heads]
        beta = [pick(beta_ref, r0, head[hh]) for hh in heads]
        g_col = [pick(g_ref, r0, DN_HEADS + head[hh]) for hh in heads]
        gl_col = [pick(gl_ref, r0, DN_HEADS + head[hh]) for hh in heads]
        decay = [jnp.exp(jnp.minimum(g_col[hh] - grow_ref[hh, gi], 0.0)) for hh in heads]
        e_g = [jnp.exp(g_col[hh]) for hh in heads]
        kb = [k[hh] * beta[hh] for hh in heads]
        vb = [(v[hh] * beta[hh]).astype(BF16) for hh in heads]
        k16 = [k[hh].astype(BF16) for hh in heads]
        n_mat = [jnp.where(low_strict, _nt(kb[hh].astype(BF16), k16[hh]) * decay[hh], 0.0) for hh in heads]
        p = [(-n_mat[hh]).astype(BF16) for hh in heads]
        x = [eye_g - n_mat[hh] for hh in heads]
        qk = [jnp.where(low_incl, _nt(q[hh].astype(BF16), k16[hh]) * decay[hh], 0.0).astype(BF16) for hh in heads]
        for hh in heads:
            qk_r[hh] = qk[hh]
            qd_r[hh] = (q[hh] * e_g[hh]).astype(BF16)
            kt_r[hh] = _nt(eye_d, (k[hh] * jnp.exp(gl_col[hh] - g_col[hh])).astype(BF16)).astype(BF16)
        yield
        for _ in range(n_double):
            p2 = [_mm(p[hh], p[hh]).astype(BF16) for hh in heads]
            x = [x[hh] + _mm(x[hh].astype(BF16), p2[hh]) for hh in heads]
            p = p2
            yield
        x16 = [x[hh].astype(BF16) for hh in heads]
        for hh in heads:
            u_r[hh] = _mm(x16[hh], vb[hh])
            w_r[hh] = _mm(x16[hh], (kb[hh] * e_g[hh]).astype(BF16)).astype(BF16)
        yield

    def scan(gi, bufs):
        u_r, w_r, qk_r, qd_r, kt_r = bufs
        r0 = pl.multiple_of(gi * grp, grp)
        vnew_s[...] = jnp.zeros_like(vnew_s)
        for ci in range(grp // chunk):
            rows = slice(ci * chunk, (ci + 1) * chunk)
            gl_row = jnp.exp(gl_ref[pl.ds(r0 + ci * chunk, 1), :])
            for hh in heads:
                state = state_s[hh]
                s16 = state.astype(BF16)
                v_new = u_r[hh, rows, :] - _mm(w_r[hh, rows, :], s16)
                vnew_s[hh, rows, :] = v_new.astype(BF16)
                o = _mm(qd_r[hh, rows, :], s16) + _mm(qk_r[hh, rows, :], vnew_s[hh])
                chunk_decay = jnp.sum(jnp.where(lane[0:1, :] == DN_HEADS + head[hh], gl_row, 0.0),
                                      axis=-1, keepdims=True)
                k_tail_t = kt_r[hh]
                kt = jnp.where(col_chunk == ci, k_tail_t, jnp.zeros_like(k_tail_t))
                state_s[hh] = state * chunk_decay + _mm(kt, vnew_s[hh])
                on = o * lax.rsqrt(jnp.mean(o * o, axis=-1, keepdims=True) + NORM_EPS) * gain
                zz = z_ref[pl.ds(r0 + ci * chunk, chunk), cols[hh]].astype(F32)
                o_ref[pl.ds(r0 + ci * chunk, chunk), cols[hh]] = (on * _silu(zz)).astype(o_ref.dtype)
            yield

    def interleave(first, *others):
        for _ in first:
            for _ in range(2):
                for steps in others:
                    next(steps, None)
        for steps in others:
            for _ in steps:
                pass

    interleave(conv(0, qkv_a, first=True))
    interleave(prepare(0, qkv_a, bufs_a), conv(1, qkv_b))

    def pair(pi, carry):
        g0 = 2 * pi
        interleave(scan(g0, bufs_a), prepare(g0 + 1, qkv_b, bufs_b), conv(g0 + 2, qkv_a))
        interleave(scan(g0 + 1, bufs_b), prepare(g0 + 2, qkv_a, bufs_a), conv(g0 + 3, qkv_b))
        return carry

    lax.fori_loop(0, ngrp // 2 - 1, pair, 0)
    interleave(scan(ngrp - 2, bufs_a), prepare(ngrp - 1, qkv_b, bufs_b))
    interleave(scan(ngrp - 1, bufs_b))


def _deltanet(proj, conv_w, beta, gcum, glast, grow, gain, *, q_blk0, k_blk0, v_blk0, z_blk0):
    bsz, seq, _ = proj.shape
    hd = HEAD_DIM
    nh = DN_HEADS
    assert seq % DN_GROUP == 0
    ngrp = seq // DN_GROUP

    hp = DN_HEADS_PER_STEP
    wide = hp * hd
    assert nh % hp == 0 and all(b0 % hp == 0 for b0 in (q_blk0, k_blk0, v_blk0, z_blk0))

    def col(blk0):
        return pl.BlockSpec((None, seq, wide), lambda b, h: (b, 0, blk0 // hp + h))

    def cw(blk0):
        return pl.BlockSpec((DN_CONV_WIDTH, wide), lambda b, h: (0, blk0 // hp + h))

    full = pl.BlockSpec((None, seq, LANES), lambda b, h: (b, 0, 0))
    return pl.pallas_call(
        functools.partial(_dn_body, seq=seq, hp=hp),
        grid=(bsz, nh // hp),
        in_specs=[col(q_blk0), col(k_blk0), col(v_blk0), col(z_blk0),
                  cw(0), cw(nh), cw(2 * nh),
                  full, full, full,
                  pl.BlockSpec((hp, ngrp, 1, DN_GROUP), lambda b, h: (b * (nh // hp) + h, 0, 0, 0)),
                  pl.BlockSpec((1, hd), lambda b, h: (0, 0))],
        out_specs=pl.BlockSpec((None, seq, wide), lambda b, h: (b, 0, h)),
        out_shape=jax.ShapeDtypeStruct((bsz, seq, nh * hd), BF16),
        scratch_shapes=[pltpu.VMEM((3, DN_GROUP // DN_CONV_ROWS, DN_CONV_ROWS + DN_CONV_PAD, wide), F32)]
        + [pltpu.VMEM((3, DN_GROUP, wide), F32)] * 2
        + [pltpu.VMEM((hp, hd, hd), F32), pltpu.VMEM((hp, DN_GROUP, hd), BF16)]
        + [pltpu.VMEM((hp, DN_GROUP, hd), F32), pltpu.VMEM((hp, DN_GROUP, hd), BF16),
           pltpu.VMEM((hp, DN_GROUP, DN_GROUP), BF16), pltpu.VMEM((hp, DN_GROUP, hd), BF16),
           pltpu.VMEM((hp, hd, DN_GROUP), BF16)] * 2,
        compiler_params=_cparams("parallel", "parallel"),
        name="deltanet",
    )(proj, proj, proj, proj, conv_w, conv_w, conv_w, beta, gcum, glast, grow, gain)


def _merge_body(ya_ref, yb_ref, ga_ref, gb_ref, x_ref, wm_ref, wd_ref, wo_ref, wr_ref, br_ref,
                npost_ref, npre_ref, gt_ref, sc_ref, sh_ref, x1_ref, u2_ref, lg_ref):
    sub = MERGE_SUB_ROWS
    for s in range(x_ref.shape[0] // sub):
        rows = slice(s * sub, (s + 1) * sub)
        ma = _mm(ya_ref[rows, :], wm_ref[...])
        mb = _mm(yb_ref[rows, :], wd_ref[...])
        merged = (jax.nn.sigmoid(ga_ref[rows, :].astype(F32)) * ma
                  + jax.nn.sigmoid(gb_ref[rows, :].astype(F32)) * mb)
        y = _mm(merged.astype(BF16), wo_ref[...])
        yn = y * lax.rsqrt(jnp.mean(y * y, axis=-1, keepdims=True) + NORM_EPS) * npost_ref[...]
        x1 = x_ref[rows, :] + gt_ref[...] * yn
        x1_ref[rows, :] = x1
        un = x1 * lax.rsqrt(jnp.mean(x1 * x1, axis=-1, keepdims=True) + NORM_EPS) * npre_ref[...]
        u2 = un * (1.0 + sc_ref[...]) + sh_ref[...]
        half = u2.shape[1] // 2
        u2_ref[rows, :] = _pack_bf16_pair(u2[:, :half], u2[:, half:])
        u_hi = u2.astype(BF16)
        u_lo = (u2 - u_hi.astype(F32)).astype(BF16)
        lg2 = _nt(wr_ref[...], u_hi)
        lg_ref[:, rows] = lg2[:LANES, :] + lg2[LANES:, :] + _nt(wr_ref[:LANES, :], u_lo) + br_ref[...]


def _merge(ya, yb, proj2, x2, wm, wd, wo, wr, br, npost, npre, mod3, seq, *, ga_blk, gb_blk):
    t, d = x2.shape
    wa = ya.shape[1]
    tm = min(256, seq)
    per_b = seq // tm
    const = lambda i: (0, 0)
    once = dict(pipeline_mode=pl.Buffered(1))

    def modspec(chunk):
        return pl.BlockSpec((None, 1, d), lambda i: (i // per_b, 0, chunk))

    return pl.pallas_call(
        _merge_body,
        grid=(t // tm,),
        in_specs=[pl.BlockSpec((tm, wa), lambda i: (i, 0)),
                  pl.BlockSpec((tm, wa), lambda i: (i, 0)),
                  pl.BlockSpec((tm, d), lambda i: (i, ga_blk)),
                  pl.BlockSpec((tm, d), lambda i: (i, gb_blk)),
                  pl.BlockSpec((tm, d), lambda i: (i, 0)),
                  pl.BlockSpec((wa, d), const, **once),
                  pl.BlockSpec((wa, d), const, **once),
                  pl.BlockSpec((d, d), const, **once),
                  pl.BlockSpec((2 * LANES, d), const, **once),
                  pl.BlockSpec((LANES, 1), const),
                  pl.BlockSpec((1, d), const),
                  pl.BlockSpec((1, d), const),
                  modspec(2), modspec(4), modspec(3)],
        out_specs=[pl.BlockSpec((tm, d), lambda i: (i, 0)),
                   pl.BlockSpec((tm, d // 2), lambda i: (i, 0)),
                   pl.BlockSpec((LANES, tm), lambda i: (0, i))],
        out_shape=[jax.ShapeDtypeStruct((t, d), F32),
                   jax.ShapeDtypeStruct((t, d // 2), jnp.uint32),
                   jax.ShapeDtypeStruct((LANES, t), F32)],
        compiler_params=_cparams("parallel"),
        name="merge",
    )(ya, yb, proj2, proj2, x2, wm, wd, wo, wr, br, npost, npre, mod3, mod3, mod3)


def _route_body(lg_ref, info_ref, col_ref, cnt_ref, run_s, *, tr):
    @pl.when(pl.program_id(0) == 0)
    def _():
        run_s[...] = jnp.zeros_like(run_s)

    lg = lg_ref[...]
    row = lax.broadcasted_iota(jnp.int32, (LANES, tr), 0)
    row_f = row.astype(F32)
    big = float(LANES)

    def first_max(vals, mask):
        mx = jnp.max(jnp.where(mask, vals, NEG), axis=0, keepdims=True)
        idx = jnp.min(jnp.where(mask & (vals == mx), row_f, big), axis=0, keepdims=True)
        return mx, idx

    gmask = row < MOE_GROUPS
    gmax, gidx = first_max(lg, gmask)
    p_group = 1.0 / jnp.sum(jnp.where(gmask, jnp.exp(lg - gmax), 0.0), axis=0, keepdims=True)
    lo = float(MOE_GROUPS) + gidx * float(MOE_EXPERTS_PER_GROUP)
    emask = (row_f >= lo) & (row_f < lo + float(MOE_EXPERTS_PER_GROUP))
    m1, i1 = first_max(lg, emask)
    m2, i2 = first_max(lg, emask & (row_f != i1))
    e2 = jnp.exp(m2 - m1)
    w1 = p_group / (1.0 + e2)
    w2 = p_group * e2 / (1.0 + e2)
    oh1 = row_f == i1
    oh2 = row_f == i2
    oh = jnp.where(oh1 | oh2, 1.0, 0.0).astype(BF16)
    r = lax.broadcasted_iota(jnp.int32, (tr, tr), 0)
    c = lax.broadcasted_iota(jnp.int32, (tr, tr), 1)
    before = jnp.where(r < c, 1.0, 0.0).astype(BF16)
    prefix = _mm(oh, before) + run_s[:, 0:1]
    rank1 = jnp.sum(jnp.where(oh1, prefix, 0.0), axis=0, keepdims=True)
    rank2 = jnp.sum(jnp.where(oh2, prefix, 0.0), axis=0, keepdims=True)
    run_s[...] = run_s[...] + jnp.sum(oh.astype(F32), axis=1, keepdims=True)
    goff = float(MOE_GROUPS)
    info = jnp.where(row == 0, i1 - goff, 0.0)
    info = jnp.where(row == 1, i2 - goff, info)
    info = jnp.where(row == 2, rank1, info)
    info = jnp.where(row == 3, rank2, info)
    info = jnp.where(row == 4, w1, info)
    info = jnp.where(row == 5, w2, info)
    info_ref[...] = info[0:8, :]
    eye = _eye(tr, BF16)
    hi = info.astype(BF16)
    rest = info - hi.astype(F32)
    mid = rest.astype(BF16)
    col_ref[...] = _nt(eye, hi) + _nt(eye, mid) + _nt(eye, (rest - mid.astype(F32)).astype(BF16))
    cnt_ref[...] = run_s[...]


def _route(logits_t):
    t = logits_t.shape[1]
    tr = min(512, t)
    return pl.pallas_call(
        functools.partial(_route_body, tr=tr),
        grid=(t // tr,),
        in_specs=[pl.BlockSpec((LANES, tr), lambda i: (0, i))],
        out_specs=[pl.BlockSpec((8, tr), lambda i: (0, i)),
                   pl.BlockSpec((tr, LANES), lambda i: (i, 0)),
                   pl.BlockSpec((LANES, LANES), lambda i: (0, 0))],
        out_shape=[jax.ShapeDtypeStruct((8, t), F32),
                   jax.ShapeDtypeStruct((t, LANES), F32),
                   jax.ShapeDtypeStruct((LANES, LANES), F32)],
        scratch_shapes=[pltpu.VMEM((LANES, LANES), F32)],
        compiler_params=_cparams("arbitrary"),
        name="route",
    )(logits_t)


def _dispatch_body(d0_ref, d1_ref, u_ref, xs_ref, sem, *, td):
    dests = (d0_ref, d1_ref)

    def row_copy(r, k):
        return pltpu.make_async_copy(u_ref.at[pl.ds(r, 1), :], xs_ref.at[pl.ds(dests[k][r], 1), :], sem)

    def start(r, carry):
        for k in range(MOE_TOPK):
            row_copy(r, k).start()
        return carry

    def wait(r, carry):
        for k in range(MOE_TOPK):
            row_copy(r, k).wait()
        return carry

    lax.fori_loop(0, td, start, 0, unroll=DMA_LOOP_UNROLL)
    lax.fori_loop(0, td, wait, 0, unroll=DMA_LOOP_UNROLL)


def _dispatch(dest, u2, n_rows):
    t, d = u2.shape
    td = min(1024, t)
    nt = t // td
    return pl.pallas_call(
        functools.partial(_dispatch_body, td=td),
        grid=(nt,),
        in_specs=[pl.BlockSpec((td,), lambda i: (i,), memory_space=pltpu.SMEM),
                  pl.BlockSpec((td,), lambda i: (nt + i,), memory_space=pltpu.SMEM),
                  pl.BlockSpec((td, d), lambda i: (i, 0))],
        out_specs=pl.BlockSpec(memory_space=pl.ANY),
        out_shape=jax.ShapeDtypeStruct((n_rows, d), u2.dtype),
        scratch_shapes=[pltpu.SemaphoreType.DMA(())],
        compiler_params=_cparams("arbitrary"),
        name="dispatch",
    )(dest, dest, u2)


def _experts_body(be_ref, nx_ref, nv_ref, nu_ref, x_ref, wg_ref, wu_ref, wd_ref, o_ref,
                  stage_g, stage_u, stage_d, wg_s, wu_s, wd_s, sems):
    i = pl.program_id(0)

    def fetch(e):
        return (pltpu.make_async_copy(wg_ref.at[e], stage_g, sems.at[0]),
                pltpu.make_async_copy(wu_ref.at[e], stage_u, sems.at[1]),
                pltpu.make_async_copy(wd_ref.at[e], stage_d, sems.at[2]))

    @pl.when(i == 0)
    def _():
        for cp in fetch(be_ref[0]):
            cp.start()

    @pl.when(jnp.logical_or(i == 0, be_ref[i] != be_ref[jnp.maximum(i - 1, 0)]))
    def _():
        for cp in fetch(be_ref[i]):
            cp.wait()
        wg_s[...] = stage_g[...].astype(BF16)
        wu_s[...] = stage_u[...].astype(BF16)
        wd_s[...] = stage_d[...].astype(BF16)

        @pl.when(nx_ref[i] >= 0)
        def _():
            for cp in fetch(nx_ref[i]):
                cp.start()

    @pl.when(i < nu_ref[0])
    def _():
        row = lax.broadcasted_iota(jnp.int32, (x_ref.shape[0], 1), 0)
        packed = jnp.where(row < nv_ref[i], x_ref[...], jnp.uint32(0))
        x_lo, x_hi = (v.astype(BF16) for v in _unpack_bf16_pair(packed))
        half = packed.shape[1]
        gate = _mm(x_lo, wg_s[0:half, :]) + _mm(x_hi, wg_s[half:, :])
        up = _mm(x_lo, wu_s[0:half, :]) + _mm(x_hi, wu_s[half:, :])
        y = _mm((_silu(gate) * up).astype(BF16), wd_s[...])
        o_ref[...] = _pack_bf16_pair(y[:, :half], y[:, half:])

    @pl.when(i >= nu_ref[0])
    def _():
        o_ref[...] = jnp.zeros_like(o_ref)


def _experts(block_expert, next_expert, block_valid, n_used, xs, wg, wu, wd):
    nr, dp = xs.shape
    d, ff = wg.shape[1], wg.shape[2]
    assert d == 2 * dp
    rb = MOE_ROWS
    row_map = lambda i, be, nx, nv, nu: (jnp.minimum(i, nu[0] - 1), 0)
    hbm = pl.BlockSpec(memory_space=pl.ANY)
    grid_spec = pltpu.PrefetchScalarGridSpec(
        num_scalar_prefetch=4,
        grid=(nr // rb,),
        in_specs=[pl.BlockSpec((rb, dp), row_map), hbm, hbm, hbm],
        out_specs=pl.BlockSpec((rb, dp), lambda i, be, nx, nv, nu: (i, 0)),
        scratch_shapes=[pltpu.VMEM((d, ff), F32), pltpu.VMEM((d, ff), F32), pltpu.VMEM((ff, d), F32),
                        pltpu.VMEM((d, ff), BF16), pltpu.VMEM((d, ff), BF16), pltpu.VMEM((ff, d), BF16),
                        pltpu.SemaphoreType.DMA((3,))],
    )
    return pl.pallas_call(
        _experts_body,
        grid_spec=grid_spec,
        out_shape=jax.ShapeDtypeStruct((nr, dp), jnp.uint32),
        compiler_params=_cparams("arbitrary"),
        name="experts",
    )(block_expert, next_expert, block_valid, n_used, xs, wg, wu, wd)


def _combine_body(d0_ref, d1_ref, n0_ref, n1_ref, info_ref, x1_ref, npost_ref, gt_ref, yb_ref, o_ref,
                  buf, sems, *, tc, rc, nt):
    i = pl.program_id(0)
    slot = lax.rem(i, 2)
    own = (d0_ref, d1_ref)
    ahead = (n0_ref, n1_ref)

    def row_copy(dests, sl, r, k):
        return pltpu.make_async_copy(yb_ref.at[pl.ds(dests[k][r], 1), :], buf.at[sl, k, pl.ds(r, 1), :],
                                     sems.at[sl])

    @pl.when(i == 0)
    def _():
        def start(r, carry):
            for k in range(MOE_TOPK):
                row_copy(own, slot, r, k).start()
            return carry

        lax.fori_loop(0, tc, start, 0, unroll=DMA_LOOP_UNROLL)

    def wait(r, carry):
        for k in range(MOE_TOPK):
            row_copy(own, slot, r, k).wait()
        return carry

    lax.fori_loop(0, tc, wait, 0, unroll=DMA_LOOP_UNROLL)

    half = buf.shape[3]
    gain_lo, gain_hi = npost_ref[:, :half], npost_ref[:, half:]
    gate_lo, gate_hi = gt_ref[:, :half], gt_ref[:, half:]

    def chunk(ci, carry, request_next):
        r0 = pl.multiple_of(ci * rc, rc)
        info = info_ref[pl.ds(r0, rc), :]
        w0, w1 = info[:, 4:5], info[:, 5:6]
        a_lo, a_hi = _unpack_bf16_pair(buf[slot, 0, pl.ds(r0, rc), :])
        b_lo, b_hi = _unpack_bf16_pair(buf[slot, 1, pl.ds(r0, rc), :])
        y_lo = w0 * a_lo + w1 * b_lo
        y_hi = w0 * a_hi + w1 * b_hi
        ms = (jnp.sum(y_lo * y_lo, axis=-1, keepdims=True)
              + jnp.sum(y_hi * y_hi, axis=-1, keepdims=True)) * (1.0 / (2 * half))
        inv = lax.rsqrt(ms + NORM_EPS)
        o_ref[pl.ds(r0, rc), :half] = x1_ref[pl.ds(r0, rc), :half] + gate_lo * (y_lo * inv * gain_lo)
        o_ref[pl.ds(r0, rc), half:] = x1_ref[pl.ds(r0, rc), half:] + gate_hi * (y_hi * inv * gain_hi)
        if request_next:
            for rr in range(rc):
                for k in range(MOE_TOPK):
                    row_copy(ahead, 1 - slot, r0 + rr, k).start()
        return carry

    @pl.when(i + 1 < nt)
    def _():
        lax.fori_loop(0, tc // rc, functools.partial(chunk, request_next=True), 0)

    @pl.when(i + 1 == nt)
    def _():
        lax.fori_loop(0, tc // rc, functools.partial(chunk, request_next=False), 0)


def _combine(dest, info, x1, npost, mod3, yb, seq):
    t, d = x1.shape
    tc = min(1024, seq)
    rc = min(128, tc)
    per_b = seq // tc
    nt = t // tc
    nxt = lambda i: jnp.minimum(i + 1, nt - 1)
    return pl.pallas_call(
        functools.partial(_combine_body, tc=tc, rc=rc, nt=nt),
        grid=(nt,),
        in_specs=[pl.BlockSpec((tc,), lambda i: (i,), memory_space=pltpu.SMEM),
                  pl.BlockSpec((tc,), lambda i: (nt + i,), memory_space=pltpu.SMEM),
                  pl.BlockSpec((tc,), lambda i: (nxt(i),), memory_space=pltpu.SMEM),
                  pl.BlockSpec((tc,), lambda i: (nt + nxt(i),), memory_space=pltpu.SMEM),
                  pl.BlockSpec((tc, LANES), lambda i: (i, 0)),
                  pl.BlockSpec((tc, d), lambda i: (i, 0)),
                  pl.BlockSpec((1, d), lambda i: (0, 0)),
                  pl.BlockSpec((None, 1, d), lambda i: (i // per_b, 0, 5)),
                  pl.BlockSpec(memory_space=pl.ANY)],
        out_specs=pl.BlockSpec((tc, d), lambda i: (i, 0)),
        out_shape=jax.ShapeDtypeStruct((t, d), F32),
        scratch_shapes=[pltpu.VMEM((2, MOE_TOPK, tc, d // 2), jnp.uint32), pltpu.SemaphoreType.DMA((2,))],
        compiler_params=_cparams("arbitrary"),
        name="combine",
    )(dest, dest, dest, dest, info, x1, npost, mod3, yb)


def _rope_tables(positions):
    half = ROPE_DIM // 2
    inv_freq = jnp.power(ROPE_THETA, -jnp.arange(half, dtype=F32) * (2.0 / ROPE_DIM))
    ang = inv_freq[None, :, None] * positions.astype(F32)[:, None, :]
    cos, sin = jnp.cos(ang), jnp.sin(ang)
    rest = (positions.shape[0], HEAD_DIM - ROPE_DIM, positions.shape[1])
    cos_t = jnp.concatenate([cos, cos, jnp.ones(rest, F32)], axis=1)
    sin_t = jnp.concatenate([-sin, sin, jnp.zeros(rest, F32)], axis=1)
    return jnp.transpose(cos_t, (0, 2, 1)), jnp.transpose(sin_t, (0, 2, 1))


def _pad_lanes(v, n=LANES):
    return jnp.pad(v, [(0, 0)] * (v.ndim - 1) + [(0, n - v.shape[-1])])


def _mixer_and_router(x, mod3, positions, norm_mix_pre, norm_mix_post, norm_ffn_pre, w_in, conv_w,
                      dn_a_log, dn_dt_bias, dn_out_norm, w_branch_moba, w_branch_delta, w_out,
                      router_group_w, router_group_b, router_expert_w, router_expert_b):
    bsz, seq, d = x.shape
    t = bsz * seq
    mw = MOBA_HEADS * HEAD_DIM
    dw = DN_HEADS * HEAD_DIM
    o_qa, o_ka, o_va = 0, mw, 2 * mw
    o_dn = 3 * mw
    o_z = o_dn + 3 * dw
    o_ba = o_z + dw
    o_ga = o_ba + 2 * DN_HEADS
    o_gb = o_ga + d
    w_small = _pad_lanes(w_in[:, o_ba:o_ga]).astype(BF16)
    x2 = x.reshape(t, d)
    u, ba2 = _prenorm(x2, norm_mix_pre[None, :], mod3, w_small, seq, sc_chunk=1, sh_chunk=0)
    assert o_ba % 1024 == 0 and (2 * d) % 1024 == 0
    proj2 = _inproj(u, w_in.T, o_ba // 1024, o_ga, 2 * d // 1024, out_first_b=True)
    proj = proj2.reshape(bsz, seq, -1)
    c0 = 2 * d // LANES
    nh = MOBA_HEADS

    cos_t, sin_t = _rope_tables(positions)
    ya = _moba(proj, cos_t, sin_t, q_blk0=c0, k_blk0=c0 + nh, v_blk0=c0 + 2 * nh)

    par = jnp.zeros((8, LANES), F32)
    par = par.at[0, DN_HEADS:2 * DN_HEADS].set(dn_a_log.astype(F32))
    par = par.at[1, DN_HEADS:2 * DN_HEADS].set(dn_dt_bias.astype(F32))
    beta, gcum, glast = _gates(ba2.reshape(bsz, seq, LANES), par)
    ngrp = seq // DN_GROUP
    grow = jnp.transpose(gcum[:, :, DN_HEADS:2 * DN_HEADS], (0, 2, 1)).reshape(bsz * DN_HEADS, ngrp, 1, DN_GROUP)
    d0 = c0 + 3 * nh
    yb = _deltanet(proj, conv_w, beta, gcum, glast, grow, dn_out_norm[None, :].astype(F32),
                   q_blk0=d0, k_blk0=d0 + DN_HEADS, v_blk0=d0 + 2 * DN_HEADS, z_blk0=d0 + 3 * DN_HEADS)

    wr = _pad_lanes(jnp.concatenate([router_group_w, router_expert_w], axis=1)).T
    wr_hi = wr.astype(BF16)
    wr = jnp.concatenate([wr_hi, (wr - wr_hi.astype(F32)).astype(BF16)], axis=0)
    br = _pad_lanes(jnp.concatenate([router_group_b, router_expert_b])[None, :]).T
    return _merge(ya.reshape(t, mw), yb.reshape(t, dw), proj2, x2,
                  w_branch_moba.astype(BF16), w_branch_delta.astype(BF16), w_out.astype(BF16), wr, br,
                  norm_mix_post[None, :], norm_ffn_pre[None, :], mod3, seq, ga_blk=0, gb_blk=1)


def _moe(x1, u2, logits, mod3, norm_ffn_post, w_gate, w_up, w_down, seq):
    t, d = x1.shape
    info_t, info, counts = _route(logits)
    counts = counts[MOE_GROUPS:MOE_GROUPS + MOE_EXPERTS, 0].astype(jnp.int32)
    rb = MOE_ROWS
    padded = (counts + rb - 1) // rb * rb
    pad_end = jnp.cumsum(padded)
    pad_start = pad_end - padded
    eid = info_t[0:MOE_TOPK].astype(jnp.int32)
    rank = info_t[MOE_TOPK:2 * MOE_TOPK].astype(jnp.int32)
    experts = jnp.arange(MOE_EXPERTS, dtype=jnp.int32)[:, None, None]
    start = jnp.sum(jnp.where(eid[None] == experts, pad_start[:, None, None], 0), axis=0)
    dest_flat = (start + rank).reshape(-1)
    n_blocks = (t * MOE_TOPK + MOE_EXPERTS * (rb - 1)) // rb + 1
    n_used = (pad_end[-1] // rb).astype(jnp.int32)
    blk_row = jnp.minimum(jnp.arange(n_blocks, dtype=jnp.int32), n_used - 1) * rb
    block_expert = jnp.minimum(jnp.sum(pad_end[None, :] <= blk_row[:, None], axis=1),
                               MOE_EXPERTS - 1).astype(jnp.int32)
    ids = jnp.arange(MOE_EXPERTS, dtype=jnp.int32)
    later_used = (ids[None, :] > ids[:, None]) & (counts[None, :] > 0)
    next_used = jnp.min(jnp.where(later_used, ids[None, :], MOE_EXPERTS), axis=1)
    next_used = jnp.where(next_used < MOE_EXPERTS, next_used, -1).astype(jnp.int32)
    own = block_expert[:, None] == ids[None, :]
    look = lambda table: jnp.sum(jnp.where(own, table[None, :], 0), axis=1)
    block_valid = jnp.clip(look(counts) - (blk_row - look(pad_start)), 0, rb).astype(jnp.int32)
    xs = _dispatch(dest_flat, u2, n_blocks * rb)
    ys = _experts(block_expert, look(next_used), block_valid, n_used[None], xs, w_gate, w_up, w_down)
    return _combine(dest_flat, info, x1, norm_ffn_post[None, :], mod3, ys, seq)


def kernel(x, c, positions, w_ada, b_ada, norm_mix_pre, norm_mix_post, norm_ffn_pre, norm_ffn_post, w_in, conv_w, dn_a_log, dn_dt_bias, dn_out_norm, w_branch_moba, w_branch_delta, w_out, router_group_w, router_group_b, router_expert_w, router_expert_b, expert_w_gate, expert_w_up, expert_w_down):
    bsz, seq, d = x.shape
    depth = w_ada.shape[0]
    for layer in range(depth):
        mod = _ada(c, w_ada[layer], b_ada[layer][None, :])
        mod3 = mod.reshape(bsz, 1, -1)
        x1, u2, logits = _mixer_and_router(
            x, mod3, positions, norm_mix_pre[layer], norm_mix_post[layer], norm_ffn_pre[layer], w_in[layer],
            conv_w[layer], dn_a_log[layer], dn_dt_bias[layer], dn_out_norm[layer], w_branch_moba[layer],
            w_branch_delta[layer], w_out[layer], router_group_w[layer], router_group_b[layer],
            router_expert_w[layer], router_expert_b[layer])
        out = _moe(x1, u2, logits, mod3, norm_ffn_post[layer], expert_w_gate[layer], expert_w_up[layer],
                   expert_w_down[layer], seq)
        x = out.reshape(bsz, seq, d)
    return x
```

```python
import functools
import math

import jax
import jax.numpy as jnp
from jax import lax
from jax.experimental import pallas as pl
from jax.experimental.pallas import tpu as pltpu

F32 = jnp.float32
BF16 = jnp.bfloat16
HI = lax.Precision.HIGHEST

NORM_EPS = 1e-6
HEAD_DIM = 128
MOBA_HEADS = 8
MOBA_BLOCK = 256
MOBA_TOPK = 3
MOBA_HEADS_PER_STEP = 2
ROPE_THETA = 500000.0
ROPE_DIM = HEAD_DIM // 4
DN_HEADS = 8
DN_CONV_WIDTH = 4
DN_CHUNK = 64
DN_GROUP = 256
DN_HEADS_PER_STEP = 4
DN_CONV_PAD = 8
DN_CONV_ROWS = 128
MOE_GROUPS = 4
MOE_EXPERTS_PER_GROUP = 8
MOE_EXPERTS = MOE_GROUPS * MOE_EXPERTS_PER_GROUP
MOE_TOPK = 2
MOE_ROWS = 256
DMA_LOOP_UNROLL = 8
MERGE_SUB_ROWS = 128
LANES = 128
NEG = -1e30

VMEM_LIMIT = 60 * 1024 * 1024


def _cparams(*sem):
    return pltpu.CompilerParams(dimension_semantics=sem, vmem_limit_bytes=VMEM_LIMIT)


def _mm(a, b, precision=None):
    return jnp.dot(a, b, precision=precision, preferred_element_type=F32)


def _nt(a, b, precision=None):
    return lax.dot_general(a, b, (((1,), (1,)), ((), ())), precision=precision,
                           preferred_element_type=F32)


def _eye(n, dtype):
    r = lax.broadcasted_iota(jnp.int32, (n, n), 0)
    c = lax.broadcasted_iota(jnp.int32, (n, n), 1)
    return jnp.where(r == c, 1.0, 0.0).astype(dtype)


def _silu(x):
    return x * jax.nn.sigmoid(x)


def _softplus(x):
    return jnp.maximum(x, 0.0) + jnp.log1p(jnp.exp(-jnp.abs(x)))


def _pack_bf16_pair(lo, hi):
    def rne(x):
        b = pltpu.bitcast(x, jnp.uint32)
        return b + jnp.uint32(0x7FFF) + ((b >> 16) & jnp.uint32(1))
    return (rne(hi) & jnp.uint32(0xFFFF0000)) | (rne(lo) >> 16)


def _unpack_bf16_pair(p):
    return pltpu.bitcast(p << 16, F32), pltpu.bitcast(p & jnp.uint32(0xFFFF0000), F32)


def _ada_body(c_ref, w_ref, b_ref, o_ref):
    cond = _silu(c_ref[...])
    w = w_ref[...]
    c_hi = cond.astype(BF16)
    c_lo = (cond - c_hi.astype(F32)).astype(BF16)
    w_hi = w.astype(BF16)
    w_lo = (w - w_hi.astype(F32)).astype(BF16)
    o_ref[...] = _mm(c_hi, w_hi) + _mm(c_lo, w_hi) + _mm(c_hi, w_lo) + b_ref[...]


def _ada(c, w, b):
    bsz, d = c.shape
    n = w.shape[1]
    tn = 1024
    return pl.pallas_call(
        _ada_body,
        grid=(n // tn,),
        in_specs=[pl.BlockSpec((bsz, d), lambda j: (0, 0)),
                  pl.BlockSpec((d, tn), lambda j: (0, j)),
                  pl.BlockSpec((1, tn), lambda j: (0, j))],
        out_specs=pl.BlockSpec((bsz, tn), lambda j: (0, j)),
        out_shape=jax.ShapeDtypeStruct((bsz, n), F32),
        compiler_params=_cparams("parallel"),
        name="ada",
    )(c, w, b)


def _prenorm_body(x_ref, g_ref, sc_ref, sh_ref, ws_ref, u_ref, os_ref, *, rc):
    gain = g_ref[...]
    scale = 1.0 + sc_ref[...]
    shift = sh_ref[...]

    def chunk(i, carry):
        r0 = pl.multiple_of(i * rc, rc)
        x = x_ref[pl.ds(r0, rc), :]
        y = x * lax.rsqrt(jnp.mean(x * x, axis=-1, keepdims=True) + NORM_EPS) * gain
        u = (y * scale + shift).astype(BF16)
        u_ref[pl.ds(r0, rc), :] = u
        os_ref[pl.ds(r0, rc), :] = _mm(u, ws_ref[...])
        return carry

    lax.fori_loop(0, x_ref.shape[0] // rc, chunk, 0)


def _prenorm(x2, gain, mod3, w_small, seq, *, sc_chunk, sh_chunk):
    t, d = x2.shape
    tm = min(1024, seq)
    per_b = seq // tm
    return pl.pallas_call(
        functools.partial(_prenorm_body, rc=min(256, tm)),
        grid=(t // tm,),
        in_specs=[pl.BlockSpec((tm, d), lambda i: (i, 0)),
                  pl.BlockSpec((1, d), lambda i: (0, 0)),
                  pl.BlockSpec((None, 1, d), lambda i: (i // per_b, 0, sc_chunk)),
                  pl.BlockSpec((None, 1, d), lambda i: (i // per_b, 0, sh_chunk)),
                  pl.BlockSpec((d, LANES), lambda i: (0, 0))],
        out_specs=[pl.BlockSpec((tm, d), lambda i: (i, 0)),
                   pl.BlockSpec((tm, LANES), lambda i: (i, 0))],
        out_shape=[jax.ShapeDtypeStruct((t, d), BF16),
                   jax.ShapeDtypeStruct((t, LANES), F32)],
        compiler_params=_cparams("parallel"),
        name="prenorm",
    )(x2, gain, mod3, mod3, w_small)


def _inproj_body(u_ref, wt_ref, o_ref, stage, w_s, sem, *, tn, n_a, b_row0, n_tiles):
    j = pl.program_id(0)

    def fetch(tile):
        row0 = jnp.where(tile < n_a, tile * tn, b_row0 + (tile - n_a) * tn)
        return pltpu.make_async_copy(wt_ref.at[pl.ds(pl.multiple_of(row0, 8), tn), :], stage, sem)

    @pl.when(pl.program_id(1) == 0)
    def _():
        @pl.when(j == 0)
        def _():
            fetch(j).start()

        fetch(j).wait()
        w_s[...] = stage[...].astype(BF16)

        @pl.when(j + 1 < n_tiles)
        def _():
            fetch(j + 1).start()

    o_ref[...] = _nt(u_ref[...], w_s[...]).astype(o_ref.dtype)


def _inproj(u, w_t, n_a, b_row0, n_b, *, out_first_b):
    t, d = u.shape
    tm = min(2048, t)
    tn = 1024
    assert b_row0 % 8 == 0 and b_row0 + n_b * tn <= w_t.shape[0]

    def out_col(j):
        if not out_first_b:
            return j
        return jnp.where(j < n_a, j + n_b, j - n_a)

    return pl.pallas_call(
        functools.partial(_inproj_body, tn=tn, n_a=n_a, b_row0=b_row0, n_tiles=n_a + n_b),
        grid=(n_a + n_b, t // tm),
        in_specs=[pl.BlockSpec((tm, d), lambda j, i: (i, 0)),
                  pl.BlockSpec(memory_space=pl.ANY)],
        out_specs=pl.BlockSpec((tm, tn), lambda j, i: (i, out_col(j))),
        out_shape=jax.ShapeDtypeStruct((t, (n_a + n_b) * tn), BF16),
        scratch_shapes=[pltpu.VMEM((tn, d), F32), pltpu.VMEM((tn, d), BF16), pltpu.SemaphoreType.DMA(())],
        compiler_params=_cparams("arbitrary", "arbitrary"),
        name="inproj",
    )(u, w_t)


def _moba_body(q_ref, k_ref, v_ref, cos_ref, sin_ref, o_ref, kr_s, vt_s, km_s, *, nblk, hp):
    blk = MOBA_BLOCK
    half = ROPE_DIM // 2
    lane = lax.broadcasted_iota(jnp.int32, (blk, HEAD_DIM), 1)
    eye_d = _eye(HEAD_DIM, BF16)
    eye_b = _eye(blk, BF16)
    scale = HEAD_DIM ** -0.5

    def rope(xf, rows):
        partner = jnp.where(lane < half, pltpu.roll(xf, HEAD_DIM - half, 1), pltpu.roll(xf, half, 1))
        return xf * cos_ref[rows, :] + partner * sin_ref[rows, :]

    def cols(h):
        return slice(h * HEAD_DIM, (h + 1) * HEAD_DIM)

    km_s[...] = jnp.zeros_like(km_s)
    for j in range(nblk):
        rows = slice(j * blk, (j + 1) * blk)
        for h in range(hp):
            kc = rope(k_ref[rows, cols(h)].astype(F32), rows)
            kr_s[rows, cols(h)] = kc.astype(BF16)
            km_s[j:j + 1, cols(h)] = jnp.mean(kc, axis=0, keepdims=True)
            vt_s[cols(h), rows] = _nt(eye_d, v_ref[rows, cols(h)]).astype(BF16)

    key_i = lax.broadcasted_iota(jnp.int32, (blk, blk), 0)
    qry_i = lax.broadcasted_iota(jnp.int32, (blk, blk), 1)
    blk_i = lax.broadcasted_iota(jnp.int32, (8, blk), 0)
    causal_bias = jnp.where(key_i <= qry_i, 0.0, NEG)

    def rows(j):
        return slice(j * blk, (j + 1) * blk)

    def begin(i, h):
        qc = rope(q_ref[rows(i), cols(h)].astype(F32), rows(i))
        st = dict(i=i, h=h, qs=(qc * scale).astype(BF16), scores=[], m=None, bias=None)
        if i > 0:
            g_t = _nt(km_s[:, cols(h)], qc, HI)
            rank = jnp.zeros((8, blk), F32)
            for jp in range(i):
                row = g_t[jp:jp + 1, :]
                beats = (row > g_t) | ((row == g_t) & (jp < blk_i))
                rank = rank + jnp.where(beats, 1.0, 0.0)
            sel = (rank < float(MOBA_TOPK)) & (blk_i < i)
            st["bias"] = jnp.where(sel, 0.0, NEG)
        return st

    def score(st, j):
        i = st["i"]
        s = _nt(kr_s[rows(j), cols(st["h"])], st["qs"])
        if j == i:
            s = s + causal_bias
        st["scores"].append(s)
        mj = jnp.max(s, axis=0, keepdims=True)
        if j < i:
            mj = mj + st["bias"][j:j + 1, :]
        st["m"] = mj if st["m"] is None else jnp.maximum(st["m"], mj)

    def accumulate(st, j):
        i = st["i"]
        if j == 0:
            st["den"] = jnp.zeros((1, blk), F32)
            st["acc"] = jnp.zeros((HEAD_DIM, blk), F32)
        shift = (st["bias"][j:j + 1, :] - st["m"]) if j < i else -st["m"]
        p = jnp.exp(st["scores"][j] + shift)
        st["den"] = st["den"] + jnp.sum(p, axis=0, keepdims=True)
        st["acc"] = st["acc"] + _mm(vt_s[cols(st["h"]), rows(j)], p.astype(BF16))
        if j == i:
            o_t = (st["acc"] / st["den"]).astype(BF16)
            o_ref[rows(i), cols(st["h"])] = _nt(eye_b, o_t).astype(o_ref.dtype)

    prev = None
    for i in range(nblk):
        cur = [begin(i, h) for h in range(hp)]
        for j in range(i + 1):
            for h in range(hp):
                score(cur[h], j)
                if prev is not None and j <= i - 1:
                    accumulate(prev[h], j)
        prev = cur
    for j in range(nblk):
        for h in range(hp):
            accumulate(prev[h], j)


def _moba(proj, cos_t, sin_t, *, q_blk0, k_blk0, v_blk0):
    bsz, seq, _ = proj.shape
    nblk = seq // MOBA_BLOCK
    assert seq % MOBA_BLOCK == 0 and 1 <= nblk <= 8
    hd = HEAD_DIM
    hp = MOBA_HEADS_PER_STEP
    wide = hp * hd
    assert MOBA_HEADS % hp == 0 and all(b0 % hp == 0 for b0 in (q_blk0, k_blk0, v_blk0))
    return pl.pallas_call(
        functools.partial(_moba_body, nblk=nblk, hp=hp),
        grid=(bsz, MOBA_HEADS // hp),
        in_specs=[pl.BlockSpec((None, seq, wide), lambda b, h: (b, 0, q_blk0 // hp + h)),
                  pl.BlockSpec((None, seq, wide), lambda b, h: (b, 0, k_blk0 // hp + h)),
                  pl.BlockSpec((None, seq, wide), lambda b, h: (b, 0, v_blk0 // hp + h)),
                  pl.BlockSpec((None, seq, hd), lambda b, h: (b, 0, 0)),
                  pl.BlockSpec((None, seq, hd), lambda b, h: (b, 0, 0))],
        out_specs=pl.BlockSpec((None, seq, wide), lambda b, h: (b, 0, h)),
        out_shape=jax.ShapeDtypeStruct((bsz, seq, MOBA_HEADS * hd), BF16),
        scratch_shapes=[pltpu.VMEM((seq, wide), BF16),
                        pltpu.VMEM((wide, seq), BF16),
                        pltpu.VMEM((8, wide), F32)],
        compiler_params=_cparams("parallel", "parallel"),
        name="moba",
    )(proj, proj, proj, cos_t, sin_t)


def _chunk_masks(n, chunk):
    r = lax.broadcasted_iota(jnp.int32, (n, n), 0)
    c = lax.broadcasted_iota(jnp.int32, (n, n), 1)
    shift = int(math.log2(chunk))
    same = jnp.right_shift(r, shift) == jnp.right_shift(c, shift)
    return r, c, same


def _gates_body(ba_ref, par_ref, beta_ref, g_ref, gl_ref, *, seq):
    grp = DN_GROUP
    r, c, same = _chunk_masks(grp, DN_CHUNK)
    sums = jnp.concatenate([jnp.where(same & (c <= r), 1.0, 0.0), jnp.where(same, 1.0, 0.0)], axis=0).astype(BF16)
    neg_a = -jnp.exp(par_ref[0:1, :])
    dt_b = par_ref[1:2, :]
    for i in range(seq // grp):
        rows = slice(i * grp, (i + 1) * grp)
        x = ba_ref[rows, :].astype(F32)
        beta_ref[rows, :] = jax.nn.sigmoid(x)
        g = neg_a * _softplus(x + dt_b)
        g_hi = g.astype(BF16)
        rest = g - g_hi.astype(F32)
        g_mid = rest.astype(BF16)
        g_lo = (rest - g_mid.astype(F32)).astype(BF16)
        acc = _mm(sums, g_hi) + _mm(sums, g_mid) + _mm(sums, g_lo)
        g_ref[rows, :] = acc[:grp, :]
        gl_ref[rows, :] = acc[grp:, :]


def _gates(ba, par):
    bsz, seq, _ = ba.shape
    spec = pl.BlockSpec((None, seq, LANES), lambda b: (b, 0, 0))
    return pl.pallas_call(
        functools.partial(_gates_body, seq=seq),
        grid=(bsz,),
        in_specs=[spec, pl.BlockSpec((8, LANES), lambda b: (0, 0))],
        out_specs=[spec, spec, spec],
        out_shape=[jax.ShapeDtypeStruct((bsz, seq, LANES), F32)] * 3,
        compiler_params=_cparams("parallel"),
        name="dn_gates",
    )(ba, par)


def _dn_body(q_ref, k_ref, v_ref, z_ref, cwq_ref, cwk_ref, cwv_ref, beta_ref, g_ref, gl_ref, grow_ref,
             gain_ref, o_ref, stage_s, qkv_a, qkv_b, state_s, vnew_s, *bufs, seq, hp):
    bufs_a, bufs_b = bufs[:5], bufs[5:]
    hg = pl.program_id(1)
    grp = DN_GROUP
    chunk = DN_CHUNK
    ngrp = seq // grp
    hd = HEAD_DIM
    pad = DN_CONV_PAD

    crows = DN_CONV_ROWS
    r, c, same = _chunk_masks(grp, chunk)
    low_incl = same & (c <= r)
    low_strict = same & (c < r)
    eye_g = jnp.where(r == c, 1.0, 0.0)
    eye_d = _eye(hd, BF16)
    lane = lax.broadcasted_iota(jnp.int32, (grp, LANES), 1)
    col_chunk = jnp.right_shift(lax.broadcasted_iota(jnp.int32, (hd, grp), 1), int(math.log2(chunk)))
    gain = gain_ref[...]
    n_double = int(math.log2(chunk)) - 1

    def pick(ref, r0, lane_id):
        return jnp.sum(jnp.where(lane == lane_id, ref[pl.ds(r0, grp), :], 0.0), axis=-1, keepdims=True)

    state_s[...] = jnp.zeros_like(state_s)
    heads = range(hp)

    cols = [slice(hh * hd, (hh + 1) * hd) for hh in heads]
    head = [hg * hp + hh for hh in heads]

    tensors = ((q_ref, cwq_ref, True, hd ** -0.5), (k_ref, cwk_ref, True, None), (v_ref, cwv_ref, False, None))
    nsub = grp // crows

    def conv(gi, qkv, first=False):
        r0 = pl.multiple_of(gi * grp, grp)
        for ti, (src, cw_ref, l2, post) in enumerate(tensors):
            cw = cw_ref[...]
            for sb in range(nsub):
                stage_s[ti, sb, pad:pad + crows, :] = src[pl.ds(r0 + sb * crows, crows), :].astype(F32)
            if first:
                stage_s[ti, 0, 0:pad, :] = jnp.zeros((pad, hp * hd), F32)
            else:
                before = src[pl.ds(pl.multiple_of(r0 - 2 * pad, 2 * pad), 2 * pad), :].astype(F32)
                stage_s[ti, 0, 0:pad, :] = before[pad:, :]
            for sb in range(1, nsub):
                stage_s[ti, sb, 0:pad, :] = stage_s[ti, sb - 1, crows:crows + pad, :]
            for sb in range(nsub):
                for hh in heads:
                    acc = None
                    for j in range(DN_CONV_WIDTH):
                        off = pad - (DN_CONV_WIDTH - 1) + j
                        term = stage_s[ti, sb, off:off + crows, cols[hh]] * cw[j:j + 1, cols[hh]]
                        acc = term if acc is None else acc + term
                    yh = _silu(acc)
                    if l2:
                        yh = yh * lax.rsqrt(jnp.sum(yh * yh, axis=-1, keepdims=True) + NORM_EPS)
                    if post is not None:
                        yh = yh * post
                    qkv[ti, sb * crows:(sb + 1) * crows, cols[hh]] = yh
                yield

    def prepare(gi, qkv, bufs):
        u_r, w_r, qk_r, qd_r, kt_r = bufs
        r0 = pl.multiple_of(gi * grp, grp)
        q = [qkv[0, :, cols[hh]] for hh in heads]
        k = [qkv[1, :, cols[hh]] for hh in heads]
        v = [qkv[2, :, cols[hh]] for hh in heads]
        beta = [pick(beta_ref, r0, head[hh]) for hh in heads]
        g_col = [pick(g_ref, r0, DN_HEADS + head[hh]) for hh in heads]
        gl_col = [pick(gl_ref, r0, DN_HEADS + head[hh]) for hh in heads]
        decay = [jnp.exp(jnp.minimum(g_col[hh] - grow_ref[hh, gi], 0.0)) for hh in heads]
        e_g = [jnp.exp(g_col[hh]) for hh in heads]
        kb = [k[hh] * beta[hh] for hh in heads]
        vb = [(v[hh] * beta[hh]).astype(BF16) for hh in heads]
        k16 = [k[hh].astype(BF16) for hh in heads]
        n_mat = [jnp.where(low_strict, _nt(kb[hh].astype(BF16), k16[hh]) * decay[hh], 0.0) for hh in heads]
        p = [(-n_mat[hh]).astype(BF16) for hh in heads]
        x = [eye_g - n_mat[hh] for hh in heads]
        qk = [jnp.where(low_incl, _nt(q[hh].astype(BF16), k16[hh]) * decay[hh], 0.0).astype(BF16) for hh in heads]
        for hh in heads:
            qk_r[hh] = qk[hh]
            qd_r[hh] = (q[hh] * e_g[hh]).astype(BF16)
            kt_r[hh] = _nt(eye_d, (k[hh] * jnp.exp(gl_col[hh] - g_col[hh])).astype(BF16)).astype(BF16)
        yield
        for _ in range(n_double):
            p2 = [_mm(p[hh], p[hh]).astype(BF16) for hh in heads]
            x = [x[hh] + _mm(x[hh].astype(BF16), p2[hh]) for hh in heads]
            p = p2
            yield
        x16 = [x[hh].astype(BF16) for hh in heads]
        for hh in heads:
            u_r[hh] = _mm(x16[hh], vb[hh])
            w_r[hh] = _mm(x16[hh], (kb[hh] * e_g[hh]).astype(BF16)).astype(BF16)
        yield

    def scan(gi, bufs):
        u_r, w_r, qk_r, qd_r, kt_r = bufs
        r0 = pl.multiple_of(gi * grp, grp)
        vnew_s[...] = jnp.zeros_like(vnew_s)
        for ci in range(grp // chunk):
            rows = slice(ci * chunk, (ci + 1) * chunk)
            gl_row = jnp.exp(gl_ref[pl.ds(r0 + ci * chunk, 1), :])
            for hh in heads:
                state = state_s[hh]
                s16 = state.astype(BF16)
                v_new = u_r[hh, rows, :] - _mm(w_r[hh, rows, :], s16)
                vnew_s[hh, rows, :] = v_new.astype(BF16)
                o = _mm(qd_r[hh, rows, :], s16) + _mm(qk_r[hh, rows, :], vnew_s[hh])
                chunk_decay = jnp.sum(jnp.where(lane[0:1, :] == DN_HEADS + head[hh], gl_row, 0.0),
                                      axis=-1, keepdims=True)
                k_tail_t = kt_r[hh]
                kt = jnp.where(col_chunk == ci, k_tail_t, jnp.zeros_like(k_tail_t))
                state_s[hh] = state * chunk_decay + _mm(kt, vnew_s[hh])
                on = o * lax.rsqrt(jnp.mean(o * o, axis=-1, keepdims=True) + NORM_EPS) * gain
                zz = z_ref[pl.ds(r0 + ci * chunk, chunk), cols[hh]].astype(F32)
                o_ref[pl.ds(r0 + ci * chunk, chunk), cols[hh]] = (on * _silu(zz)).astype(o_ref.dtype)
            yield

    def interleave(first, *others):
        for _ in first:
            for _ in range(2):
                for steps in others:
                    next(steps, None)
        for steps in others:
            for _ in steps:
                pass

    interleave(conv(0, qkv_a, first=True))
    interleave(prepare(0, qkv_a, bufs_a), conv(1, qkv_b))

    def pair(pi, carry):
        g0 = 2 * pi
        interleave(scan(g0, bufs_a), prepare(g0 + 1, qkv_b, bufs_b), conv(g0 + 2, qkv_a))
        interleave(scan(g0 + 1, bufs_b), prepare(g0 + 2, qkv_a, bufs_a), conv(g0 + 3, qkv_b))
        return carry

    lax.fori_loop(0, ngrp // 2 - 1, pair, 0)
    interleave(scan(ngrp - 2, bufs_a), prepare(ngrp - 1, qkv_b, bufs_b))
    interleave(scan(ngrp - 1, bufs_b))


def _deltanet(proj, conv_w, beta, gcum, glast, grow, gain, *, q_blk0, k_blk0, v_blk0, z_blk0):
    bsz, seq, _ = proj.shape
    hd = HEAD_DIM
    nh = DN_HEADS
    assert seq % DN_GROUP == 0
    ngrp = seq // DN_GROUP

    hp = DN_HEADS_PER_STEP
    wide = hp * hd
    assert nh % hp == 0 and all(b0 % hp == 0 for b0 in (q_blk0, k_blk0, v_blk0, z_blk0))

    def col(blk0):
        return pl.BlockSpec((None, seq, wide), lambda b, h: (b, 0, blk0 // hp + h))

    def cw(blk0):
        return pl.BlockSpec((DN_CONV_WIDTH, wide), lambda b, h: (0, blk0 // hp + h))

    full = pl.BlockSpec((None, seq, LANES), lambda b, h: (b, 0, 0))
    return pl.pallas_call(
        functools.partial(_dn_body, seq=seq, hp=hp),
        grid=(bsz, nh // hp),
        in_specs=[col(q_blk0), col(k_blk0), col(v_blk0), col(z_blk0),
                  cw(0), cw(nh), cw(2 * nh),
                  full, full, full,
                  pl.BlockSpec((hp, ngrp, 1, DN_GROUP), lambda b, h: (b * (nh // hp) + h, 0, 0, 0)),
                  pl.BlockSpec((1, hd), lambda b, h: (0, 0))],
        out_specs=pl.BlockSpec((None, seq, wide), lambda b, h: (b, 0, h)),
        out_shape=jax.ShapeDtypeStruct((bsz, seq, nh * hd), BF16),
        scratch_shapes=[pltpu.VMEM((3, DN_GROUP // DN_CONV_ROWS, DN_CONV_ROWS + DN_CONV_PAD, wide), F32)]
        + [pltpu.VMEM((3, DN_GROUP, wide), F32)] * 2
        + [pltpu.VMEM((hp, hd, hd), F32), pltpu.VMEM((hp, DN_GROUP, hd), BF16)]
        + [pltpu.VMEM((hp, DN_GROUP, hd), F32), pltpu.VMEM((hp, DN_GROUP, hd), BF16),
           pltpu.VMEM((hp, DN_GROUP, DN_GROUP), BF16), pltpu.VMEM((hp, DN_GROUP, hd), BF16),
           pltpu.VMEM((hp, hd, DN_GROUP), BF16)] * 2,
        compiler_params=_cparams("parallel", "parallel"),
        name="deltanet",
    )(proj, proj, proj, proj, conv_w, conv_w, conv_w, beta, gcum, glast, grow, gain)


def _merge_body(ya_ref, yb_ref, ga_ref, gb_ref, x_ref, wm_ref, wd_ref, wo_ref, wr_ref, br_ref,
                npost_ref, npre_ref, gt_ref, sc_ref, sh_ref, x1_ref, u2_ref, lg_ref):
    sub = MERGE_SUB_ROWS
    for s in range(x_ref.shape[0] // sub):
        rows = slice(s * sub, (s + 1) * sub)
        ma = _mm(ya_ref[rows, :], wm_ref[...])
        mb = _mm(yb_ref[rows, :], wd_ref[...])
        merged = (jax.nn.sigmoid(ga_ref[rows, :].astype(F32)) * ma
                  + jax.nn.sigmoid(gb_ref[rows, :].astype(F32)) * mb)
        y = _mm(merged.astype(BF16), wo_ref[...])
        yn = y * lax.rsqrt(jnp.mean(y * y, axis=-1, keepdims=True) + NORM_EPS) * npost_ref[...]
        x1 = x_ref[rows, :] + gt_ref[...] * yn
        x1_ref[rows, :] = x1
        un = x1 * lax.rsqrt(jnp.mean(x1 * x1, axis=-1, keepdims=True) + NORM_EPS) * npre_ref[...]
        u2 = un * (1.0 + sc_ref[...]) + sh_ref[...]
        half = u2.shape[1] // 2
        u2_ref[rows, :] = _pack_bf16_pair(u2[:, :half], u2[:, half:])
        u_hi = u2.astype(BF16)
        u_lo = (u2 - u_hi.astype(F32)).astype(BF16)
        lg2 = _nt(wr_ref[...], u_hi)
        lg_ref[:, rows] = lg2[:LANES, :] + lg2[LANES:, :] + _nt(wr_ref[:LANES, :], u_lo) + br_ref[...]


def _merge(ya, yb, proj2, x2, wm, wd, wo, wr, br, npost, npre, mod3, seq, *, ga_blk, gb_blk):
    t, d = x2.shape
    wa = ya.shape[1]
    tm = min(256, seq)
    per_b = seq // tm
    const = lambda i: (0, 0)
    once = dict(pipeline_mode=pl.Buffered(1))

    def modspec(chunk):
        return pl.BlockSpec((None, 1, d), lambda i: (i // per_b, 0, chunk))

    return pl.pallas_call(
        _merge_body,
        grid=(t // tm,),
        in_specs=[pl.BlockSpec((tm, wa), lambda i: (i, 0)),
                  pl.BlockSpec((tm, wa), lambda i: (i, 0)),
                  pl.BlockSpec((tm, d), lambda i: (i, ga_blk)),
                  pl.BlockSpec((tm, d), lambda i: (i, gb_blk)),
                  pl.BlockSpec((tm, d), lambda i: (i, 0)),
                  pl.BlockSpec((wa, d), const, **once),
                  pl.BlockSpec((wa, d), const, **once),
                  pl.BlockSpec((d, d), const, **once),
                  pl.BlockSpec((2 * LANES, d), const, **once),
                  pl.BlockSpec((LANES, 1), const),
                  pl.BlockSpec((1, d), const),
                  pl.BlockSpec((1, d), const),
                  modspec(2), modspec(4), modspec(3)],
        out_specs=[pl.BlockSpec((tm, d), lambda i: (i, 0)),
                   pl.BlockSpec((tm, d // 2), lambda i: (i, 0)),
                   pl.BlockSpec((LANES, tm), lambda i: (0, i))],
        out_shape=[jax.ShapeDtypeStruct((t, d), F32),
                   jax.ShapeDtypeStruct((t, d // 2), jnp.uint32),
                   jax.ShapeDtypeStruct((LANES, t), F32)],
        compiler_params=_cparams("parallel"),
        name="merge",
    )(ya, yb, proj2, proj2, x2, wm, wd, wo, wr, br, npost, npre, mod3, mod3, mod3)


def _route_body(lg_ref, info_ref, col_ref, cnt_ref, run_s, *, tr):
    @pl.when(pl.program_id(0) == 0)
    def _():
        run_s[...] = jnp.zeros_like(run_s)

    lg = lg_ref[...]
    row = lax.broadcasted_iota(jnp.int32, (LANES, tr), 0)
    row_f = row.astype(F32)
    big = float(LANES)

    def first_max(vals, mask):
        mx = jnp.max(jnp.where(mask, vals, NEG), axis=0, keepdims=True)
        idx = jnp.min(jnp.where(mask & (vals == mx), row_f, big), axis=0, keepdims=True)
        return mx, idx

    gmask = row < MOE_GROUPS
    gmax, gidx = first_max(lg, gmask)
    p_group = 1.0 / jnp.sum(jnp.where(gmask, jnp.exp(lg - gmax), 0.0), axis=0, keepdims=True)
    lo = float(MOE_GROUPS) + gidx * float(MOE_EXPERTS_PER_GROUP)
    emask = (row_f >= lo) & (row_f < lo + float(MOE_EXPERTS_PER_GROUP))
    m1, i1 = first_max(lg, emask)
    m2, i2 = first_max(lg, emask & (row_f != i1))
    e2 = jnp.exp(m2 - m1)
    w1 = p_group / (1.0 + e2)
    w2 = p_group * e2 / (1.0 + e2)
    oh1 = row_f == i1
    oh2 = row_f == i2
    oh = jnp.where(oh1 | oh2, 1.0, 0.0).astype(BF16)
    r = lax.broadcasted_iota(jnp.int32, (tr, tr), 0)
    c = lax.broadcasted_iota(jnp.int32, (tr, tr), 1)
    before = jnp.where(r < c, 1.0, 0.0).astype(BF16)
    prefix = _mm(oh, before) + run_s[:, 0:1]
    rank1 = jnp.sum(jnp.where(oh1, prefix, 0.0), axis=0, keepdims=True)
    rank2 = jnp.sum(jnp.where(oh2, prefix, 0.0), axis=0, keepdims=True)
    run_s[...] = run_s[...] + jnp.sum(oh.astype(F32), axis=1, keepdims=True)
    goff = float(MOE_GROUPS)
    info = jnp.where(row == 0, i1 - goff, 0.0)
    info = jnp.where(row == 1, i2 - goff, info)
    info = jnp.where(row == 2, rank1, info)
    info = jnp.where(row == 3, rank2, info)
    info = jnp.where(row == 4, w1, info)
    info = jnp.where(row == 5, w2, info)
    info_ref[...] = info[0:8, :]
    eye = _eye(tr, BF16)
    hi = info.astype(BF16)
    rest = info - hi.astype(F32)
    mid = rest.astype(BF16)
    col_ref[...] = _nt(eye, hi) + _nt(eye, mid) + _nt(eye, (rest - mid.astype(F32)).astype(BF16))
    cnt_ref[...] = run_s[...]


def _route(logits_t):
    t = logits_t.shape[1]
    tr = min(512, t)
    return pl.pallas_call(
        functools.partial(_route_body, tr=tr),
        grid=(t // tr,),
        in_specs=[pl.BlockSpec((LANES, tr), lambda i: (0, i))],
        out_specs=[pl.BlockSpec((8, tr), lambda i: (0, i)),
                   pl.BlockSpec((tr, LANES), lambda i: (i, 0)),
                   pl.BlockSpec((LANES, LANES), lambda i: (0, 0))],
        out_shape=[jax.ShapeDtypeStruct((8, t), F32),
                   jax.ShapeDtypeStruct((t, LANES), F32),
                   jax.ShapeDtypeStruct((LANES, LANES), F32)],
        scratch_shapes=[pltpu.VMEM((LANES, LANES), F32)],
        compiler_params=_cparams("arbitrary"),
        name="route",
    )(logits_t)


def _dispatch_body(d0_ref, d1_ref, u_ref, xs_ref, sem, *, td):
    dests = (d0_ref, d1_ref)

    def row_copy(r, k):
        return pltpu.make_async_copy(u_ref.at[pl.ds(r, 1), :], xs_ref.at[pl.ds(dests[k][r], 1), :], sem)

    def start(r, carry):
        for k in range(MOE_TOPK):
            row_copy(r, k).start()
        return carry

    def wait(r, carry):
        for k in range(MOE_TOPK):
            row_copy(r, k).wait()
        return carry

    lax.fori_loop(0, td, start, 0, unroll=DMA_LOOP_UNROLL)
    lax.fori_loop(0, td, wait, 0, unroll=DMA_LOOP_UNROLL)


def _dispatch(dest, u2, n_rows):
    t, d = u2.shape
    td = min(1024, t)
    nt = t // td
    return pl.pallas_call(
        functools.partial(_dispatch_body, td=td),
        grid=(nt,),
        in_specs=[pl.BlockSpec((td,), lambda i: (i,), memory_space=pltpu.SMEM),
                  pl.BlockSpec((td,), lambda i: (nt + i,), memory_space=pltpu.SMEM),
                  pl.BlockSpec((td, d), lambda i: (i, 0))],
        out_specs=pl.BlockSpec(memory_space=pl.ANY),
        out_shape=jax.ShapeDtypeStruct((n_rows, d), u2.dtype),
        scratch_shapes=[pltpu.SemaphoreType.DMA(())],
        compiler_params=_cparams("arbitrary"),
        name="dispatch",
    )(dest, dest, u2)


def _experts_body(be_ref, nx_ref, nv_ref, nu_ref, x_ref, wg_ref, wu_ref, wd_ref, o_ref,
                  stage_g, stage_u, stage_d, wg_s, wu_s, wd_s, sems):
    i = pl.program_id(0)

    def fetch(e):
        return (pltpu.make_async_copy(wg_ref.at[e], stage_g, sems.at[0]),
                pltpu.make_async_copy(wu_ref.at[e], stage_u, sems.at[1]),
                pltpu.make_async_copy(wd_ref.at[e], stage_d, sems.at[2]))

    @pl.when(i == 0)
    def _():
        for cp in fetch(be_ref[0]):
            cp.start()

    @pl.when(jnp.logical_or(i == 0, be_ref[i] != be_ref[jnp.maximum(i - 1, 0)]))
    def _():
        for cp in fetch(be_ref[i]):
            cp.wait()
        wg_s[...] = stage_g[...].astype(BF16)
        wu_s[...] = stage_u[...].astype(BF16)
        wd_s[...] = stage_d[...].astype(BF16)

        @pl.when(nx_ref[i] >= 0)
        def _():
            for cp in fetch(nx_ref[i]):
                cp.start()

    @pl.when(i < nu_ref[0])
    def _():
        row = lax.broadcasted_iota(jnp.int32, (x_ref.shape[0], 1), 0)
        packed = jnp.where(row < nv_ref[i], x_ref[...], jnp.uint32(0))
        x_lo, x_hi = (v.astype(BF16) for v in _unpack_bf16_pair(packed))
        half = packed.shape[1]
        gate = _mm(x_lo, wg_s[0:half, :]) + _mm(x_hi, wg_s[half:, :])
        up = _mm(x_lo, wu_s[0:half, :]) + _mm(x_hi, wu_s[half:, :])
        y = _mm((_silu(gate) * up).astype(BF16), wd_s[...])
        o_ref[...] = _pack_bf16_pair(y[:, :half], y[:, half:])

    @pl.when(i >= nu_ref[0])
    def _():
        o_ref[...] = jnp.zeros_like(o_ref)


def _experts(block_expert, next_expert, block_valid, n_used, xs, wg, wu, wd):
    nr, dp = xs.shape
    d, ff = wg.shape[1], wg.shape[2]
    assert d == 2 * dp
    rb = MOE_ROWS
    row_map = lambda i, be, nx, nv, nu: (jnp.minimum(i, nu[0] - 1), 0)
    hbm = pl.BlockSpec(memory_space=pl.ANY)
    grid_spec = pltpu.PrefetchScalarGridSpec(
        num_scalar_prefetch=4,
        grid=(nr // rb,),
        in_specs=[pl.BlockSpec((rb, dp), row_map), hbm, hbm, hbm],
        out_specs=pl.BlockSpec((rb, dp), lambda i, be, nx, nv, nu: (i, 0)),
        scratch_shapes=[pltpu.VMEM((d, ff), F32), pltpu.VMEM((d, ff), F32), pltpu.VMEM((ff, d), F32),
                        pltpu.VMEM((d, ff), BF16), pltpu.VMEM((d, ff), BF16), pltpu.VMEM((ff, d), BF16),
                        pltpu.SemaphoreType.DMA((3,))],
    )
    return pl.pallas_call(
        _experts_body,
        grid_spec=grid_spec,
        out_shape=jax.ShapeDtypeStruct((nr, dp), jnp.uint32),
        compiler_params=_cparams("arbitrary"),
        name="experts",
    )(block_expert, next_expert, block_valid, n_used, xs, wg, wu, wd)


def _combine_body(d0_ref, d1_ref, n0_ref, n1_ref, info_ref, x1_ref, npost_ref, gt_ref, yb_ref, o_ref,
                  buf, sems, *, tc, rc, nt):
    i = pl.program_id(0)
    slot = lax.rem(i, 2)
    own = (d0_ref, d1_ref)
    ahead = (n0_ref, n1_ref)

    def row_copy(dests, sl, r, k):
        return pltpu.make_async_copy(yb_ref.at[pl.ds(dests[k][r], 1), :], buf.at[sl, k, pl.ds(r, 1), :],
                                     sems.at[sl])

    @pl.when(i == 0)
    def _():
        def start(r, carry):
            for k in range(MOE_TOPK):
                row_copy(own, slot, r, k).start()
            return carry

        lax.fori_loop(0, tc, start, 0, unroll=DMA_LOOP_UNROLL)

    def wait(r, carry):
        for k in range(MOE_TOPK):
            row_copy(own, slot, r, k).wait()
        return carry

    lax.fori_loop(0, tc, wait, 0, unroll=DMA_LOOP_UNROLL)

    half = buf.shape[3]
    gain_lo, gain_hi = npost_ref[:, :half], npost_ref[:, half:]
    gate_lo, gate_hi = gt_ref[:, :half], gt_ref[:, half:]

    def chunk(ci, carry, request_next):
        r0 = pl.multiple_of(ci * rc, rc)
        info = info_ref[pl.ds(r0, rc), :]
        w0, w1 = info[:, 4:5], info[:, 5:6]
        a_lo, a_hi = _unpack_bf16_pair(buf[slot, 0, pl.ds(r0, rc), :])
        b_lo, b_hi = _unpack_bf16_pair(buf[slot, 1, pl.ds(r0, rc), :])
        y_lo = w0 * a_lo + w1 * b_lo
        y_hi = w0 * a_hi + w1 * b_hi
        ms = (jnp.sum(y_lo * y_lo, axis=-1, keepdims=True)
              + jnp.sum(y_hi * y_hi, axis=-1, keepdims=True)) * (1.0 / (2 * half))
        inv = lax.rsqrt(ms + NORM_EPS)
        o_ref[pl.ds(r0, rc), :half] = x1_ref[pl.ds(r0, rc), :half] + gate_lo * (y_lo * inv * gain_lo)
        o_ref[pl.ds(r0, rc), half:] = x1_ref[pl.ds(r0, rc), half:] + gate_hi * (y_hi * inv * gain_hi)
        if request_next:
            for rr in range(rc):
                for k in range(MOE_TOPK):
                    row_copy(ahead, 1 - slot, r0 + rr, k).start()
        return carry

    @pl.when(i + 1 < nt)
    def _():
        lax.fori_loop(0, tc // rc, functools.partial(chunk, request_next=True), 0)

    @pl.when(i + 1 == nt)
    def _():
        lax.fori_loop(0, tc // rc, functools.partial(chunk, request_next=False), 0)


def _combine(dest, info, x1, npost, mod3, yb, seq):
    t, d = x1.shape
    tc = min(1024, seq)
    rc = min(128, tc)
    per_b = seq // tc
    nt = t // tc
    nxt = lambda i: jnp.minimum(i + 1, nt - 1)
    return pl.pallas_call(
        functools.partial(_combine_body, tc=tc, rc=rc, nt=nt),
        grid=(nt,),
        in_specs=[pl.BlockSpec((tc,), lambda i: (i,), memory_space=pltpu.SMEM),
                  pl.BlockSpec((tc,), lambda i: (nt + i,), memory_space=pltpu.SMEM),
                  pl.BlockSpec((tc,), lambda i: (nxt(i),), memory_space=pltpu.SMEM),
                  pl.BlockSpec((tc,), lambda i: (nt + nxt(i),), memory_space=pltpu.SMEM),
                  pl.BlockSpec((tc, LANES), lambda i: (i, 0)),
                  pl.BlockSpec((tc, d), lambda i: (i, 0)),
                  pl.BlockSpec((1, d), lambda i: (0, 0)),
                  pl.BlockSpec((None, 1, d), lambda i: (i // per_b, 0, 5)),
                  pl.BlockSpec(memory_space=pl.ANY)],
        out_specs=pl.BlockSpec((tc, d), lambda i: (i, 0)),
        out_shape=jax.ShapeDtypeStruct((t, d), F32),
        scratch_shapes=[pltpu.VMEM((2, MOE_TOPK, tc, d // 2), jnp.uint32), pltpu.SemaphoreType.DMA((2,))],
        compiler_params=_cparams("arbitrary"),
        name="combine",
    )(dest, dest, dest, dest, info, x1, npost, mod3, yb)


def _rope_tables(positions):
    half = ROPE_DIM // 2
    inv_freq = jnp.power(ROPE_THETA, -jnp.arange(half, dtype=F32) * (2.0 / ROPE_DIM))
    ang =inv_freq[None, :, None] * positions.astype(F32)[:, None, :]
    cos, sin = jnp.cos(ang), jnp.sin(ang)
    rest = (positions.shape[0], HEAD_DIM - ROPE_DIM, positions.shape[1])
    cos_t = jnp.concatenate([cos, cos, jnp.ones(rest, F32)], axis=1)
    sin_t = jnp.concatenate([-sin, sin, jnp.zeros(rest, F32)], axis=1)
    return jnp.transpose(cos_t, (0, 2, 1)), jnp.transpose(sin_t, (0, 2, 1))


def _pad_lanes(v, n=LANES):
    return jnp.pad(v, [(0, 0)] * (v.ndim - 1) + [(0, n - v.shape[-1])])


def _mixer_and_router(x, mod3, positions, norm_mix_pre, norm_mix_post, norm_ffn_pre, w_in, conv_w,
                      dn_a_log, dn_dt_bias, dn_out_norm, w_branch_moba, w_branch_delta, w_out,
                      router_group_w, router_group_b, router_expert_w, router_expert_b):
    bsz, seq, d = x.shape
    t = bsz * seq
    mw = MOBA_HEADS * HEAD_DIM
    dw = DN_HEADS * HEAD_DIM
    o_qa, o_ka, o_va = 0, mw, 2 * mw
    o_dn = 3 * mw
    o_z = o_dn + 3 * dw
    o_ba = o_z + dw
    o_ga = o_ba + 2 * DN_HEADS
    o_gb = o_ga + d
    w_small = _pad_lanes(w_in[:, o_ba:o_ga]).astype(BF16)
    x2 = x.reshape(t, d)
    u, ba2 = _prenorm(x2, norm_mix_pre[None, :], mod3, w_small, seq, sc_chunk=1, sh_chunk=0)
    assert o_ba % 1024 == 0 and (2 * d) % 1024 == 0
    proj2 = _inproj(u, w_in.T, o_ba // 1024, o_ga, 2 * d // 1024, out_first_b=True)
    proj = proj2.reshape(bsz, seq, -1)
    c0 = 2 * d // LANES
    nh = MOBA_HEADS

    cos_t, sin_t = _rope_tables(positions)
    ya = _moba(proj, cos_t, sin_t, q_blk0=c0, k_blk0=c0 + nh, v_blk0=c0 + 2 * nh)

    par = jnp.zeros((8, LANES), F32)
    par = par.at[0, DN_HEADS:2 * DN_HEADS].set(dn_a_log.astype(F32))
    par = par.at[1, DN_HEADS:2 * DN_HEADS].set(dn_dt_bias.astype(F32))
    beta, gcum, glast = _gates(ba2.reshape(bsz, seq, LANES), par)
    ngrp = seq // DN_GROUP
    grow = jnp.transpose(gcum[:, :, DN_HEADS:2 * DN_HEADS], (0, 2, 1)).reshape(bsz * DN_HEADS, ngrp, 1, DN_GROUP)
    d0 = c0 + 3 * nh
    yb = _deltanet(proj, conv_w, beta, gcum, glast, grow, dn_out_norm[None, :].astype(F32),
                   q_blk0=d0, k_blk0=d0 + DN_HEADS, v_blk0=d0 + 2 * DN_HEADS, z_blk0=d0 + 3 * DN_HEADS)

    wr = _pad_lanes(jnp.concatenate([router_group_w, router_expert_w], axis=1)).T
    wr_hi = wr.astype(BF16)
    wr = jnp.concatenate([wr_hi, (wr - wr_hi.astype(F32)).astype(BF16)], axis=0)
    br = _pad_lanes(jnp.concatenate([router_group_b, router_expert_b])[None, :]).T
    return _merge(ya.reshape(t, mw), yb.reshape(t, dw), proj2, x2,
                  w_branch_moba.astype(BF16), w_branch_delta.astype(BF16), w_out.astype(BF16), wr, br,
                  norm_mix_post[None, :], norm_ffn_pre[None, :], mod3, seq, ga_blk=0, gb_blk=1)


def _moe(x1, u2, logits, mod3, norm_ffn_post, w_gate, w_up, w_down, seq):
    t, d = x1.shape
    info_t, info, counts = _route(logits)
    counts = counts[MOE_GROUPS:MOE_GROUPS + MOE_EXPERTS, 0].astype(jnp.int32)
    rb = MOE_ROWS
    padded = (counts + rb - 1) // rb * rb
    pad_end = jnp.cumsum(padded)
    pad_start = pad_end - padded
    eid = info_t[0:MOE_TOPK].astype(jnp.int32)
    rank = info_t[MOE_TOPK:2 * MOE_TOPK].astype(jnp.int32)
    experts = jnp.arange(MOE_EXPERTS, dtype=jnp.int32)[:, None, None]
    start = jnp.sum(jnp.where(eid[None] == experts, pad_start[:, None, None], 0), axis=0)
    dest_flat = (start + rank).reshape(-1)
    n_blocks = (t * MOE_TOPK + MOE_EXPERTS * (rb - 1)) // rb + 1
    n_used = (pad_end[-1] // rb).astype(jnp.int32)
    blk_row = jnp.minimum(jnp.arange(n_blocks, dtype=jnp.int32), n_used - 1) * rb
    block_expert = jnp.minimum(jnp.sum(pad_end[None, :] <= blk_row[:, None], axis=1),
                               MOE_EXPERTS - 1).astype(jnp.int32)
    ids = jnp.arange(MOE_EXPERTS, dtype=jnp.int32)
    later_used = (ids[None, :] > ids[:, None]) & (counts[None, :] > 0)
    next_used = jnp.min(jnp.where(later_used, ids[None, :], MOE_EXPERTS), axis=1)
    next_used = jnp.where(next_used < MOE_EXPERTS, next_used, -1).astype(jnp.int32)
    own = block_expert[:, None] == ids[None, :]
    look = lambda table: jnp.sum(jnp.where(own, table[None, :], 0), axis=1)
    block_valid = jnp.clip(look(counts) - (blk_row - look(pad_start)), 0, rb).astype(jnp.int32)
    xs = _dispatch(dest_flat, u2, n_blocks * rb)
    ys = _experts(block_expert, look(next_used), block_valid, n_used[None], xs, w_gate, w_up, w_down)
    return _combine(dest_flat, info, x1, norm_ffn_post[None, :], mod3, ys, seq)


def kernel(x, c, positions, w_ada, b_ada, norm_mix_pre, norm_mix_post, norm_ffn_pre, norm_ffn_post, w_in, conv_w, dn_a_log, dn_dt_bias, dn_out_norm, w_branch_moba, w_branch_delta, w_out, router_group_w, router_group_b, router_expert_w, router_expert_b, expert_w_gate, expert_w_up, expert_w_down):
    bsz, seq, d = x.shape
    depth = w_ada.shape[0]
    for layer in range(depth):
        mod = _ada(c, w_ada[layer], b_ada[layer][None, :])
        mod3 = mod.reshape(bsz, 1, -1)
        x1, u2, logits = _mixer_and_router(
            x, mod3, positions, norm_mix_pre[layer], norm_mix_post[layer], norm_ffn_pre[layer], w_in[layer],
            conv_w[layer], dn_a_log[layer], dn_dt_bias[layer], dn_out_norm[layer], w_branch_moba[layer],
            w_branch_delta[layer], w_out[layer], router_group_w[layer], router_group_b[layer],
            router_expert_w[layer], router_expert_b[layer])
        out = _moe(x1, u2, logits, mod3, norm_ffn_post[layer], expert_w_gate[layer], expert_w_up[layer],
                   expert_w_down[layer], seq)
        x = out.reshape(bsz, seq, d)
    return x
```
